```python
import math
import jax, jax.numpy as jnp
from jax import lax
import numpy as np

D_MODEL = 4096
BATCH = 1
SEQ = 8192
DEPTH = 1

HEAD_DIM = 128
DIFF_V_DIM = 2 * HEAD_DIM
DIFF_HEADS = (D_MODEL // 2) // DIFF_V_DIM
DIFF_WIDTH = DIFF_HEADS * DIFF_V_DIM
NSA_HEADS = (D_MODEL - DIFF_WIDTH) // HEAD_DIM
NSA_KV_HEADS = 4
NSA_GROUP = NSA_HEADS // NSA_KV_HEADS
NSA_WIDTH = NSA_HEADS * HEAD_DIM
CMP_BLOCK = 32
CMP_STRIDE = 16
CMP_HIDDEN = 256
SEL_BLOCK = 64
SEL_TOPN = 16
WINDOW = 512
Q_BLOCK = 128
D_FF = 11008
EPS = 1e-6
NEG = -1e30
FORCE_SCORE = 1e4

SPLIT_SIZES = [
    DIFF_HEADS * 2 * HEAD_DIM,
    DIFF_HEADS * 2 * HEAD_DIM,
    DIFF_HEADS * DIFF_V_DIM,
    NSA_HEADS * HEAD_DIM,
    NSA_KV_HEADS * HEAD_DIM,
    NSA_KV_HEADS * HEAD_DIM,
    NSA_KV_HEADS * HEAD_DIM,
    NSA_KV_HEADS * HEAD_DIM,
    NSA_KV_HEADS * HEAD_DIM,
    NSA_KV_HEADS * HEAD_DIM,
    3 * NSA_HEADS,
]
N_IN = sum(SPLIT_SIZES)

kernel_name = "hybrid_diff_nsa_macaron_alibi"


def rmsnorm(x, g):
    x32 = x.astype(jnp.float32)
    y = x32 * lax.rsqrt(jnp.mean(x32 * x32, axis=-1, keepdims=True) + EPS)
    return (y * g.astype(jnp.float32)).astype(x.dtype)


def swiglu(x, w_gate, w_up, w_down):
    return (jax.nn.silu(x @ w_gate) * (x @ w_up)) @ w_down


def alibi_slopes(n):
    return 2.0 ** (-8.0 * jnp.arange(1, n + 1, dtype=jnp.float32) / n)


def lambda_init(layer):
    return 0.8 - 0.6 * math.exp(-0.3 * layer)


def diff_attention(q, k, v, lam, slopes):
    B, S, H, _, Dh = q.shape
    scale = Dh ** -0.5
    kpos = jnp.arange(S)

    def block(i):
        s0 = i * Q_BLOCK
        qb = lax.dynamic_slice_in_dim(q, s0, Q_BLOCK, axis=1)
        dist = (s0 + jnp.arange(Q_BLOCK))[:, None] - kpos[None, :]
        bias = -slopes[:, None, None, None] * dist
        s = jnp.einsum('bqhmd,bkhmd->bhmqk', qb, k) * scale + bias
        p = jax.nn.softmax(jnp.where(dist >= 0, s, NEG), axis=-1)
        a = p[:, :, 0] - lam * p[:, :, 1]
        return jnp.einsum('bhqk,bkhe->bqhe', a, v)

    o = lax.map(block, jnp.arange(S // Q_BLOCK))
    return jnp.moveaxis(o, 0, 1).reshape(B, S, H, -1)


def compress(kv, pos, w1, w2):
    B, S, Hkv, Dh = kv.shape
    n_cmp = (S - CMP_BLOCK) // CMP_STRIDE + 1
    idx = jnp.arange(n_cmp)[:, None] * CMP_STRIDE + jnp.arange(CMP_BLOCK)[None, :]
    blocks = kv[:, idx] + pos[:, None, :]
    blocks = blocks.transpose(0, 1, 3, 2, 4).reshape(B, n_cmp, Hkv, CMP_BLOCK * Dh)
    return jax.nn.gelu(blocks @ w1) @ w2


def native_sparse_attention(q, k_cmp, v_cmp, k_sel, v_sel, k_win, v_win, gates,
                            pos_k, w1_k, w2_k, pos_v, w1_v, w2_v, slopes):
    B, S, Hkv, G, Dh = q.shape
    scale = Dh ** -0.5
    kc = compress(k_cmp, pos_k, w1_k, w2_k)
    vc = compress(v_cmp, pos_v, w1_v, w2_v)
    n_cmp = kc.shape[1]
    cmp_start = jnp.arange(n_cmp) * CMP_STRIDE
    cmp_end = cmp_start + CMP_BLOCK - 1
    n_sel = S // SEL_BLOCK
    topn = min(SEL_TOPN, n_sel)
    sel_start = jnp.arange(n_sel) * SEL_BLOCK
    overlap = ((cmp_start[:, None] <= sel_start[None, :] + SEL_BLOCK - 1)
               & (cmp_end[:, None] >= sel_start[None, :])).astype(jnp.float32)
    ks_t = k_sel.reshape(B, n_sel, SEL_BLOCK, Hkv, Dh).transpose(0, 3, 1, 2, 4)
    vs_t = v_sel.reshape(B, n_sel, SEL_BLOCK, Hkv, Dh).transpose(0, 3, 1, 2, 4)
    bi = jnp.arange(B)[:, None, None, None]
    hi = jnp.arange(Hkv)[None, :, None, None]
    pad = ((0, 0), (WINDOW, 0), (0, 0), (0, 0))
    kw_pad = jnp.pad(k_win, pad)
    vw_pad = jnp.pad(v_win, pad)
    sl = slopes[:, :, None, None]
    blk_ids = jnp.arange(n_sel)

    def block(i):
        s0 = i * Q_BLOCK
        t = s0 + jnp.arange(Q_BLOCK)
        qb = lax.dynamic_slice_in_dim(q, s0, Q_BLOCK, axis=1)
        gb = lax.dynamic_slice_in_dim(gates, s0, Q_BLOCK, axis=1)
        dist_c = t[:, None] - cmp_end[None, :]
        ok_c = dist_c >= 0
        s_c = jnp.einsum('bqhgd,bchd->bhgqc', qb, kc) * scale - sl * dist_c
        p_c = jax.nn.softmax(jnp.where(ok_c, s_c, NEG), axis=-1) * ok_c
        o_c = jnp.einsum('bhgqc,bchd->bqhgd', p_c, vc)
        imp = jnp.einsum('bhgqc,cn->bhqn', p_c, overlap)
        cur = t // SEL_BLOCK
        valid = blk_ids[None, :] <= cur[:, None]
        forced = ((blk_ids[None, :] == 0) | (blk_ids[None, :] == cur[:, None])
                  | (blk_ids[None, :] == cur[:, None] - 1))
        score = jnp.where(forced, FORCE_SCORE, jnp.where(valid, imp, -1.0))
        _, sel = lax.top_k(score, topn)
        kg = ks_t[bi, hi, sel].reshape(B, Hkv, Q_BLOCK, topn * SEL_BLOCK, Dh)
        vg = vs_t[bi, hi, sel].reshape(B, Hkv, Q_BLOCK, topn * SEL_BLOCK, Dh)
        pos_s = sel[..., None] * SEL_BLOCK + jnp.arange(SEL_BLOCK)
        dist_s = (t[:, None, None] - pos_s).reshape(B, Hkv, 1, Q_BLOCK, topn * SEL_BLOCK)
        s_s = jnp.einsum('bqhgd,bhqmd->bhgqm', qb, kg) * scale - sl * dist_s
        p_s = jax.nn.softmax(jnp.where(dist_s >= 0, s_s, NEG), axis=-1)
        o_s = jnp.einsum('bhgqm,bhqmd->bqhgd', p_s, vg)
        kw = lax.dynamic_slice_in_dim(kw_pad, s0, WINDOW + Q_BLOCK, axis=1)
        vw = lax.dynamic_slice_in_dim(vw_pad, s0, WINDOW + Q_BLOCK, axis=1)
        kpos = s0 - WINDOW + jnp.arange(WINDOW + Q_BLOCK)
        dist_w = t[:, None] - kpos[None, :]
        ok_w = (dist_w >= 0) & (dist_w < WINDOW) & (kpos[None, :] >= 0)
        s_w = jnp.einsum('bqhgd,bkhd->bhgqk', qb, kw) * scale - sl * dist_w
        p_w = jax.nn.softmax(jnp.where(ok_w, s_w, NEG), axis=-1)
        o_w = jnp.einsum('bhgqk,bkhd->bqhgd', p_w, vw)
        return gb[..., 0:1] * o_c + gb[..., 1:2] * o_s + gb[..., 2:3] * o_w

    o = lax.map(block, jnp.arange(S // Q_BLOCK))
    return jnp.moveaxis(o, 0, 1).reshape(B, S, Hkv * G * Dh)


def setup_inputs(seed: int = 0) -> dict:
    key = jax.random.key(seed)
    ks = jax.random.split(key, 32)

    def nrm(k, shape, scale):
        return jax.random.normal(k, shape, jnp.float32) * scale

    def gain(k, shape):
        return 1.0 + 0.05 * jax.random.normal(k, shape, jnp.float32)

    L, D = DEPTH, D_MODEL
    cin = CMP_BLOCK * HEAD_DIM
    return {
        "x": nrm(ks[0], (BATCH, SEQ, D), 1.0),
        "ffn1_norm": gain(ks[1], (L, D)),
        "ffn1_w_gate": nrm(ks[2], (L, D, D_FF), D ** -0.5),
        "ffn1_w_up": nrm(ks[3], (L, D, D_FF), D ** -0.5),
        "ffn1_w_down": nrm(ks[4], (L, D_FF, D), D_FF ** -0.5),
        "mix_norm": gain(ks[5], (L, D)),
        "w_in": nrm(ks[6], (L, D, N_IN), D ** -0.5),
        "gate_bias": nrm(ks[7], (L, 3 * NSA_HEADS), 0.1),
        "lambda_q1": nrm(ks[8], (L, HEAD_DIM), 0.1),
        "lambda_k1": nrm(ks[9], (L, HEAD_DIM), 0.1),
        "lambda_q2": nrm(ks[10], (L, HEAD_DIM), 0.1),
        "lambda_k2": nrm(ks[11], (L, HEAD_DIM), 0.1),
        "diff_norm": gain(ks[12], (L, DIFF_V_DIM)),
        "cmp_pos_k": nrm(ks[13], (L, CMP_BLOCK, HEAD_DIM), 0.1),
        "cmp_w1_k": nrm(ks[14], (L, cin, CMP_HIDDEN), cin ** -0.5),
        "cmp_w2_k": nrm(ks[15], (L, CMP_HIDDEN, HEAD_DIM), CMP_HIDDEN ** -0.5),
        "cmp_pos_v": nrm(ks[16], (L, CMP_BLOCK, HEAD_DIM), 0.1),
        "cmp_w1_v": nrm(ks[17], (L, cin, CMP_HIDDEN), cin ** -0.5),
        "cmp_w2_v": nrm(ks[18], (L, CMP_HIDDEN, HEAD_DIM), CMP_HIDDEN ** -0.5),
        "w_out": nrm(ks[19], (L, D, D), D ** -0.5),
        "ffn2_norm": gain(ks[20], (L, D)),
        "ffn2_w_gate": nrm(ks[21], (L, D, D_FF), D ** -0.5),
        "ffn2_w_up": nrm(ks[22], (L, D, D_FF), D ** -0.5),
        "ffn2_w_down": nrm(ks[23], (L, D_FF, D), D_FF ** -0.5),
        "final_norm": gain(ks[24], (D,)),
    }


def reference(x, ffn1_norm, ffn1_w_gate, ffn1_w_up, ffn1_w_down, mix_norm, w_in, gate_bias,
              lambda_q1, lambda_k1, lambda_q2, lambda_k2, diff_norm,
              cmp_pos_k, cmp_w1_k, cmp_w2_k, cmp_pos_v, cmp_w1_v, cmp_w2_v,
              w_out, ffn2_norm, ffn2_w_gate, ffn2_w_up, ffn2_w_down, final_norm):
    B, S, _ = x.shape
    f32 = jnp.float32
    offsets = [int(o) for o in np.cumsum(SPLIT_SIZES)[:-1]]
    diff_slopes = alibi_slopes(DIFF_HEADS)
    nsa_slopes = alibi_slopes(NSA_HEADS).reshape(NSA_KV_HEADS, NSA_GROUP)
    kv_shape = (B, S, NSA_KV_HEADS, HEAD_DIM)
    h = x
    for l in range(DEPTH):
        h = h + 0.5 * swiglu(rmsnorm(h, ffn1_norm[l]), ffn1_w_gate[l], ffn1_w_up[l], ffn1_w_down[l])
        u = rmsnorm(h, mix_norm[l])
        proj = (u @ w_in[l]).astype(f32)
        dq, dk, dv, nq, kc, vc, ksl, vsl, kwn, vwn, g = jnp.split(proj, offsets, axis=-1)
        lam0 = lambda_init(l)
        lam = (jnp.exp(jnp.dot(lambda_q1[l].astype(f32), lambda_k1[l].astype(f32)))
               - jnp.exp(jnp.dot(lambda_q2[l].astype(f32), lambda_k2[l].astype(f32))) + lam0)
        o_diff = diff_attention(dq.reshape(B, S, DIFF_HEADS, 2, HEAD_DIM),
                                dk.reshape(B, S, DIFF_HEADS, 2, HEAD_DIM),
                                dv.reshape(B, S, DIFF_HEADS, DIFF_V_DIM), lam, diff_slopes)
        o_diff = rmsnorm(o_diff, diff_norm[l]) * (1.0 - lam0)
        gates = jax.nn.sigmoid(g + gate_bias[l].astype(f32)).reshape(B, S, NSA_KV_HEADS, NSA_GROUP, 3)
        o_nsa = native_sparse_attention(
            nq.reshape(B, S, NSA_KV_HEADS, NSA_GROUP, HEAD_DIM),
            kc.reshape(kv_shape), vc.reshape(kv_shape),
            ksl.reshape(kv_shape), vsl.reshape(kv_shape),
            kwn.reshape(kv_shape), vwn.reshape(kv_shape), gates,
            cmp_pos_k[l].astype(f32), cmp_w1_k[l].astype(f32), cmp_w2_k[l].astype(f32),
            cmp_pos_v[l].astype(f32), cmp_w1_v[l].astype(f32), cmp_w2_v[l].astype(f32),
            nsa_slopes)
        mix = jnp.concatenate([o_diff.reshape(B, S, DIFF_WIDTH), o_nsa], axis=-1).astype(h.dtype)
        h = h + mix @ w_out[l]
        h = h + 0.5 * swiglu(rmsnorm(h, ffn2_norm[l]), ffn2_w_gate[l], ffn2_w_up[l], ffn2_w_down[l])
    return rmsnorm(h, final_norm)
```

```python
import functools
import math

import numpy as np
import jax
import jax.numpy as jnp
from jax import lax
from jax.experimental import pallas as pl
from jax.experimental.pallas import tpu as pltpu

D_MODEL = 4096
DEPTH = 1
HEAD_DIM = 128
DIFF_V_DIM = 2 * HEAD_DIM
DIFF_HEADS = (D_MODEL // 2) // DIFF_V_DIM
DIFF_WIDTH = DIFF_HEADS * DIFF_V_DIM
NSA_HEADS = (D_MODEL - DIFF_WIDTH) // HEAD_DIM
NSA_KV_HEADS = 4
NSA_GROUP = NSA_HEADS // NSA_KV_HEADS
NSA_WIDTH = NSA_HEADS * HEAD_DIM
CMP_BLOCK = 32
CMP_STRIDE = 16
CMP_HIDDEN = 256
SEL_BLOCK = 64
SEL_TOPN = 16
WINDOW = 512
EPS = 1e-6
NEG = -1e30
FORCE_SCORE = 1e4

LANES = 128
VMEM_LIMIT = 56 * 1024 * 1024

OFF_DQ = 0
OFF_DK = OFF_DQ + DIFF_HEADS * 2 * HEAD_DIM
OFF_DV = OFF_DK + DIFF_HEADS * 2 * HEAD_DIM
OFF_NQ = OFF_DV + DIFF_HEADS * DIFF_V_DIM
OFF_KC = OFF_NQ + NSA_HEADS * HEAD_DIM
OFF_VC = OFF_KC + NSA_KV_HEADS * HEAD_DIM
OFF_KS = OFF_VC + NSA_KV_HEADS * HEAD_DIM
OFF_VS = OFF_KS + NSA_KV_HEADS * HEAD_DIM
OFF_KW = OFF_VS + NSA_KV_HEADS * HEAD_DIM
OFF_VW = OFF_KW + NSA_KV_HEADS * HEAD_DIM
OFF_G = OFF_VW + NSA_KV_HEADS * HEAD_DIM
GATES_PER_KV = 3 * NSA_GROUP

F32 = jnp.float32
BF16 = jnp.bfloat16
NT_DIMS = (((1,), (1,)), ((), ()))


def _params(n_axes):
    return pltpu.CompilerParams(dimension_semantics=("arbitrary",) * n_axes,
                                vmem_limit_bytes=VMEM_LIMIT)


def _pick(n, pref):
    b = min(pref, n)
    while n % b:
        b //= 2
    return b


def _rmsnorm_kernel(x_ref, g_ref, o_ref):
    x = x_ref[...]
    ms = jnp.mean(x * x, axis=-1, keepdims=True)
    o_ref[...] = (x * lax.rsqrt(ms + EPS) * g_ref[...]).astype(o_ref.dtype)


def _rmsnorm(x, g, out_dtype):
    s, d = x.shape
    bm = _pick(s, 256)
    return pl.pallas_call(
        _rmsnorm_kernel,
        grid=(s // bm,),
        in_specs=[pl.BlockSpec((bm, d), lambda i: (i, 0)),
                  pl.BlockSpec((1, d), lambda i: (0, 0))],
        out_specs=pl.BlockSpec((bm, d), lambda i: (i, 0)),
        out_shape=jax.ShapeDtypeStruct((s, d), out_dtype),
        compiler_params=_params(1),
        name="rmsnorm",
    )(x, g.reshape(1, d).astype(F32))


def _ffn_up_kernel(u_ref, wg_ref, wu_ref, o_ref):
    u = u_ref[...]
    g = jnp.dot(u, wg_ref[...], preferred_element_type=F32)
    up = jnp.dot(u, wu_ref[...], preferred_element_type=F32)
    o_ref[...] = (g * jax.nn.sigmoid(g) * up).astype(o_ref.dtype)


def _ffn_up(u, wg, wu):
    s, d = u.shape
    f = wg.shape[1]
    bm, bn = _pick(s, 1024), _pick(f, 512)
    return pl.pallas_call(
        _ffn_up_kernel,
        grid=(s // bm, f // bn),
        in_specs=[pl.BlockSpec((bm, d), lambda i, j: (i, 0)),
                  pl.BlockSpec((d, bn), lambda i, j: (0, j)),
                  pl.BlockSpec((d, bn), lambda i, j: (0, j))],
        out_specs=pl.BlockSpec((bm, bn), lambda i, j: (i, j)),
        out_shape=jax.ShapeDtypeStruct((s, f), BF16),
        compiler_params=_params(2),
        name="ffn_up",
    )(u, wg, wu)


def _mm_res_acc_kernel(a_ref, b_ref, r_ref, o_ref, acc_ref, *, alpha, nk):
    k = pl.program_id(2)

    @pl.when(k == 0)
    def _():
        acc_ref[...] = jnp.zeros_like(acc_ref)

    acc_ref[...] += jnp.dot(a_ref[...], b_ref[...], preferred_element_type=F32)

    @pl.when(k == nk - 1)
    def _():
        o_ref[...] = r_ref[...] + alpha * acc_ref[...]


def _mm_res_acc(a, b, res, alpha, bk_pref):
    s, kdim = a.shape
    n = b.shape[1]
    bm, bn, bk = _pick(s, 1024), _pick(n, 1024), _pick(kdim, bk_pref)
    nk = kdim // bk
    return pl.pallas_call(
        functools.partial(_mm_res_acc_kernel, alpha=alpha, nk=nk),
        grid=(s // bm, n // bn, nk),
        in_specs=[pl.BlockSpec((bm, bk), lambda i, j, k: (i, k)),
                  pl.BlockSpec((bk, bn), lambda i, j, k: (k, j)),
                  pl.BlockSpec((bm, bn), lambda i, j, k: (i, j))],
        out_specs=pl.BlockSpec((bm, bn), lambda i, j, k: (i, j)),
        out_shape=jax.ShapeDtypeStruct((s, n), F32),
        scratch_shapes=[pltpu.VMEM((bm, bn), F32)],
        compiler_params=_params(3),
        name="mm_res_acc",
    )(a, b, res)


def _mm_kernel(a_ref, b_ref, o_ref):
    o_ref[...] = jnp.dot(a_ref[...], b_ref[...], preferred_element_type=F32).astype(o_ref.dtype)


def _mm(a, b, out_dtype):
    s, kdim = a.shape
    n = b.shape[1]
    bm, bn = _pick(s, 1024), _pick(n, 512)
    return pl.pallas_call(
        _mm_kernel,
        grid=(s // bm, n // bn),
        in_specs=[pl.BlockSpec((bm, kdim), lambda i, j: (i, 0)),
                  pl.BlockSpec((kdim, bn), lambda i, j: (0, j))],
        out_specs=pl.BlockSpec((bm, bn), lambda i, j: (i, j)),
        out_shape=jax.ShapeDtypeStruct((s, n), out_dtype),
        compiler_params=_params(2),
        name="mm",
    )(a, b)


def _gate_kernel(a_ref, b_ref, bias_ref, o_ref):
    z = jnp.dot(a_ref[...], b_ref[...], preferred_element_type=F32) + bias_ref[...]
    o_ref[...] = jax.nn.sigmoid(z)


def _gate_proj(u, wg, bias):
    s, kdim = u.shape
    n = wg.shape[1]
    bm = _pick(s, 1024)
    return pl.pallas_call(
        _gate_kernel,
        grid=(s // bm,),
        in_specs=[pl.BlockSpec((bm, kdim), lambda i: (i, 0)),
                  pl.BlockSpec((kdim, n), lambda i: (0, 0)),
                  pl.BlockSpec((1, n), lambda i: (0, 0))],
        out_specs=pl.BlockSpec((bm, n), lambda i: (i, 0)),
        out_shape=jax.ShapeDtypeStruct((s, n), F32),
        compiler_params=_params(1),
        name="gate_proj",
    )(u, wg, bias)


def _mm2_res_kernel(a1_ref, a2_ref, b1_ref, b2_ref, r_ref, o_ref):
    acc = jnp.dot(a1_ref[...], b1_ref[...], preferred_element_type=F32)
    acc += jnp.dot(a2_ref[...], b2_ref[...], preferred_element_type=F32)
    o_ref[...] = r_ref[...] + acc


def _mm2_res(a1, a2, b1, b2, res):
    s, k1 = a1.shape
    k2 = a2.shape[1]
    n = b1.shape[1]
    bm, bn = _pick(s, 1024), _pick(n, 512)
    return pl.pallas_call(
        _mm2_res_kernel,
        grid=(s // bm, n // bn),
        in_specs=[pl.BlockSpec((bm, k1), lambda i, j: (i, 0)),
                  pl.BlockSpec((bm, k2), lambda i, j: (i, 0)),
                  pl.BlockSpec((k1, bn), lambda i, j: (0, j)),
                  pl.BlockSpec((k2, bn), lambda i, j: (0, j)),
                  pl.BlockSpec((bm, bn), lambda i, j: (i, j))],
        out_specs=pl.BlockSpec((bm, bn), lambda i, j: (i, j)),
        out_shape=jax.ShapeDtypeStruct((s, n), F32),
        compiler_params=_params(2),
        name="mm2_res",
    )(a1, a2, b1, b2, res)


def _tri_pairs(n):
    qi = np.repeat(np.arange(n), np.arange(1, n + 1))
    kj = np.concatenate([np.arange(i + 1) for i in range(n)])
    return jnp.asarray(qi, jnp.int32), jnp.asarray(kj, jnp.int32)


def _lanes(x, width):
    if width == LANES:
        return x
    return jnp.tile(x, (1, width // LANES))


def _online_softmax_step(s, c, v, m_ref, l_ref, acc_ref, rows):
    width = s.shape[1]
    m_prev = m_ref[rows, :]
    l_prev = l_ref[rows, :]
    m_cur = jnp.max(s, axis=1, keepdims=True) + c
    m_next = jnp.maximum(m_prev, m_cur)
    p = jnp.exp(s - _lanes(m_next - c, width))
    alpha = jnp.exp(m_prev - m_next)
    l_ref[rows, :] = alpha * l_prev + jnp.sum(p, axis=1, keepdims=True)
    m_ref[rows, :] = m_next
    pv = jnp.dot(p.astype(BF16), v, preferred_element_type=F32)
    acc_ref[rows, :] = _lanes(alpha, pv.shape[1]) * acc_ref[rows, :] + pv


def _diff_kernel(qi_ref, kj_ref, slope_ref, q_ref, k_ref, v_ref, lam_ref, g_ref, o_ref,
                 bias_ref, m_ref, l_ref, acc_ref, *, bq, lam0):
    h = pl.program_id(0)
    p = pl.program_id(1)
    qi = qi_ref[p]
    kj = kj_ref[p]
    slope = slope_ref[h]
    scale = HEAD_DIM ** -0.5

    @pl.when(p == 0)
    def _():
        rel = (lax.broadcasted_iota(jnp.int32, (bq, bq), 0)
               - lax.broadcasted_iota(jnp.int32, (bq, bq), 1))
        b = -slope * rel.astype(F32)
        bias_ref[0] = b
        bias_ref[1] = jnp.where(rel >= 0, b, NEG)

    @pl.when(kj == 0)
    def _():
        m_ref[...] = jnp.full_like(m_ref, NEG)
        l_ref[...] = jnp.zeros_like(l_ref)
        acc_ref[...] = jnp.zeros_like(acc_ref)

    is_diag = kj == qi
    bias = bias_ref[is_diag.astype(jnp.int32)]
    c = -slope * ((qi - kj) * bq).astype(F32)
    v = v_ref[...]
    for mp in range(2):
        q = q_ref[:, mp * HEAD_DIM:(mp + 1) * HEAD_DIM]
        k = k_ref[:, mp * HEAD_DIM:(mp + 1) * HEAD_DIM]
        s = lax.dot_general(q, k, NT_DIMS, preferred_element_type=F32) * scale + bias
        _online_softmax_step(s, c, v, m_ref, l_ref, acc_ref, pl.ds(mp * bq, bq))

    @pl.when(is_diag)
    def _():
        lam_rows = lam_ref[...]
        d1 = jnp.sum(lam_rows[0:1] * lam_rows[1:2], axis=1, keepdims=True)
        d2 = jnp.sum(lam_rows[2:3] * lam_rows[3:4], axis=1, keepdims=True)
        lam = jnp.exp(d1) - jnp.exp(d2) + lam0
        o1 = acc_ref[0:bq, :] / _lanes(l_ref[0:bq, :], DIFF_V_DIM)
        o2 = acc_ref[bq:2 * bq, :] / _lanes(l_ref[bq:2 * bq, :], DIFF_V_DIM)
        o = o1 - lam * o2
        ms = jnp.mean(o * o, axis=-1, keepdims=True)
        y = o * lax.rsqrt(ms + EPS) * g_ref[...]
        o_ref[...] = (y * (1.0 - lam0)).astype(o_ref.dtype)


def _diff_attention(proj, lam_rows, gain, layer, bq):
    s = proj.shape[0]
    nq = s // bq
    qi, kj = _tri_pairs(nq)
    slopes = jnp.asarray(2.0 ** (-8.0 * np.arange(1, DIFF_HEADS + 1) / DIFF_HEADS), F32)
    lam0 = 0.8 - 0.6 * math.exp(-0.3 * layer)
    w = DIFF_V_DIM
    grid_spec = pltpu.PrefetchScalarGridSpec(
        num_scalar_prefetch=3,
        grid=(DIFF_HEADS, qi.shape[0]),
        in_specs=[
            pl.BlockSpec((bq, w), lambda h, p, qi, kj, sl: (qi[p], OFF_DQ // w + h)),
            pl.BlockSpec((bq, w), lambda h, p, qi, kj, sl: (kj[p], OFF_DK // w + h)),
            pl.BlockSpec((bq, w), lambda h, p, qi, kj, sl: (kj[p], OFF_DV // w + h)),
            pl.BlockSpec((8, HEAD_DIM), lambda h, p, qi, kj, sl: (0, 0)),
            pl.BlockSpec((1, w), lambda h, p, qi, kj, sl: (0, 0)),
        ],
        out_specs=pl.BlockSpec((bq, w), lambda h, p, qi, kj, sl: (qi[p], h)),
        scratch_shapes=[pltpu.VMEM((2, bq, bq), F32),
                        pltpu.VMEM((2 * bq, LANES), F32),
                        pltpu.VMEM((2 * bq, LANES), F32),
                        pltpu.VMEM((2 * bq, w), F32)],
    )
    return pl.pallas_call(
        functools.partial(_diff_kernel, bq=bq, lam0=lam0),
        grid_spec=grid_spec,
        out_shape=jax.ShapeDtypeStruct((s, DIFF_WIDTH), BF16),
        compiler_params=_params(2),
        name="diff_attention",
    )(qi, kj, slopes, proj, proj, proj, lam_rows, gain)


def _compress_kernel(a_ref, pos_ref, w1_ref, w2_ref, o_ref):
    half = CMP_STRIDE * HEAD_DIM
    a = a_ref[...].astype(F32)
    pos = pos_ref[...]
    top = (a + pos[:, :half]).astype(BF16)
    bot = (a + pos[:, half:]).astype(BF16)
    t = jnp.dot(top, w1_ref[:half, :], preferred_element_type=F32)
    b = jnp.dot(bot, w1_ref[half:, :], preferred_element_type=F32)
    rows = a.shape[0]
    hid = t + pltpu.roll(b, rows - 1, 0)
    act = jax.nn.gelu(hid)
    o_ref[...] = jnp.dot(act.astype(BF16), w2_ref[...], preferred_element_type=F32).astype(o_ref.dtype)


def _compress(a, pos, w1, w2):
    _, hkv, rows, width = a.shape
    return pl.pallas_call(
        _compress_kernel,
        grid=(2, hkv),
        in_specs=[pl.BlockSpec((None, None, rows, width), lambda t, h: (t, h, 0, 0)),
                  pl.BlockSpec((None, 1, 2 * width), lambda t, h: (t, 0, 0)),
                  pl.BlockSpec((None, 2 * width, CMP_HIDDEN), lambda t, h: (t, 0, 0)),
                  pl.BlockSpec((None, CMP_HIDDEN, HEAD_DIM), lambda t, h: (t, 0, 0))],
        out_specs=pl.BlockSpec((None, None, rows, HEAD_DIM), lambda t, h: (t, h, 0, 0)),
        out_shape=jax.ShapeDtypeStruct((2, hkv, rows, HEAD_DIM), BF16),
        compiler_params=_params(2),
        name="nsa_compress",
    )(a, pos, w1, w2)


def _split3(x):
    hi = x.astype(BF16)
    r1 = x - hi.astype(F32)
    mid = r1.astype(BF16)
    lo = (r1 - mid.astype(F32)).astype(BF16)
    return hi, mid, lo


def _stack_heads(q):
    return jnp.concatenate([q[:, g * HEAD_DIM:(g + 1) * HEAD_DIM] for g in range(NSA_GROUP)], axis=0)


def _cmp_select_kernel(slope_ref, q_ref, kc_ref, vc_ref, ov_ref, oc_ref, sel_ref, *, bq, n_cmp, topn):
    hkv = pl.program_id(0)
    i = pl.program_id(1)
    scale = HEAD_DIM ** -0.5
    ncp = kc_ref.shape[0]
    q = _stack_heads(q_ref[...])
    s = lax.dot_general(q, kc_ref[...], NT_DIMS, preferred_element_type=F32) * scale
    t = i * bq + lax.broadcasted_iota(jnp.int32, (bq, ncp), 0)
    cidx = lax.broadcasted_iota(jnp.int32, (bq, ncp), 1)
    dist = t - (cidx * CMP_STRIDE + CMP_BLOCK - 1)
    ok = (dist >= 0) & (cidx < n_cmp)
    distf = dist.astype(F32)
    vc = vc_ref[...]
    psum = jnp.zeros((bq, ncp), F32)
    for g in range(NSA_GROUP):
        slope = slope_ref[hkv * NSA_GROUP + g]
        sg = jnp.where(ok, s[g * bq:(g + 1) * bq] - slope * distf, NEG)
        mx = jnp.max(sg, axis=1, keepdims=True)
        e = jnp.where(ok, jnp.exp(sg - mx), 0.0)
        den = jnp.sum(e, axis=1, keepdims=True)
        pg = e / jnp.where(den > 0.0, den, 1.0)
        oc_ref[:, g * HEAD_DIM:(g + 1) * HEAD_DIM] = jnp.dot(
            pg.astype(BF16), vc, preferred_element_type=F32)
        psum = psum + pg
    ov = ov_ref[...]
    imp = jnp.zeros((bq, LANES), F32)
    for piece in _split3(psum):
        imp = imp + jnp.dot(piece, ov, preferred_element_type=F32)
    tq = i * bq + lax.broadcasted_iota(jnp.int32, (bq, LANES), 0)
    blk = lax.broadcasted_iota(jnp.int32, (bq, LANES), 1)
    cur = lax.shift_right_arithmetic(tq, SEL_BLOCK.bit_length() - 1)
    forced = (blk == 0) | (blk == cur) | (blk == cur - 1)
    score = jnp.where(forced, FORCE_SCORE, jnp.where(blk <= cur, imp, -1.0))
    removed = -3.0e38
    blkf = blk.astype(F32)

    def pick(_, carry):
        sc, sel = carry
        mx = jnp.max(sc, axis=1, keepdims=True)
        first = jnp.min(jnp.where(sc == mx, blkf, float(LANES)), axis=1, keepdims=True)
        hit = blkf == first
        return jnp.where(hit, removed, sc), jnp.where(hit, 1.0, sel)

    _, sel = lax.fori_loop(0, topn, pick, (score, jnp.zeros((bq, LANES), F32)))
    sel_ref[...] = sel.astype(sel_ref.dtype)


def _cmp_select(proj, kvc, overlap, slopes, bq):
    s = proj.shape[0]
    n_cmp = (s - CMP_BLOCK) // CMP_STRIDE + 1
    n_sel = s // SEL_BLOCK
    topn = min(SEL_TOPN, n_sel)
    ncp = kvc.shape[2]
    gw = NSA_GROUP * HEAD_DIM
    grid_spec = pltpu.PrefetchScalarGridSpec(
        num_scalar_prefetch=1,
        grid=(NSA_KV_HEADS, s // bq),
        in_specs=[
            pl.BlockSpec((bq, gw), lambda h, i, sl: (i, OFF_NQ // gw + h)),
            pl.BlockSpec((None, None, ncp, HEAD_DIM), lambda h, i, sl: (0, h, 0, 0)),
            pl.BlockSpec((None, None, ncp, HEAD_DIM), lambda h, i, sl: (1, h, 0, 0)),
            pl.BlockSpec((ncp, LANES), lambda h, i, sl: (0, 0)),
        ],
        out_specs=[pl.BlockSpec((bq, gw), lambda h, i, sl: (i, h)),
                   pl.BlockSpec((None, bq, LANES), lambda h, i, sl: (h, i, 0))],
    )
    return pl.pallas_call(
        functools.partial(_cmp_select_kernel, bq=bq, n_cmp=n_cmp, topn=topn),
        grid_spec=grid_spec,
        out_shape=[jax.ShapeDtypeStruct((s, NSA_WIDTH), F32),
                   jax.ShapeDtypeStruct((NSA_KV_HEADS, s, LANES), BF16)],
        compiler_params=_params(2),
        name="nsa_cmp_select",
    )(slopes, proj, kvc, kvc, overlap)


def _sel_kernel(qi_ref, kj_ref, slope_ref, q_ref, k_ref, v_ref, sel_ref, o_ref,
                bias_ref, m_ref, l_ref, acc_ref, *, bq):
    hkv = pl.program_id(0)
    p = pl.program_id(1)
    qi = qi_ref[p]
    kj = kj_ref[p]
    scale = HEAD_DIM ** -0.5

    @pl.when(p == 0)
    def _():
        rel = (lax.broadcasted_iota(jnp.int32, (bq, bq), 0)
               - lax.broadcasted_iota(jnp.int32, (bq, bq), 1)).astype(F32)
        for g in range(NSA_GROUP):
            bias_ref[g] = -slope_ref[hkv * NSA_GROUP + g] * rel

    @pl.when(kj == 0)
    def _():
        m_ref[...] = jnp.full_like(m_ref, NEG)
        l_ref[...] = jnp.zeros_like(l_ref)
        acc_ref[...] = jnp.zeros_like(acc_ref)

    blk = lax.broadcasted_iota(jnp.int32, (LANES, bq), 0)
    key = kj * bq + lax.broadcasted_iota(jnp.int32, (LANES, bq), 1)
    key_blk = lax.shift_right_arithmetic(key, SEL_BLOCK.bit_length() - 1)
    expand = jnp.where(blk == key_blk, 1.0, 0.0).astype(BF16)
    member = jnp.dot(sel_ref[...], expand, preferred_element_type=F32)
    dist = (lax.broadcasted_iota(jnp.int32, (bq, bq), 0)
            - lax.broadcasted_iota(jnp.int32, (bq, bq), 1)) + (qi - kj) * bq
    maskb = jnp.where(dist >= 0, jnp.where(member > 0.5, 0.0, NEG), NEG)

    q = _stack_heads(q_ref[...])
    s_all = lax.dot_general(q, k_ref[...], NT_DIMS, preferred_element_type=F32) * scale
    v = v_ref[...]
    dtile = ((qi - kj) * bq).astype(F32)
    for g in range(NSA_GROUP):
        c = -slope_ref[hkv * NSA_GROUP + g] * dtile
        s = s_all[g * bq:(g + 1) * bq] + bias_ref[g] + maskb
        _online_softmax_step(s, c, v, m_ref, l_ref, acc_ref, pl.ds(g * bq, bq))

    @pl.when(kj == qi)
    def _():
        for g in range(NSA_GROUP):
            rows = pl.ds(g * bq, bq)
            o_ref[:, g * HEAD_DIM:(g + 1) * HEAD_DIM] = acc_ref[rows, :] / l_ref[rows, :]


def _sel_attention(proj, sel, slopes, bq):
    s = proj.shape[0]
    qi, kj = _tri_pairs(s // bq)
    gw = NSA_GROUP * HEAD_DIM
    d = HEAD_DIM
    grid_spec = pltpu.PrefetchScalarGridSpec(
        num_scalar_prefetch=3,
        grid=(NSA_KV_HEADS, qi.shape[0]),
        in_specs=[
            pl.BlockSpec((bq, gw), lambda h, p, qi, kj, sl: (qi[p], OFF_NQ // gw + h)),
            pl.BlockSpec((bq, d), lambda h, p, qi, kj, sl: (kj[p], OFF_KS // d + h)),
            pl.BlockSpec((bq, d), lambda h, p, qi, kj, sl: (kj[p], OFF_VS // d + h)),
            pl.BlockSpec((None, bq, LANES), lambda h, p, qi, kj, sl: (h, qi[p], 0)),
        ],
        out_specs=pl.BlockSpec((bq, gw), lambda h, p, qi, kj, sl: (qi[p], h)),
        scratch_shapes=[pltpu.VMEM((NSA_GROUP, bq, bq), F32),
                        pltpu.VMEM((NSA_GROUP * bq, LANES), F32),
                        pltpu.VMEM((NSA_GROUP * bq, LANES), F32),
                        pltpu.VMEM((NSA_GROUP * bq, d), F32)],
    )
    return pl.pallas_call(
        functools.partial(_sel_kernel, bq=bq),
        grid_spec=grid_spec,
        out_shape=jax.ShapeDtypeStruct((s, NSA_WIDTH), F32),
        compiler_params=_params(2),
        name="nsa_selected",
    )(qi, kj, slopes, proj, proj, proj, sel)


def _win_kernel(slope_ref, q_ref, kp_ref, kc_ref, vp_ref, vc_ref, oc_ref, os_ref, gate_ref, o_ref,
                bias_ref):
    hkv = pl.program_id(0)
    i = pl.program_id(1)
    bq = WINDOW
    scale = HEAD_DIM ** -0.5

    @pl.when(i == 0)
    def _():
        rel = (lax.broadcasted_iota(jnp.int32, (bq, bq), 0)
               - lax.broadcasted_iota(jnp.int32, (bq, bq), 1))
        relf = rel.astype(F32)
        for g in range(NSA_GROUP):
            slope = slope_ref[hkv * NSA_GROUP + g]
            bias_ref[g, 0] = jnp.where(rel < 0, -slope * (relf + WINDOW), NEG)
            bias_ref[g, 1] = jnp.where(rel >= 0, -slope * relf, NEG)

    q = _stack_heads(q_ref[...])
    sp_all = lax.dot_general(q, kp_ref[...], NT_DIMS, preferred_element_type=F32) * scale
    sc_all = lax.dot_general(q, kc_ref[...], NT_DIMS, preferred_element_type=F32) * scale
    first = jnp.where(i == 0, NEG, 0.0)
    vp = vp_ref[...]
    vc = vc_ref[...]
    gates = gate_ref[...]
    for g in range(NSA_GROUP):
        rows = slice(g * bq, (g + 1) * bq)
        cols = slice(g * HEAD_DIM, (g + 1) * HEAD_DIM)
        sp = sp_all[rows] + bias_ref[g, 0] + first
        sc = sc_all[rows] + bias_ref[g, 1]
        mx = jnp.maximum(jnp.max(sp, axis=1, keepdims=True), jnp.max(sc, axis=1, keepdims=True))
        pp = jnp.exp(sp - mx)
        pc = jnp.exp(sc - mx)
        den = jnp.sum(pp, axis=1, keepdims=True) + jnp.sum(pc, axis=1, keepdims=True)
        ow = (jnp.dot(pp.astype(BF16), vp, preferred_element_type=F32)
              + jnp.dot(pc.astype(BF16), vc, preferred_element_type=F32)) / den
        gc = gates[:, 3 * g + 0:3 * g + 1]
        gs = gates[:, 3 * g + 1:3 * g + 2]
        gw = gates[:, 3 * g + 2:3 * g + 3]
        o_ref[:, cols] = (gc * oc_ref[:, cols] + gs * os_ref[:, cols] + gw * ow).astype(o_ref.dtype)


def _win_combine(proj, o_cmp, o_sel, gates, slopes):
    s = proj.shape[0]
    bq = WINDOW
    gw = NSA_GROUP * HEAD_DIM
    d = HEAD_DIM
    prev = lambda i: jnp.maximum(i - 1, 0)
    grid_spec = pltpu.PrefetchScalarGridSpec(
        num_scalar_prefetch=1,
        grid=(NSA_KV_HEADS, s // bq),
        in_specs=[
            pl.BlockSpec((bq, gw), lambda h, i, sl: (i, OFF_NQ // gw + h)),
            pl.BlockSpec((bq, d), lambda h, i, sl: (prev(i), OFF_KW // d + h)),
            pl.BlockSpec((bq, d), lambda h, i, sl: (i, OFF_KW // d + h)),
            pl.BlockSpec((bq, d), lambda h, i, sl: (prev(i), OFF_VW // d + h)),
            pl.BlockSpec((bq, d), lambda h, i, sl: (i, OFF_VW // d + h)),
            pl.BlockSpec((bq, gw), lambda h, i, sl: (i, h)),
            pl.BlockSpec((bq, gw), lambda h, i, sl: (i, h)),
            pl.BlockSpec((bq, LANES), lambda h, i, sl: (i, h)),
        ],
        out_specs=pl.BlockSpec((bq, gw), lambda h, i, sl: (i, h)),
        scratch_shapes=[pltpu.VMEM((NSA_GROUP, 2, bq, bq), F32)],
    )
    return pl.pallas_call(
        _win_kernel,
        grid_spec=grid_spec,
        out_shape=jax.ShapeDtypeStruct((s, NSA_WIDTH), BF16),
        compiler_params=_params(2),
        name="nsa_window_combine",
    )(slopes, proj, proj, proj, proj, proj, o_cmp, o_sel, gates)


def _pad_axis(w, axis, mult):
    pad = (-w.shape[axis]) % mult
    if not pad:
        return w
    widths = [(0, 0)] * w.ndim
    widths[axis] = (0, pad)
    return jnp.pad(w, widths)


def _ffn(h, norm_g, w_gate, w_up, w_down):
    u = _rmsnorm(h, norm_g, BF16)
    wg = _pad_axis(w_gate.astype(BF16), 1, 1024)
    wu = _pad_axis(w_up.astype(BF16), 1, 1024)
    wd = _pad_axis(w_down.astype(BF16), 0, 1024)
    act = _ffn_up(u, wg, wu)
    return _mm_res_acc(act, wd, h, 0.5, wd.shape[0] // 4)


def _overlap_matrix(s, rows):
    n_cmp = (s - CMP_BLOCK) // CMP_STRIDE + 1
    n_sel = s // SEL_BLOCK
    cs = np.arange(rows)[:, None] * CMP_STRIDE
    ss = np.arange(LANES)[None, :] * SEL_BLOCK
    ov = (cs <= ss + SEL_BLOCK - 1) & (cs + CMP_BLOCK - 1 >= ss)
    ov &= (np.arange(rows)[:, None] < n_cmp) & (np.arange(LANES)[None, :] < n_sel)
    return jnp.asarray(ov, BF16)


def _layer(h, layer, ffn1_norm, ffn1_w_gate, ffn1_w_up, ffn1_w_down, mix_norm, w_in, gate_bias,
           lambda_q1, lambda_k1, lambda_q2, lambda_k2, diff_norm,
           cmp_pos_k, cmp_w1_k, cmp_w2_k, cmp_pos_v, cmp_w1_v, cmp_w2_v,
           w_out, ffn2_norm, ffn2_w_gate, ffn2_w_up, ffn2_w_down):
    s = h.shape[0]
    assert s % WINDOW == 0 and s // SEL_BLOCK <= LANES and OFF_G % 512 == 0
    h = _ffn(h, ffn1_norm, ffn1_w_gate, ffn1_w_up, ffn1_w_down)

    u = _rmsnorm(h, mix_norm, BF16)
    proj = _mm(u, w_in[:, :OFF_G].astype(BF16), BF16)
    wg = w_in[:, OFF_G:].reshape(-1, NSA_KV_HEADS, GATES_PER_KV)
    wg = jnp.pad(wg, ((0, 0), (0, 0), (0, LANES - GATES_PER_KV))).reshape(-1, NSA_KV_HEADS * LANES)
    gb = jnp.pad(gate_bias.astype(F32).reshape(NSA_KV_HEADS, GATES_PER_KV),
                 ((0, 0), (0, LANES - GATES_PER_KV))).reshape(1, NSA_KV_HEADS * LANES)
    gates = _gate_proj(u, wg.astype(BF16), gb)

    lam_rows = jnp.pad(jnp.stack([lambda_q1, lambda_k1, lambda_q2, lambda_k2]).astype(F32),
                       ((0, 4), (0, 0)))
    o_diff = _diff_attention(proj, lam_rows, diff_norm.reshape(1, -1).astype(F32), layer,
                             _pick(s, 512))

    nsa_slopes = jnp.asarray(2.0 ** (-8.0 * np.arange(1, NSA_HEADS + 1) / NSA_HEADS), F32)
    rows = s // CMP_STRIDE
    kv_cmp = proj[:, OFF_KC:OFF_KS].reshape(rows, CMP_STRIDE, 2, NSA_KV_HEADS, HEAD_DIM)
    kv_cmp = kv_cmp.transpose(2, 3, 0, 1, 4).reshape(2, NSA_KV_HEADS, rows, CMP_STRIDE * HEAD_DIM)
    pos = jnp.stack([cmp_pos_k, cmp_pos_v]).astype(F32).reshape(2, 1, CMP_BLOCK * HEAD_DIM)
    w1 = jnp.stack([cmp_w1_k, cmp_w1_v]).astype(BF16)
    w2 = jnp.stack([cmp_w2_k, cmp_w2_v]).astype(BF16)
    kvc = _compress(kv_cmp, pos, w1, w2)
    o_cmp, sel = _cmp_select(proj, kvc, _overlap_matrix(s, rows), nsa_slopes, _pick(s, 256))
    o_sel = _sel_attention(proj, sel, nsa_slopes, _pick(s, 512))
    o_nsa = _win_combine(proj, o_cmp, o_sel, gates, nsa_slopes)

    wo = w_out.astype(BF16)
    h = _mm2_res(o_diff, o_nsa, wo[:DIFF_WIDTH], wo[DIFF_WIDTH:], h)
    return _ffn(h, ffn2_norm, ffn2_w_gate, ffn2_w_up, ffn2_w_down)


def kernel(x, ffn1_norm, ffn1_w_gate, ffn1_w_up, ffn1_w_down, mix_norm, w_in, gate_bias, lambda_q1, lambda_k1, lambda_q2, lambda_k2, diff_norm, cmp_pos_k, cmp_w1_k, cmp_w2_k, cmp_pos_v, cmp_w1_v, cmp_w2_v, w_out, ffn2_norm, ffn2_w_gate, ffn2_w_up, ffn2_w_down, final_norm):
    b, s, d = x.shape
    per_layer = (ffn1_norm, ffn1_w_gate, ffn1_w_up, ffn1_w_down, mix_norm, w_in, gate_bias,
                 lambda_q1, lambda_k1, lambda_q2, lambda_k2, diff_norm,
                 cmp_pos_k, cmp_w1_k, cmp_w2_k, cmp_pos_v, cmp_w1_v, cmp_w2_v,
                 w_out, ffn2_norm, ffn2_w_gate, ffn2_w_up, ffn2_w_down)
    outs = []
    for bi in range(b):
        h = x[bi]
        for layer in range(DEPTH):
            h = _layer(h, layer, *[p[layer] for p in per_layer])
        outs.append(_rmsnorm(h, final_norm, x.dtype))
    return jnp.stack(outs)
```

```python
import functools
import math

import ml_dtypes
import numpy as np
import jax
import jax.numpy as jnp
from jax import lax
from jax.experimental import pallas as pl
from jax.experimental.pallas import tpu as pltpu

D_MODEL = 4096
DEPTH = 1
HEAD_DIM = 128
DIFF_V_DIM = 2 * HEAD_DIM
DIFF_HEADS = (D_MODEL // 2) // DIFF_V_DIM
DIFF_WIDTH = DIFF_HEADS * DIFF_V_DIM
NSA_HEADS = (D_MODEL - DIFF_WIDTH) // HEAD_DIM
NSA_KV_HEADS = 4
NSA_GROUP = NSA_HEADS // NSA_KV_HEADS
NSA_WIDTH = NSA_HEADS * HEAD_DIM
CMP_BLOCK = 32
CMP_STRIDE = 16
CMP_HIDDEN = 256
SEL_BLOCK = 64
SEL_TOPN = 16
WINDOW = 512
EPS = 1e-6
NEG = -1e30
FORCE_SCORE = 1e4

LANES = 128
VMEM_LIMIT = 56 * 1024 * 1024
LOG2E = 1.4426950408889634
QSCALE = HEAD_DIM ** -0.5 * LOG2E
MASK_BIG = 2.0 ** 100
KPOS_SPLIT = 32

OFF_DQ = 0
OFF_DK = OFF_DQ + DIFF_HEADS * 2 * HEAD_DIM
OFF_DV = OFF_DK + DIFF_HEADS * 2 * HEAD_DIM
OFF_NQ = OFF_DV + DIFF_HEADS * DIFF_V_DIM
OFF_KC = OFF_NQ + NSA_HEADS * HEAD_DIM
OFF_VC = OFF_KC + NSA_KV_HEADS * HEAD_DIM
OFF_KS = OFF_VC + NSA_KV_HEADS * HEAD_DIM
OFF_VS = OFF_KS + NSA_KV_HEADS * HEAD_DIM
OFF_KW = OFF_VS + NSA_KV_HEADS * HEAD_DIM
OFF_VW = OFF_KW + NSA_KV_HEADS * HEAD_DIM
OFF_G = OFF_VW + NSA_KV_HEADS * HEAD_DIM
GATES_PER_KV = 3 * NSA_GROUP

F32 = jnp.float32
BF16 = jnp.bfloat16
NT_DIMS = (((1,), (1,)), ((), ()))


def _params(n_axes):
    return pltpu.CompilerParams(dimension_semantics=("arbitrary",) * n_axes,
                                vmem_limit_bytes=VMEM_LIMIT)


def _pick(n, pref):
    b = min(pref, n)
    while n % b:
        b //= 2
    return b


def _rmsnorm_kernel(x_ref, g_ref, o_ref):
    x = x_ref[...]
    ms = jnp.mean(x * x, axis=-1, keepdims=True)
    o_ref[...] = (x * lax.rsqrt(ms + EPS) * g_ref[...]).astype(o_ref.dtype)


def _rmsnorm(x, g, out_dtype):
    s, d = x.shape
    bm = _pick(s, 256)
    return pl.pallas_call(
        _rmsnorm_kernel,
        grid=(s // bm,),
        in_specs=[pl.BlockSpec((bm, d), lambda i: (i, 0)),
                  pl.BlockSpec((1, d), lambda i: (0, 0))],
        out_specs=pl.BlockSpec((bm, d), lambda i: (i, 0)),
        out_shape=jax.ShapeDtypeStruct((s, d), out_dtype),
        compiler_params=_params(1),
        name="rmsnorm",
    )(x, g.reshape(1, d).astype(F32))


def _ffn_up_kernel(u_ref, wg_ref, wu_ref, o_ref):
    u = u_ref[...]
    g = jnp.dot(u, wg_ref[...], preferred_element_type=F32)
    up = jnp.dot(u, wu_ref[...], preferred_element_type=F32)
    o_ref[...] = (g * jax.nn.sigmoid(g) * up).astype(o_ref.dtype)


def _ffn_up(u, wg, wu):
    s, d = u.shape
    f = wg.shape[1]
    bm, bn = _pick(s, 1024), min(512, f)
    return pl.pallas_call(
        _ffn_up_kernel,
        grid=(s // bm, pl.cdiv(f, bn)),
        in_specs=[pl.BlockSpec((bm, d), lambda i, j: (i, 0)),
                  pl.BlockSpec((d, bn), lambda i, j: (0, j)),
                  pl.BlockSpec((d, bn), lambda i, j: (0, j))],
        out_specs=pl.BlockSpec((bm, bn), lambda i, j: (i, j)),
        out_shape=jax.ShapeDtypeStruct((s, f), BF16),
        compiler_params=_params(2),
        name="ffn_up",
    )(u, wg, wu)


def _mm_res_kernel(a_ref, b_ref, r_ref, o_ref, *, alpha):
    o_ref[...] = r_ref[...] + alpha * jnp.dot(a_ref[...], b_ref[...], preferred_element_type=F32)


def _mm_res(a, b, res, alpha):
    s, kdim = a.shape
    n = b.shape[1]
    bm, bn = _pick(s, 512), _pick(n, 512)
    return pl.pallas_call(
        functools.partial(_mm_res_kernel, alpha=alpha),
        grid=(s // bm, n // bn),
        in_specs=[pl.BlockSpec((bm, kdim), lambda i, j: (i, 0)),
                  pl.BlockSpec((kdim, bn), lambda i, j: (0, j)),
                  pl.BlockSpec((bm, bn), lambda i, j: (i, j))],
        out_specs=pl.BlockSpec((bm, bn), lambda i, j: (i, j)),
        out_shape=jax.ShapeDtypeStruct((s, n), F32),
        compiler_params=_params(2),
        name="mm_res",
    )(a, b, res)


def _mm_scaled_kernel(a_ref, b_ref, cs_ref, o_ref):
    acc = jnp.dot(a_ref[...], b_ref[...], preferred_element_type=F32)
    o_ref[...] = (acc * cs_ref[...]).astype(o_ref.dtype)


def _mm_scaled(a, b, colscale, out_dtype):
    s, kdim = a.shape
    n = b.shape[1]
    bm, bn = _pick(s, 1024), _pick(n, 512)
    return pl.pallas_call(
        _mm_scaled_kernel,
        grid=(s // bm, n // bn),
        in_specs=[pl.BlockSpec((bm, kdim), lambda i, j: (i, 0)),
                  pl.BlockSpec((kdim, bn), lambda i, j: (0, j)),
                  pl.BlockSpec((1, bn), lambda i, j: (0, j))],
        out_specs=pl.BlockSpec((bm, bn), lambda i, j: (i, j)),
        out_shape=jax.ShapeDtypeStruct((s, n), out_dtype),
        compiler_params=_params(2),
        name="mm_scaled",
    )(a, b, colscale)


def _gate_kernel(a_ref, b_ref, bias_ref, o_ref):
    z = jnp.dot(a_ref[...], b_ref[...], preferred_element_type=F32) + bias_ref[...]
    o_ref[...] = jax.nn.sigmoid(z)


def _gate_proj(u, wg, bias):
    s, kdim = u.shape
    n = wg.shape[1]
    bm = _pick(s, 1024)
    return pl.pallas_call(
        _gate_kernel,
        grid=(s // bm,),
        in_specs=[pl.BlockSpec((bm, kdim), lambda i: (i, 0)),
                  pl.BlockSpec((kdim, n), lambda i: (0, 0)),
                  pl.BlockSpec((1, n), lambda i: (0, 0))],
        out_specs=pl.BlockSpec((bm, n), lambda i: (i, 0)),
        out_shape=jax.ShapeDtypeStruct((s, n), F32),
        compiler_params=_params(1),
        name="gate_proj",
    )(u, wg, bias)


def _mm2_res_kernel(a1_ref, a2_ref, b1_ref, b2_ref, r_ref, o_ref):
    acc = jnp.dot(a1_ref[...], b1_ref[...], preferred_element_type=F32)
    acc += jnp.dot(a2_ref[...], b2_ref[...], preferred_element_type=F32)
    o_ref[...] = r_ref[...] + acc


def _mm2_res(a1, a2, b1, b2, res):
    s, k1 = a1.shape
    k2 = a2.shape[1]
    n = b1.shape[1]
    bm, bn = _pick(s, 1024), _pick(n, 512)
    return pl.pallas_call(
        _mm2_res_kernel,
        grid=(s // bm, n // bn),
        in_specs=[pl.BlockSpec((bm, k1), lambda i, j: (i, 0)),
                  pl.BlockSpec((bm, k2), lambda i, j: (i, 0)),
                  pl.BlockSpec((k1, bn), lambda i, j: (0, j)),
                  pl.BlockSpec((k2, bn), lambda i, j: (0, j)),
                  pl.BlockSpec((bm, bn), lambda i, j: (i, j))],
        out_specs=pl.BlockSpec((bm, bn), lambda i, j: (i, j)),
        out_shape=jax.ShapeDtypeStruct((s, n), F32),
        compiler_params=_params(2),
        name="mm2_res",
    )(a1, a2, b1, b2, res)


def _tri_pairs(n):
    qi = np.repeat(np.arange(n), np.arange(1, n + 1))
    kj = np.concatenate([np.arange(i + 1) for i in range(n)])
    return jnp.asarray(qi, jnp.int32), jnp.asarray(kj, jnp.int32)


def _bf16_round(x):
    return np.asarray(x, np.float32).astype(ml_dtypes.bfloat16).astype(np.float64)


def _alibi_consts(n_heads):
    c = np.float32(2.0 ** (-8.0 * np.arange(1, n_heads + 1) / n_heads) * LOG2E).astype(np.float64)
    hi = _bf16_round(c)
    mid = _bf16_round(c - hi)
    lo = _bf16_round(c - hi - mid)
    rows = np.zeros((n_heads, 1, LANES), np.float32)
    for col, piece in enumerate((hi, hi, mid, mid, lo, lo)):
        rows[:, 0, col] = piece
    return jnp.asarray(c, F32), jnp.asarray(rows, BF16)


def _kpos_ext(bk):
    pos = np.arange(bk)
    a = (pos // KPOS_SPLIT) * KPOS_SPLIT
    b = pos % KPOS_SPLIT
    ext = np.zeros((bk, LANES), np.float32)
    for col in range(0, 6, 2):
        ext[:, col] = a
        ext[:, col + 1] = b
    return jnp.asarray(ext, BF16)


def _one_col(rows):
    x = np.zeros((rows, LANES), np.float32)
    x[:, 0] = 1.0
    return jnp.asarray(x, BF16)


def _causal_tile(b):
    r = np.arange(b)
    return jnp.asarray(np.where(r[:, None] >= r[None, :], 0.0, NEG), F32)


def _lanes(x, width):
    if width == LANES:
        return x
    return jnp.tile(x, (1, width // LANES))


def _flash_update(s, c, m_ref, rows):
    m_prev = m_ref[rows, :]
    m_cur = jnp.max(s, axis=1, keepdims=True) + c
    m_next = jnp.maximum(m_prev, m_cur)
    m_ref[rows, :] = m_next
    p = jnp.exp2(s - _lanes(m_next - c, s.shape[1]))
    return p, jnp.exp2(m_prev - m_next)


def _diff_kernel(qi_ref, kj_ref, c_ref, q_ref, k_ref, v_ref, qext_ref, kext_ref, causal_ref,
                 lam_ref, g_ref, o_ref, qa_ref, m_ref, l_ref, acc_ref, *, bq, lam0):
    h = pl.program_id(0)
    p = pl.program_id(1)
    qi = qi_ref[p]
    kj = kj_ref[p]

    @pl.when(kj == 0)
    def _():
        m_ref[...] = jnp.full_like(m_ref, NEG)
        l_ref[...] = jnp.zeros_like(l_ref)
        acc_ref[...] = jnp.zeros_like(acc_ref)
        ext = jnp.broadcast_to(qext_ref[...], (bq, LANES))
        for mp in range(2):
            qa_ref[mp, :, :HEAD_DIM] = q_ref[:, mp * HEAD_DIM:(mp + 1) * HEAD_DIM]
            qa_ref[mp, :, HEAD_DIM:] = ext

    def step(diag):
        c = 0.0 if diag else -c_ref[h] * ((qi - kj) * bq).astype(F32)
        v = v_ref[...]
        kext = kext_ref[...]
        for mp in range(2):
            ka = jnp.concatenate([k_ref[:, mp * HEAD_DIM:(mp + 1) * HEAD_DIM], kext], axis=1)
            s = lax.dot_general(qa_ref[mp], ka, NT_DIMS, preferred_element_type=F32)
            if diag:
                s = s + causal_ref[...]
            rows = pl.ds(mp * bq, bq)
            pr, alpha = _flash_update(s, c, m_ref, rows)
            l_ref[rows, :] = alpha * l_ref[rows, :] + jnp.sum(pr, axis=1, keepdims=True)
            pv = jnp.dot(pr.astype(BF16), v, preferred_element_type=F32)
            acc_ref[rows, :] = _lanes(alpha, DIFF_V_DIM) * acc_ref[rows, :] + pv

    @pl.when(kj < qi)
    def _():
        step(False)

    @pl.when(kj == qi)
    def _():
        step(True)
        lam_rows = lam_ref[...]
        d1 = jnp.sum(lam_rows[0:1] * lam_rows[1:2], axis=1, keepdims=True)
        d2 = jnp.sum(lam_rows[2:3] * lam_rows[3:4], axis=1, keepdims=True)
        lam = jnp.exp(d1) - jnp.exp(d2) + lam0
        o1 = acc_ref[0:bq, :] / _lanes(l_ref[0:bq, :], DIFF_V_DIM)
        o2 = acc_ref[bq:2 * bq, :] / _lanes(l_ref[bq:2 * bq, :], DIFF_V_DIM)
        o = o1 - lam * o2
        ms = jnp.mean(o * o, axis=-1, keepdims=True)
        y = o * lax.rsqrt(ms + EPS) * g_ref[...]
        o_ref[...] = (y * (1.0 - lam0)).astype(o_ref.dtype)


def _diff_attention(proj, lam_rows, gain, layer, bq):
    s = proj.shape[0]
    qi, kj = _tri_pairs(s // bq)
    cvals, qext = _alibi_consts(DIFF_HEADS)
    lam0 = 0.8 - 0.6 * math.exp(-0.3 * layer)
    w = DIFF_V_DIM
    const2 = lambda h, p, qi, kj, c: (0, 0)
    grid_spec = pltpu.PrefetchScalarGridSpec(
        num_scalar_prefetch=3,
        grid=(DIFF_HEADS, qi.shape[0]),
        in_specs=[
            pl.BlockSpec((bq, w), lambda h, p, qi, kj, c: (qi[p], OFF_DQ // w + h)),
            pl.BlockSpec((bq, w), lambda h, p, qi, kj, c: (kj[p], OFF_DK // w + h)),
            pl.BlockSpec((bq, w), lambda h, p, qi, kj, c: (kj[p], OFF_DV // w + h)),
            pl.BlockSpec((None, 1, LANES), lambda h, p, qi, kj, c: (h, 0, 0)),
            pl.BlockSpec((bq, LANES), const2),
            pl.BlockSpec((bq, bq), const2),
            pl.BlockSpec((8, HEAD_DIM), const2),
            pl.BlockSpec((1, w), const2),
        ],
        out_specs=pl.BlockSpec((bq, w), lambda h, p, qi, kj, c: (qi[p], h)),
        scratch_shapes=[pltpu.VMEM((2, bq, 2 * HEAD_DIM), BF16),
                        pltpu.VMEM((2 * bq, LANES), F32),
                        pltpu.VMEM((2 * bq, LANES), F32),
                        pltpu.VMEM((2 * bq, w), F32)],
    )
    return pl.pallas_call(
        functools.partial(_diff_kernel, bq=bq, lam0=lam0),
        grid_spec=grid_spec,
        out_shape=jax.ShapeDtypeStruct((s, DIFF_WIDTH), BF16),
        compiler_params=_params(2),
        name="diff_attention",
    )(qi, kj, cvals, proj, proj, proj, qext, _kpos_ext(bq), _causal_tile(bq), lam_rows, gain)


def _compress_kernel(a_ref, pos_ref, w1_ref, w2_ref, o_ref):
    half = CMP_STRIDE * HEAD_DIM
    a = a_ref[...].astype(F32)
    pos = pos_ref[...]
    top = (a + pos[:, :half]).astype(BF16)
    bot = (a + pos[:, half:]).astype(BF16)
    t = jnp.dot(top, w1_ref[:half, :], preferred_element_type=F32)
    b = jnp.dot(bot, w1_ref[half:, :], preferred_element_type=F32)
    rows = a.shape[0]
    hid = t + pltpu.roll(b, rows - 1, 0)
    act = jax.nn.gelu(hid)
    o_ref[...] = jnp.dot(act.astype(BF16), w2_ref[...], preferred_element_type=F32).astype(o_ref.dtype)


def _compress(a, pos, w1, w2):
    _, hkv, rows, width = a.shape
    return pl.pallas_call(
        _compress_kernel,
        grid=(2, hkv),
        in_specs=[pl.BlockSpec((None, None, rows, width), lambda t, h: (t, h, 0, 0)),
                  pl.BlockSpec((None, 1, 2 * width), lambda t, h: (t, 0, 0)),
                  pl.BlockSpec((None, 2 * width, CMP_HIDDEN), lambda t, h: (t, 0, 0)),
                  pl.BlockSpec((None, CMP_HIDDEN, HEAD_DIM), lambda t, h: (t, 0, 0))],
        out_specs=pl.BlockSpec((None, None, rows, HEAD_DIM), lambda t, h: (t, h, 0, 0)),
        out_shape=jax.ShapeDtypeStruct((2, hkv, rows, HEAD_DIM), BF16),
        compiler_params=_params(2),
        name="nsa_compress",
    )(a, pos, w1, w2)


def _split3(x):
    hi = x.astype(BF16)
    r1 = x - hi.astype(F32)
    mid = r1.astype(BF16)
    lo = (r1 - mid.astype(F32)).astype(BF16)
    return hi, mid, lo


def _stack_heads(q):
    return jnp.concatenate([q[:, g * HEAD_DIM:(g + 1) * HEAD_DIM] for g in range(NSA_GROUP)], axis=0)


def _cmp_select_kernel(c_ref, q_ref, kc_ref, vc_ref, ov_ref, oc_ref, sel_ref, *, bq, n_cmp, topn):
    hkv = pl.program_id(0)
    i = pl.program_id(1)
    ncp = kc_ref.shape[0]
    q = _stack_heads(q_ref[...])
    s = lax.dot_general(q, kc_ref[...], NT_DIMS, preferred_element_type=F32)
    t = i * bq + lax.broadcasted_iota(jnp.int32, (bq, ncp), 0)
    cidx = lax.broadcasted_iota(jnp.int32, (bq, ncp), 1)
    dist = t - (cidx * CMP_STRIDE + CMP_BLOCK - 1)
    ok = (dist >= 0) & (cidx < n_cmp)
    distf = dist.astype(F32)
    vc = vc_ref[...]
    psum = jnp.zeros((bq, ncp), F32)
    for g in range(NSA_GROUP):
        sg = jnp.where(ok, s[g * bq:(g + 1) * bq] - c_ref[hkv * NSA_GROUP + g] * distf, NEG)
        mx = jnp.max(sg, axis=1, keepdims=True)
        e = jnp.where(ok, jnp.exp2(sg - mx), 0.0)
        den = jnp.sum(e, axis=1, keepdims=True)
        pg = e / jnp.where(den > 0.0, den, 1.0)
        oc_ref[:, g * HEAD_DIM:(g + 1) * HEAD_DIM] = jnp.dot(
            pg.astype(BF16), vc, preferred_element_type=F32)
        psum = psum + pg
    ov = ov_ref[...]
    imp = jnp.zeros((bq, LANES), F32)
    for piece in _split3(psum):
        imp = imp + jnp.dot(piece, ov, preferred_element_type=F32)
    imp_t = imp.T
    tq = i * bq + lax.broadcasted_iota(jnp.int32, (LANES, bq), 1)
    blk = lax.broadcasted_iota(jnp.int32, (LANES, bq), 0)
    cur = lax.shift_right_arithmetic(tq, SEL_BLOCK.bit_length() - 1)
    forced = (blk == 0) | (blk == cur) | (blk == cur - 1)
    score = jnp.where(forced, FORCE_SCORE, jnp.where(blk <= cur, imp_t, -1.0))
    removed = -3.0e38
    blkf = blk.astype(F32)

    def pick(_, carry):
        sc, sel = carry
        mx = jnp.max(sc, axis=0, keepdims=True)
        first = jnp.min(jnp.where(sc == mx, blkf, float(LANES)), axis=0, keepdims=True)
        hit = blkf == first
        return jnp.where(hit, removed, sc), jnp.where(hit, 1.0, sel)

    _, sel_t = lax.fori_loop(0, topn, pick, (score, jnp.zeros((LANES, bq), F32)))
    sel_ref[...] = sel_t.T.astype(sel_ref.dtype)


def _cmp_select(proj, kvc, overlap, cvals, bq):
    s = proj.shape[0]
    n_cmp = (s - CMP_BLOCK) // CMP_STRIDE + 1
    n_sel = s // SEL_BLOCK
    topn = min(SEL_TOPN, n_sel)
    ncp = kvc.shape[2]
    gw = NSA_GROUP * HEAD_DIM
    grid_spec = pltpu.PrefetchScalarGridSpec(
        num_scalar_prefetch=1,
        grid=(NSA_KV_HEADS, s // bq),
        in_specs=[
            pl.BlockSpec((bq, gw), lambda h, i, c: (i, OFF_NQ // gw + h)),
            pl.BlockSpec((None, None, ncp, HEAD_DIM), lambda h, i, c: (0, h, 0, 0)),
            pl.BlockSpec((None, None, ncp, HEAD_DIM), lambda h, i, c: (1, h, 0, 0)),
            pl.BlockSpec((ncp, LANES), lambda h, i, c: (0, 0)),
        ],
        out_specs=[pl.BlockSpec((bq, gw), lambda h, i, c: (i, h)),
                   pl.BlockSpec((None, bq, LANES), lambda h, i, c: (h, i, 0))],
    )
    return pl.pallas_call(
        functools.partial(_cmp_select_kernel, bq=bq, n_cmp=n_cmp, topn=topn),
        grid_spec=grid_spec,
        out_shape=[jax.ShapeDtypeStruct((s, NSA_WIDTH), F32),
                   jax.ShapeDtypeStruct((NSA_KV_HEADS, s, LANES), BF16)],
        compiler_params=_params(2),
        name="nsa_cmp_select",
    )(cvals, proj, kvc, kvc, overlap)


def _sel_kernel(qi_ref, kj_ref, c_ref, q_ref, k_ref, v_ref, sel_ref, exp_ref, qext_ref, kext_ref,
                one_ref, causal_ref, o_ref, qa_ref, m_ref, acc_ref, *, bq):
    hkv = pl.program_id(0)
    p = pl.program_id(1)
    qi = qi_ref[p]
    kj = kj_ref[p]

    @pl.when(kj == 0)
    def _():
        m_ref[...] = jnp.full_like(m_ref, NEG)
        acc_ref[...] = jnp.zeros_like(acc_ref)
        for g in range(NSA_GROUP):
            qa_ref[g * bq:(g + 1) * bq, :HEAD_DIM] = q_ref[:, g * HEAD_DIM:(g + 1) * HEAD_DIM]
            qa_ref[g * bq:(g + 1) * bq, HEAD_DIM:] = jnp.broadcast_to(qext_ref[g], (bq, LANES))

    def step(diag):
        one = one_ref[...]
        maskb = jnp.dot(jnp.concatenate([sel_ref[...], one], axis=1), exp_ref[...],
                        preferred_element_type=F32)
        if diag:
            maskb = maskb + causal_ref[...]
        ka = jnp.concatenate([k_ref[...], kext_ref[...]], axis=1)
        va = jnp.concatenate([v_ref[...], one], axis=1)
        s_all = lax.dot_general(qa_ref[...], ka, NT_DIMS, preferred_element_type=F32)
        dtile = ((qi - kj) * bq).astype(F32)
        for g in range(NSA_GROUP):
            c = 0.0 if diag else -c_ref[hkv * NSA_GROUP + g] * dtile
            rows = pl.ds(g * bq, bq)
            pr, alpha = _flash_update(s_all[g * bq:(g + 1) * bq] + maskb, c, m_ref, rows)
            pv = jnp.dot(pr.astype(BF16), va, preferred_element_type=F32)
            acc_ref[rows, :] = _lanes(alpha, 2 * HEAD_DIM) * acc_ref[rows, :] + pv

    @pl.when(kj < qi)
    def _():
        step(False)

    @pl.when(kj == qi)
    def _():
        step(True)
        for g in range(NSA_GROUP):
            rows = pl.ds(g * bq, bq)
            o_ref[:, g * HEAD_DIM:(g + 1) * HEAD_DIM] = (
                acc_ref[rows, :HEAD_DIM] / acc_ref[rows, HEAD_DIM:HEAD_DIM + 1])


def _expand_aug(s, bk):
    key_blk = np.arange(s) // SEL_BLOCK
    e = np.zeros((2 * LANES, s), np.float32)
    e[key_blk, np.arange(s)] = MASK_BIG
    e[LANES, :] = -MASK_BIG
    return jnp.asarray(e.reshape(2 * LANES, s // bk, bk).transpose(1, 0, 2), BF16)


def _sel_attention(proj, sel, cvals, qext, bq):
    s = proj.shape[0]
    qi, kj = _tri_pairs(s // bq)
    gw = NSA_GROUP * HEAD_DIM
    d = HEAD_DIM
    const2 = lambda h, p, qi, kj, c: (0, 0)
    grid_spec = pltpu.PrefetchScalarGridSpec(
        num_scalar_prefetch=3,
        grid=(NSA_KV_HEADS, qi.shape[0]),
        in_specs=[
            pl.BlockSpec((bq, gw), lambda h, p, qi, kj, c: (qi[p], OFF_NQ // gw + h)),
            pl.BlockSpec((bq, d), lambda h, p, qi, kj, c: (kj[p], OFF_KS // d + h)),
            pl.BlockSpec((bq, d), lambda h, p, qi, kj, c: (kj[p], OFF_VS // d + h)),
            pl.BlockSpec((None, bq, LANES), lambda h, p, qi, kj, c: (h, qi[p], 0)),
            pl.BlockSpec((None, 2 * LANES, bq), lambda h, p, qi, kj, c: (kj[p], 0, 0)),
            pl.BlockSpec((None, NSA_GROUP, 1, LANES), lambda h, p, qi, kj, c: (h, 0, 0, 0)),
            pl.BlockSpec((bq, LANES), const2),
            pl.BlockSpec((bq, LANES), const2),
            pl.BlockSpec((bq, bq), const2),
        ],
        out_specs=pl.BlockSpec((bq, gw), lambda h, p, qi, kj, c: (qi[p], h)),
        scratch_shapes=[pltpu.VMEM((NSA_GROUP * bq, 2 * d), BF16),
                        pltpu.VMEM((NSA_GROUP * bq, LANES), F32),
                        pltpu.VMEM((NSA_GROUP * bq, 2 * d), F32)],
    )
    return pl.pallas_call(
        functools.partial(_sel_kernel, bq=bq),
        grid_spec=grid_spec,
        out_shape=jax.ShapeDtypeStruct((s, NSA_WIDTH), F32),
        compiler_params=_params(2),
        name="nsa_selected",
    )(qi, kj, cvals, proj, proj, proj, sel, _expand_aug(s, bq),
      qext.reshape(NSA_KV_HEADS, NSA_GROUP, 1, LANES), _kpos_ext(bq), _one_col(bq), _causal_tile(bq))


def _win_kernel(c_ref, q_ref, kp_ref, kc_ref, vp_ref, vc_ref, one_ref, oc_ref, os_ref, gate_ref, o_ref,
                bias_ref):
    hkv = pl.program_id(0)
    i = pl.program_id(1)
    bq = WINDOW

    @pl.when(i == 0)
    def _():
        rel = (lax.broadcasted_iota(jnp.int32, (bq, bq), 0)
               - lax.broadcasted_iota(jnp.int32, (bq, bq), 1))
        relf = rel.astype(F32)
        for g in range(NSA_GROUP):
            c = c_ref[hkv * NSA_GROUP + g]
            bias_ref[g, 0] = jnp.where(rel < 0, -c * (relf + WINDOW), NEG)
            bias_ref[g, 1] = jnp.where(rel >= 0, -c * relf, NEG)

    q = _stack_heads(q_ref[...])
    sp_all = lax.dot_general(q, kp_ref[...], NT_DIMS, preferred_element_type=F32)
    sc_all = lax.dot_general(q, kc_ref[...], NT_DIMS, preferred_element_type=F32)
    first = jnp.where(i == 0, NEG, 0.0)
    one = one_ref[...]
    vp = jnp.concatenate([vp_ref[...], one], axis=1)
    vc = jnp.concatenate([vc_ref[...], one], axis=1)
    gates = gate_ref[...]
    for g in range(NSA_GROUP):
        rows = slice(g * bq, (g + 1) * bq)
        cols = slice(g * HEAD_DIM, (g + 1) * HEAD_DIM)
        sp = sp_all[rows] + bias_ref[g, 0] + first
        sc = sc_all[rows] + bias_ref[g, 1]
        mx = jnp.maximum(jnp.max(sp, axis=1, keepdims=True), jnp.max(sc, axis=1, keepdims=True))
        pp = jnp.exp2(sp - mx).astype(BF16)
        pc = jnp.exp2(sc - mx).astype(BF16)
        pv = (jnp.dot(pp, vp, preferred_element_type=F32)
              + jnp.dot(pc, vc, preferred_element_type=F32))
        ow = pv[:, :HEAD_DIM] / pv[:, HEAD_DIM:HEAD_DIM + 1]
        gc = gates[:, 3 * g + 0:3 * g + 1]
        gs = gates[:, 3 * g + 1:3 * g + 2]
        gw = gates[:, 3 * g + 2:3 * g + 3]
        o_ref[:, cols] = (gc * oc_ref[:, cols] + gs * os_ref[:, cols] + gw * ow).astype(o_ref.dtype)


def _win_combine(proj, o_cmp, o_sel, gates, cvals):
    s = proj.shape[0]
    bq = WINDOW
    gw = NSA_GROUP * HEAD_DIM
    d = HEAD_DIM
    prev = lambda i: jnp.maximum(i - 1, 0)
    grid_spec = pltpu.PrefetchScalarGridSpec(
        num_scalar_prefetch=1,
        grid=(NSA_KV_HEADS, s // bq),
        in_specs=[
            pl.BlockSpec((bq, gw), lambda h, i, c: (i, OFF_NQ // gw + h)),
            pl.BlockSpec((bq, d), lambda h, i, c: (prev(i), OFF_KW // d + h)),
            pl.BlockSpec((bq, d), lambda h, i, c: (i, OFF_KW // d + h)),
            pl.BlockSpec((bq, d), lambda h, i, c: (prev(i), OFF_VW // d + h)),
            pl.BlockSpec((bq, d), lambda h, i, c: (i, OFF_VW // d + h)),
            pl.BlockSpec((bq, LANES), lambda h, i, c: (0, 0)),
            pl.BlockSpec((bq, gw), lambda h, i, c: (i, h)),
            pl.BlockSpec((bq, gw), lambda h, i, c: (i, h)),
            pl.BlockSpec((bq, LANES), lambda h, i, c: (i, h)),
        ],
        out_specs=pl.BlockSpec((bq, gw), lambda h, i, c: (i, h)),
        scratch_shapes=[pltpu.VMEM((NSA_GROUP, 2, bq, bq), F32)],
    )
    return pl.pallas_call(
        _win_kernel,
        grid_spec=grid_spec,
        out_shape=jax.ShapeDtypeStruct((s, NSA_WIDTH), BF16),
        compiler_params=_params(2),
        name="nsa_window_combine",
    )(cvals, proj, proj, proj, proj, proj, _one_col(bq), o_cmp, o_sel, gates)


def _ffn(h, norm_g, w_gate, w_up, w_down):
    u = _rmsnorm(h, norm_g, BF16)
    act = _ffn_up(u, w_gate.astype(BF16), w_up.astype(BF16))
    return _mm_res(act, w_down.astype(BF16), h, 0.5)


def _overlap_matrix(s, rows):
    n_cmp = (s - CMP_BLOCK) // CMP_STRIDE + 1
    n_sel = s // SEL_BLOCK
    cs = np.arange(rows)[:, None] * CMP_STRIDE
    ss = np.arange(LANES)[None, :] * SEL_BLOCK
    ov = (cs <= ss + SEL_BLOCK - 1) & (cs + CMP_BLOCK - 1 >= ss)
    ov &= (np.arange(rows)[:, None] < n_cmp) & (np.arange(LANES)[None, :] < n_sel)
    return jnp.asarray(ov, BF16)


def _query_colscale():
    cs = np.ones((1, OFF_G), np.float32)
    cs[:, OFF_DQ:OFF_DK] = QSCALE
    cs[:, OFF_NQ:OFF_KC] = QSCALE
    return jnp.asarray(cs)


def _layer(h, layer, ffn1_norm, ffn1_w_gate, ffn1_w_up, ffn1_w_down, mix_norm, w_in, gate_bias,
           lambda_q1, lambda_k1, lambda_q2, lambda_k2, diff_norm,
           cmp_pos_k, cmp_w1_k, cmp_w2_k, cmp_pos_v, cmp_w1_v, cmp_w2_v,
           w_out, ffn2_norm, ffn2_w_gate, ffn2_w_up, ffn2_w_down):
    s = h.shape[0]
    assert s % WINDOW == 0 and s // SEL_BLOCK <= LANES and OFF_G % 512 == 0
    h = _ffn(h, ffn1_norm, ffn1_w_gate, ffn1_w_up, ffn1_w_down)

    u = _rmsnorm(h, mix_norm, BF16)
    proj = _mm_scaled(u, w_in[:, :OFF_G].astype(BF16), _query_colscale(), BF16)
    wg = w_in[:, OFF_G:].reshape(-1, NSA_KV_HEADS, GATES_PER_KV)
    wg = jnp.pad(wg, ((0, 0), (0, 0), (0, LANES - GATES_PER_KV))).reshape(-1, NSA_KV_HEADS * LANES)
    gb = jnp.pad(gate_bias.astype(F32).reshape(NSA_KV_HEADS, GATES_PER_KV),
                 ((0, 0), (0, LANES - GATES_PER_KV))).reshape(1, NSA_KV_HEADS * LANES)
    gates = _gate_proj(u, wg.astype(BF16), gb)

    lam_rows = jnp.pad(jnp.stack([lambda_q1, lambda_k1, lambda_q2, lambda_k2]).astype(F32),
                       ((0, 4), (0, 0)))
    o_diff = _diff_attention(proj, lam_rows, diff_norm.reshape(1, -1).astype(F32), layer,
                             _pick(s, 512))

    nsa_c, nsa_qext = _alibi_consts(NSA_HEADS)
    rows = s // CMP_STRIDE
    kv_cmp = proj[:, OFF_KC:OFF_KS].reshape(rows, CMP_STRIDE, 2, NSA_KV_HEADS, HEAD_DIM)
    kv_cmp = kv_cmp.transpose(2, 3, 0, 1, 4).reshape(2, NSA_KV_HEADS, rows, CMP_STRIDE * HEAD_DIM)
    pos = jnp.stack([cmp_pos_k, cmp_pos_v]).astype(F32).reshape(2, 1, CMP_BLOCK * HEAD_DIM)
    w1 = jnp.stack([cmp_w1_k, cmp_w1_v]).astype(BF16)
    w2 = jnp.stack([cmp_w2_k, cmp_w2_v]).astype(BF16)
    kvc = _compress(kv_cmp, pos, w1, w2)
    o_cmp, sel = _cmp_select(proj, kvc, _overlap_matrix(s, rows), nsa_c, _pick(s, 256))
    o_sel = _sel_attention(proj, sel, nsa_c, nsa_qext, _pick(s, 512))
    o_nsa = _win_combine(proj, o_cmp, o_sel, gates, nsa_c)

    wo = w_out.astype(BF16)
    h = _mm2_res(o_diff, o_nsa, wo[:DIFF_WIDTH], wo[DIFF_WIDTH:], h)
    return _ffn(h, ffn2_norm, ffn2_w_gate, ffn2_w_up, ffn2_w_down)


def kernel(x, ffn1_norm, ffn1_w_gate, ffn1_w_up, ffn1_w_down, mix_norm, w_in, gate_bias, lambda_q1, lambda_k1, lambda_q2, lambda_k2, diff_norm, cmp_pos_k, cmp_w1_k, cmp_w2_k, cmp_pos_v, cmp_w1_v, cmp_w2_v, w_out, ffn2_norm, ffn2_w_gate, ffn2_w_up, ffn2_w_down, final_norm):
    b, s, d = x.shape
    per_layer = (ffn1_norm, ffn1_w_gate, ffn1_w_up, ffn1_w_down, mix_norm, w_in, gate_bias,
                 lambda_q1, lambda_k1, lambda_q2, lambda_k2, diff_norm,
                 cmp_pos_k, cmp_w1_k, cmp_w2_k, cmp_pos_v, cmp_w1_v, cmp_w2_v,
                 w_out, ffn2_norm, ffn2_w_gate, ffn2_w_up, ffn2_w_down)
    outs = []
    for bi in range(b):
        h = x.reshape(s, d) if b == 1 else x[bi]
        for layer in range(DEPTH):
            h = _layer(h, layer, *[p[layer] for p in per_layer])
        outs.append(_rmsnorm(h, final_norm, x.dtype))
    return outs[0].reshape(1, s, d) if b == 1 else jnp.stack(outs)
```

```python
import functools
import math

import ml_dtypes
import numpy as np
import jax
import jax.numpy as jnp
from jax import lax
from jax.experimental import pallas as pl
from jax.experimental.pallas import tpu as pltpu

D_MODEL = 4096
DEPTH = 1
HEAD_DIM = 128
DIFF_V_DIM = 2 * HEAD_DIM
DIFF_HEADS = (D_MODEL // 2) // DIFF_V_DIM
DIFF_WIDTH = DIFF_HEADS * DIFF_V_DIM
NSA_HEADS = (D_MODEL - DIFF_WIDTH) // HEAD_DIM
NSA_KV_HEADS = 4
NSA_GROUP = NSA_HEADS // NSA_KV_HEADS
NSA_WIDTH = NSA_HEADS * HEAD_DIM
CMP_BLOCK = 32
CMP_STRIDE = 16
CMP_HIDDEN = 256
SEL_BLOCK = 64
SEL_TOPN = 16
WINDOW = 512
EPS = 1e-6
NEG = -1e30
FORCE_SCORE = 1e4

LANES = 128
VMEM_LIMIT = 56 * 1024 * 1024
LOG2E = 1.4426950408889634
QSCALE = HEAD_DIM ** -0.5 * LOG2E
MASK_BIG = 2.0 ** 100
KPOS_SPLIT = 32

OFF_DQ = 0
OFF_DK = OFF_DQ + DIFF_HEADS * 2 * HEAD_DIM
OFF_DV = OFF_DK + DIFF_HEADS * 2 * HEAD_DIM
OFF_NQ = OFF_DV + DIFF_HEADS * DIFF_V_DIM
OFF_KC = OFF_NQ + NSA_HEADS * HEAD_DIM
OFF_VC = OFF_KC + NSA_KV_HEADS * HEAD_DIM
OFF_KS = OFF_VC + NSA_KV_HEADS * HEAD_DIM
OFF_VS = OFF_KS + NSA_KV_HEADS * HEAD_DIM
OFF_KW = OFF_VS + NSA_KV_HEADS * HEAD_DIM
OFF_VW = OFF_KW + NSA_KV_HEADS * HEAD_DIM
OFF_G = OFF_VW + NSA_KV_HEADS * HEAD_DIM
GATES_PER_KV = 3 * NSA_GROUP

F32 = jnp.float32
BF16 = jnp.bfloat16
NT_DIMS = (((1,), (1,)), ((), ()))


def _params(n_axes):
    return pltpu.CompilerParams(dimension_semantics=("arbitrary",) * n_axes,
                                vmem_limit_bytes=VMEM_LIMIT)


def _pick(n, pref):
    b = min(pref, n)
    while n % b:
        b //= 2
    return b


def _rmsnorm_kernel(x_ref, g_ref, o_ref):
    x = x_ref[...]
    ms = jnp.mean(x * x, axis=-1, keepdims=True)
    o_ref[...] = (x * lax.rsqrt(ms + EPS) * g_ref[...]).astype(o_ref.dtype)


def _rmsnorm(x, g, out_dtype):
    s, d = x.shape
    bm = _pick(s, 256)
    return pl.pallas_call(
        _rmsnorm_kernel,
        grid=(s // bm,),
        in_specs=[pl.BlockSpec((bm, d), lambda i: (i, 0)),
                  pl.BlockSpec((1, d), lambda i: (0, 0))],
        out_specs=pl.BlockSpec((bm, d), lambda i: (i, 0)),
        out_shape=jax.ShapeDtypeStruct((s, d), out_dtype),
        compiler_params=_params(1),
        name="rmsnorm",
    )(x, g.reshape(1, d).astype(F32))


def _ffn_up_kernel(u_ref, wg_ref, wu_ref, o_ref, wgb_ref, wub_ref):
    @pl.when(pl.program_id(1) == 0)
    def _():
        wgb_ref[...] = wg_ref[...].astype(BF16)
        wub_ref[...] = wu_ref[...].astype(BF16)

    u = u_ref[...]
    g = jnp.dot(u, wgb_ref[...], preferred_element_type=F32)
    up = jnp.dot(u, wub_ref[...], preferred_element_type=F32)
    o_ref[...] = (g * jax.nn.sigmoid(g) * up).astype(o_ref.dtype)


def _ffn_up(u, wg, wu):
    s, d = u.shape
    f = wg.shape[1]
    bm, bn = _pick(s, 512), min(512, f)
    return pl.pallas_call(
        _ffn_up_kernel,
        grid=(pl.cdiv(f, bn), s // bm),
        in_specs=[pl.BlockSpec((bm, d), lambda j, i: (i, 0)),
                  pl.BlockSpec((d, bn), lambda j, i: (0, j)),
                  pl.BlockSpec((d, bn), lambda j, i: (0, j))],
        out_specs=pl.BlockSpec((bm, bn), lambda j, i: (i, j)),
        out_shape=jax.ShapeDtypeStruct((s, f), BF16),
        scratch_shapes=[pltpu.VMEM((d, bn), BF16), pltpu.VMEM((d, bn), BF16)],
        compiler_params=_params(2),
        name="ffn_up",
    )(u, wg, wu)


def _mm_res_kernel(a_ref, b_ref, r_ref, o_ref, *, alpha):
    o_ref[...] = r_ref[...] + alpha * jnp.dot(a_ref[...], b_ref[...], preferred_element_type=F32)


def _mm_res(a, b, res, alpha):
    s, kdim = a.shape
    n = b.shape[1]
    bm, bn = _pick(s, 512), _pick(n, 512)
    return pl.pallas_call(
        functools.partial(_mm_res_kernel, alpha=alpha),
        grid=(s // bm, n // bn),
        in_specs=[pl.BlockSpec((bm, kdim), lambda i, j: (i, 0)),
                  pl.BlockSpec((kdim, bn), lambda i, j: (0, j)),
                  pl.BlockSpec((bm, bn), lambda i, j: (i, j))],
        out_specs=pl.BlockSpec((bm, bn), lambda i, j: (i, j)),
        out_shape=jax.ShapeDtypeStruct((s, n), F32),
        compiler_params=_params(2),
        name="mm_res",
    )(a, b, res)


def _proj_kernel(a_ref, b_ref, cs_ref, o_ref, bb_ref):
    @pl.when(pl.program_id(1) == 0)
    def _():
        bb_ref[...] = b_ref[...].astype(BF16)

    acc = jnp.dot(a_ref[...], bb_ref[...], preferred_element_type=F32) * cs_ref[...]
    for t in range(o_ref.shape[0]):
        o_ref[t] = acc[:, t * LANES:(t + 1) * LANES].astype(o_ref.dtype)


def _proj_slabs(a, b, colscale):
    s, kdim = a.shape
    n = colscale.shape[1]
    bm, bn = _pick(s, 1024), _pick(n, 512)
    return pl.pallas_call(
        _proj_kernel,
        grid=(n // bn, s // bm),
        in_specs=[pl.BlockSpec((bm, kdim), lambda j, i: (i, 0)),
                  pl.BlockSpec((kdim, bn), lambda j, i: (0, j)),
                  pl.BlockSpec((1, bn), lambda j, i: (0, j))],
        out_specs=pl.BlockSpec((bn // LANES, bm, LANES), lambda j, i: (j, i, 0)),
        out_shape=jax.ShapeDtypeStruct((n // LANES, s, LANES), BF16),
        scratch_shapes=[pltpu.VMEM((kdim, bn), BF16)],
        compiler_params=_params(2),
        name="proj_slabs",
    )(a, b, colscale)


def _gate_kernel(a_ref, b_ref, bias_ref, o_ref):
    z = jnp.dot(a_ref[...], b_ref[...], preferred_element_type=F32) + bias_ref[...]
    o_ref[...] = jax.nn.sigmoid(z)


def _gate_proj(u, wg, bias):
    s, kdim = u.shape
    n = wg.shape[1]
    bm = _pick(s, 1024)
    return pl.pallas_call(
        _gate_kernel,
        grid=(s // bm,),
        in_specs=[pl.BlockSpec((bm, kdim), lambda i: (i, 0)),
                  pl.BlockSpec((kdim, n), lambda i: (0, 0)),
                  pl.BlockSpec((1, n), lambda i: (0, 0))],
        out_specs=pl.BlockSpec((bm, n), lambda i: (i, 0)),
        out_shape=jax.ShapeDtypeStruct((s, n), F32),
        compiler_params=_params(1),
        name="gate_proj",
    )(u, wg, bias)


def _mm2_res_kernel(a1_ref, a2_ref, b1_ref, b2_ref, r_ref, o_ref):
    acc = jnp.dot(a1_ref[...], b1_ref[...], preferred_element_type=F32)
    acc += jnp.dot(a2_ref[...], b2_ref[...], preferred_element_type=F32)
    o_ref[...] = r_ref[...] + acc


def _mm2_res(a1, a2, b1, b2, res):
    s, k1 = a1.shape
    k2 = a2.shape[1]
    n = b1.shape[1]
    bm, bn = _pick(s, 1024), _pick(n, 512)
    return pl.pallas_call(
        _mm2_res_kernel,
        grid=(s // bm, n // bn),
        in_specs=[pl.BlockSpec((bm, k1), lambda i, j: (i, 0)),
                  pl.BlockSpec((bm, k2), lambda i, j: (i, 0)),
                  pl.BlockSpec((k1, bn), lambda i, j: (0, j)),
                  pl.BlockSpec((k2, bn), lambda i, j: (0, j)),
                  pl.BlockSpec((bm, bn), lambda i, j: (i, j))],
        out_specs=pl.BlockSpec((bm, bn), lambda i, j: (i, j)),
        out_shape=jax.ShapeDtypeStruct((s, n), F32),
        compiler_params=_params(2),
        name="mm2_res",
    )(a1, a2, b1, b2, res)


def _tri_pairs(n):
    qi = np.repeat(np.arange(n), np.arange(1, n + 1))
    kj = np.concatenate([np.arange(i + 1) for i in range(n)])
    return jnp.asarray(qi, jnp.int32), jnp.asarray(kj, jnp.int32)


def _bf16_round(x):
    return np.asarray(x, np.float32).astype(ml_dtypes.bfloat16).astype(np.float64)


def _alibi_consts(n_heads):
    c = np.float32(2.0 ** (-8.0 * np.arange(1, n_heads + 1) / n_heads) * LOG2E).astype(np.float64)
    hi = _bf16_round(c)
    mid = _bf16_round(c - hi)
    lo = _bf16_round(c - hi - mid)
    rows = np.zeros((n_heads, 1, LANES), np.float32)
    for col, piece in enumerate((hi, hi, mid, mid, lo, lo)):
        rows[:, 0, col] = piece
    return jnp.asarray(c, F32), jnp.asarray(rows, BF16)


def _kpos_ext(bk):
    pos = np.arange(bk)
    a = (pos // KPOS_SPLIT) * KPOS_SPLIT
    b = pos % KPOS_SPLIT
    ext = np.zeros((bk, LANES), np.float32)
    for col in range(0, 6, 2):
        ext[:, col] = a
        ext[:, col + 1] = b
    return jnp.asarray(ext, BF16)


def _one_col(rows):
    x = np.zeros((rows, LANES), np.float32)
    x[:, 0] = 1.0
    return jnp.asarray(x, BF16)


def _causal_tile(b):
    r = np.arange(b)
    return jnp.asarray(np.where(r[:, None] >= r[None, :], 0.0, NEG), F32)


def _lanes(x, width):
    if width == LANES:
        return x
    return jnp.tile(x, (1, width // LANES))


def _flash_update(s, c, m_ref, rows):
    m_prev = m_ref[rows, :]
    m_cur = jnp.max(s, axis=1, keepdims=True) + c
    m_next = jnp.maximum(m_prev, m_cur)
    m_ref[rows, :] = m_next
    p = jnp.exp2(s - _lanes(m_next - c, s.shape[1]))
    return p, jnp.exp2(m_prev - m_next)


def _diff_kernel(qi_ref, kj_ref, c_ref, q_ref, k_ref, v_ref, qext_ref, kext_ref, causal_ref,
                 lam_ref, g_ref, o_ref, qa_ref, m_ref, l_ref, acc_ref, *, bq, lam0):
    h = pl.program_id(0)
    p = pl.program_id(1)
    qi = qi_ref[p]
    kj = kj_ref[p]

    @pl.when(kj == 0)
    def _():
        m_ref[...] = jnp.full_like(m_ref, NEG)
        l_ref[...] = jnp.zeros_like(l_ref)
        acc_ref[...] = jnp.zeros_like(acc_ref)
        ext = jnp.broadcast_to(qext_ref[...], (bq, LANES))
        for mp in range(2):
            qa_ref[mp, :, :HEAD_DIM] = q_ref[mp]
            qa_ref[mp, :, HEAD_DIM:] = ext

    def step(diag):
        c = 0.0 if diag else -c_ref[h] * ((qi - kj) * bq).astype(F32)
        v = jnp.concatenate([v_ref[0], v_ref[1]], axis=1)
        kext = kext_ref[...]
        for mp in range(2):
            ka = jnp.concatenate([k_ref[mp], kext], axis=1)
            s = lax.dot_general(qa_ref[mp], ka, NT_DIMS, preferred_element_type=F32)
            if diag:
                s = s + causal_ref[...]
            rows = pl.ds(mp * bq, bq)
            pr, alpha = _flash_update(s, c, m_ref, rows)
            l_ref[rows, :] = alpha * l_ref[rows, :] + jnp.sum(pr, axis=1, keepdims=True)
            pv = jnp.dot(pr.astype(BF16), v, preferred_element_type=F32)
            acc_ref[rows, :] = _lanes(alpha, DIFF_V_DIM) * acc_ref[rows, :] + pv

    @pl.when(kj < qi)
    def _():
        step(False)

    @pl.when(kj == qi)
    def _():
        step(True)
        lam_rows = lam_ref[...]
        d1 = jnp.sum(lam_rows[0:1] * lam_rows[1:2], axis=1, keepdims=True)
        d2 = jnp.sum(lam_rows[2:3] * lam_rows[3:4], axis=1, keepdims=True)
        lam = jnp.exp(d1) - jnp.exp(d2) + lam0
        o1 = acc_ref[0:bq, :] / _lanes(l_ref[0:bq, :], DIFF_V_DIM)
        o2 = acc_ref[bq:2 * bq, :] / _lanes(l_ref[bq:2 * bq, :], DIFF_V_DIM)
        o = o1 - lam * o2
        ms = jnp.mean(o * o, axis=-1, keepdims=True)
        y = o * lax.rsqrt(ms + EPS) * g_ref[...]
        o_ref[...] = (y * (1.0 - lam0)).astype(o_ref.dtype)


def _diff_attention(proj, lam_rows, gain, layer, bq):
    s = proj.shape[1]
    qi, kj = _tri_pairs(s // bq)
    cvals, qext = _alibi_consts(DIFF_HEADS)
    lam0 = 0.8 - 0.6 * math.exp(-0.3 * layer)
    w = DIFF_V_DIM
    const2 = lambda h, p, qi, kj, c: (0, 0)
    grid_spec = pltpu.PrefetchScalarGridSpec(
        num_scalar_prefetch=3,
        grid=(DIFF_HEADS, qi.shape[0]),
        in_specs=[
            pl.BlockSpec((2, bq, LANES), lambda h, p, qi, kj, c: (OFF_DQ // w + h, qi[p], 0)),
            pl.BlockSpec((2, bq, LANES), lambda h, p, qi, kj, c: (OFF_DK // w + h, kj[p], 0)),
            pl.BlockSpec((2, bq, LANES), lambda h, p, qi, kj, c: (OFF_DV // w + h, kj[p], 0)),
            pl.BlockSpec((None, 1, LANES), lambda h, p, qi, kj, c: (h, 0, 0)),
            pl.BlockSpec((bq, LANES), const2),
            pl.BlockSpec((bq, bq), const2),
            pl.BlockSpec((8, HEAD_DIM), const2),
            pl.BlockSpec((1, w), const2),
        ],
        out_specs=pl.BlockSpec((bq, w), lambda h, p, qi, kj, c: (qi[p], h)),
        scratch_shapes=[pltpu.VMEM((2, bq, 2 * HEAD_DIM), BF16),
                        pltpu.VMEM((2 * bq, LANES), F32),
                        pltpu.VMEM((2 * bq, LANES), F32),
                        pltpu.VMEM((2 * bq, w), F32)],
    )
    return pl.pallas_call(
        functools.partial(_diff_kernel, bq=bq, lam0=lam0),
        grid_spec=grid_spec,
        out_shape=jax.ShapeDtypeStruct((s, DIFF_WIDTH), BF16),
        compiler_params=_params(2),
        name="diff_attention",
    )(qi, kj, cvals, proj, proj, proj, qext, _kpos_ext(bq), _causal_tile(bq), lam_rows, gain)


def _compress_kernel(a_ref, pos_ref, w1_ref, w2_ref, o_ref):
    half = CMP_STRIDE * HEAD_DIM
    a = a_ref[...].astype(F32)
    pos = pos_ref[...]
    top = (a + pos[:, :half]).astype(BF16)
    bot = (a + pos[:, half:]).astype(BF16)
    t = jnp.dot(top, w1_ref[:half, :], preferred_element_type=F32)
    b = jnp.dot(bot, w1_ref[half:, :], preferred_element_type=F32)
    rows = a.shape[0]
    hid = t + pltpu.roll(b, rows - 1, 0)
    act = jax.nn.gelu(hid)
    o_ref[...] = jnp.dot(act.astype(BF16), w2_ref[...], preferred_element_type=F32).astype(o_ref.dtype)


def _compress(a, pos, w1, w2):
    _, hkv, rows, width = a.shape
    return pl.pallas_call(
        _compress_kernel,
        grid=(2, hkv),
        in_specs=[pl.BlockSpec((None, None, rows, width), lambda t, h: (t, h, 0, 0)),
                  pl.BlockSpec((None, 1, 2 * width), lambda t, h: (t, 0, 0)),
                  pl.BlockSpec((None, 2 * width, CMP_HIDDEN), lambda t, h: (t, 0, 0)),
                  pl.BlockSpec((None, CMP_HIDDEN, HEAD_DIM), lambda t, h: (t, 0, 0))],
        out_specs=pl.BlockSpec((None, None, rows, HEAD_DIM), lambda t, h: (t, h, 0, 0)),
        out_shape=jax.ShapeDtypeStruct((2, hkv, rows, HEAD_DIM), BF16),
        compiler_params=_params(2),
        name="nsa_compress",
    )(a, pos, w1, w2)


def _split3(x):
    hi = x.astype(BF16)
    r1 = x - hi.astype(F32)
    mid = r1.astype(BF16)
    lo = (r1 - mid.astype(F32)).astype(BF16)
    return hi, mid, lo


def _stack_heads(q):
    return jnp.concatenate([q[:, g * HEAD_DIM:(g + 1) * HEAD_DIM] for g in range(NSA_GROUP)], axis=0)


def _cmp_select_kernel(c_ref, q_ref, kc_ref, vc_ref, ov_ref, oc_ref, sel_ref, *, bq, n_cmp, topn):
    hkv = pl.program_id(0)
    i = pl.program_id(1)
    ncp = kc_ref.shape[0]
    q = q_ref[...].reshape(NSA_GROUP * bq, HEAD_DIM)
    s = lax.dot_general(q, kc_ref[...], NT_DIMS, preferred_element_type=F32)
    t = i * bq + lax.broadcasted_iota(jnp.int32, (bq, ncp), 0)
    cidx = lax.broadcasted_iota(jnp.int32, (bq, ncp), 1)
    dist = t - (cidx * CMP_STRIDE + CMP_BLOCK - 1)
    ok = (dist >= 0) & (cidx < n_cmp)
    distf = dist.astype(F32)
    vc = vc_ref[...]
    psum = jnp.zeros((bq, ncp), F32)
    for g in range(NSA_GROUP):
        sg = jnp.where(ok, s[g * bq:(g + 1) * bq] - c_ref[hkv * NSA_GROUP + g] * distf, NEG)
        mx = jnp.max(sg, axis=1, keepdims=True)
        e = jnp.where(ok, jnp.exp2(sg - mx), 0.0)
        den = jnp.sum(e, axis=1, keepdims=True)
        pg = e / jnp.where(den > 0.0, den, 1.0)
        oc_ref[:, g * HEAD_DIM:(g + 1) * HEAD_DIM] = jnp.dot(
            pg.astype(BF16), vc, preferred_element_type=F32)
        psum = psum + pg
    ov = ov_ref[...]
    imp = jnp.zeros((bq, LANES), F32)
    for piece in _split3(psum):
        imp = imp + jnp.dot(piece, ov, preferred_element_type=F32)
    imp_t = imp.T
    tq = i * bq + lax.broadcasted_iota(jnp.int32, (LANES, bq), 1)
    blk = lax.broadcasted_iota(jnp.int32, (LANES, bq), 0)
    cur = lax.shift_right_arithmetic(tq, SEL_BLOCK.bit_length() - 1)
    forced = (blk == 0) | (blk == cur) | (blk == cur - 1)
    score = jnp.where(forced, FORCE_SCORE, jnp.where(blk <= cur, imp_t, -1.0))
    removed = -3.0e38
    blkf = blk.astype(F32)

    def pick(_, carry):
        sc, sel = carry
        mx = jnp.max(sc, axis=0, keepdims=True)
        first = jnp.min(jnp.where(sc == mx, blkf, float(LANES)), axis=0, keepdims=True)
        hit = blkf == first
        return jnp.where(hit, removed, sc), jnp.where(hit, 1.0, sel)

    _, sel_t = lax.fori_loop(0, topn, pick, (score, jnp.zeros((LANES, bq), F32)))
    sel_ref[...] = sel_t.T.astype(sel_ref.dtype)


def _cmp_select(proj, kvc, overlap, cvals, bq):
    s = proj.shape[1]
    n_cmp = (s - CMP_BLOCK) // CMP_STRIDE + 1
    n_sel = s // SEL_BLOCK
    topn = min(SEL_TOPN, n_sel)
    ncp = kvc.shape[2]
    gw = NSA_GROUP * HEAD_DIM
    grid_spec = pltpu.PrefetchScalarGridSpec(
        num_scalar_prefetch=1,
        grid=(NSA_KV_HEADS, s // bq),
        in_specs=[
            pl.BlockSpec((NSA_GROUP, bq, LANES), lambda h, i, c: (OFF_NQ // gw + h, i, 0)),
            pl.BlockSpec((None, None, ncp, HEAD_DIM), lambda h, i, c: (0, h, 0, 0)),
            pl.BlockSpec((None, None, ncp, HEAD_DIM), lambda h, i, c: (1, h, 0, 0)),
            pl.BlockSpec((ncp, LANES), lambda h, i, c: (0, 0)),
        ],
        out_specs=[pl.BlockSpec((bq, gw), lambda h, i, c: (i, h)),
                   pl.BlockSpec((None, bq, LANES), lambda h, i, c: (h, i, 0))],
    )
    return pl.pallas_call(
        functools.partial(_cmp_select_kernel, bq=bq, n_cmp=n_cmp, topn=topn),
        grid_spec=grid_spec,
        out_shape=[jax.ShapeDtypeStruct((s, NSA_WIDTH), F32),
                   jax.ShapeDtypeStruct((NSA_KV_HEADS, s, LANES), BF16)],
        compiler_params=_params(2),
        name="nsa_cmp_select",
    )(cvals, proj, kvc, kvc, overlap)


def _sel_kernel(qi_ref, kj_ref, c_ref, q_ref, k_ref, v_ref, sel_ref, exp_ref, qext_ref, kext_ref,
                one_ref, causal_ref, o_ref, qa_ref, m_ref, acc_ref, *, bq):
    hkv = pl.program_id(0)
    p = pl.program_id(1)
    qi = qi_ref[p]
    kj = kj_ref[p]

    @pl.when(kj == 0)
    def _():
        m_ref[...] = jnp.full_like(m_ref, NEG)
        acc_ref[...] = jnp.zeros_like(acc_ref)
        for g in range(NSA_GROUP):
            qa_ref[g * bq:(g + 1) * bq, :HEAD_DIM] = q_ref[g]
            qa_ref[g * bq:(g + 1) * bq, HEAD_DIM:] = jnp.broadcast_to(qext_ref[g], (bq, LANES))

    def step(diag):
        one = one_ref[...]
        maskb = jnp.dot(jnp.concatenate([sel_ref[...], one], axis=1), exp_ref[...],
                        preferred_element_type=F32)
        if diag:
            maskb = maskb + causal_ref[...]
        ka = jnp.concatenate([k_ref[...], kext_ref[...]], axis=1)
        va = jnp.concatenate([v_ref[...], one], axis=1)
        s_all = lax.dot_general(qa_ref[...], ka, NT_DIMS, preferred_element_type=F32)
        dtile = ((qi - kj) * bq).astype(F32)
        for g in range(NSA_GROUP):
            c = 0.0 if diag else -c_ref[hkv * NSA_GROUP + g] * dtile
            rows = pl.ds(g * bq, bq)
            pr, alpha = _flash_update(s_all[g * bq:(g + 1) * bq] + maskb, c, m_ref, rows)
            pv = jnp.dot(pr.astype(BF16), va, preferred_element_type=F32)
            acc_ref[rows, :] = _lanes(alpha, 2 * HEAD_DIM) * acc_ref[rows, :] + pv

    @pl.when(kj < qi)
    def _():
        step(False)

    @pl.when(kj == qi)
    def _():
        step(True)
        for g in range(NSA_GROUP):
            rows = pl.ds(g * bq, bq)
            o_ref[:, g * HEAD_DIM:(g + 1) * HEAD_DIM] = (
                acc_ref[rows, :HEAD_DIM] / acc_ref[rows, HEAD_DIM:HEAD_DIM + 1])


def _expand_aug(s, bk):
    key_blk = np.arange(s) // SEL_BLOCK
    e = np.zeros((2 * LANES, s), np.float32)
    e[key_blk, np.arange(s)] = MASK_BIG
    e[LANES, :] = -MASK_BIG
    return jnp.asarray(e.reshape(2 * LANES, s // bk, bk).transpose(1, 0, 2), BF16)


def _sel_attention(proj, sel, cvals, qext, bq):
    s = proj.shape[1]
    qi, kj = _tri_pairs(s // bq)
    gw = NSA_GROUP * HEAD_DIM
    d = HEAD_DIM
    const2 = lambda h, p, qi, kj, c: (0, 0)
    grid_spec = pltpu.PrefetchScalarGridSpec(
        num_scalar_prefetch=3,
        grid=(NSA_KV_HEADS, qi.shape[0]),
        in_specs=[
            pl.BlockSpec((NSA_GROUP, bq, d), lambda h, p, qi, kj, c: (OFF_NQ // gw + h, qi[p], 0)),
            pl.BlockSpec((None, bq, d), lambda h, p, qi, kj, c: (OFF_KS // d + h, kj[p], 0)),
            pl.BlockSpec((None, bq, d), lambda h, p, qi, kj, c: (OFF_VS // d + h, kj[p], 0)),
            pl.BlockSpec((None, bq, LANES), lambda h, p, qi, kj, c: (h, qi[p], 0)),
            pl.BlockSpec((None, 2 * LANES, bq), lambda h, p, qi, kj, c: (kj[p], 0, 0)),
            pl.BlockSpec((None, NSA_GROUP, 1, LANES), lambda h, p, qi, kj, c: (h, 0, 0, 0)),
            pl.BlockSpec((bq, LANES), const2),
            pl.BlockSpec((bq, LANES), const2),
            pl.BlockSpec((bq, bq), const2),
        ],
        out_specs=pl.BlockSpec((bq, gw), lambda h, p, qi, kj, c: (qi[p], h)),
        scratch_shapes=[pltpu.VMEM((NSA_GROUP * bq, 2 * d), BF16),
                        pltpu.VMEM((NSA_GROUP * bq, LANES), F32),
                        pltpu.VMEM((NSA_GROUP * bq, 2 * d), F32)],
    )
    return pl.pallas_call(
        functools.partial(_sel_kernel, bq=bq),
        grid_spec=grid_spec,
        out_shape=jax.ShapeDtypeStruct((s, NSA_WIDTH), F32),
        compiler_params=_params(2),
        name="nsa_selected",
    )(qi, kj, cvals, proj, proj, proj, sel, _expand_aug(s, bq),
      qext.reshape(NSA_KV_HEADS, NSA_GROUP, 1, LANES), _kpos_ext(bq), _one_col(bq), _causal_tile(bq))


def _win_kernel(c_ref, q_ref, kp_ref, kc_ref, vp_ref, vc_ref, one_ref, oc_ref, os_ref, gate_ref, o_ref,
                bias_ref):
    hkv = pl.program_id(0)
    i = pl.program_id(1)
    bq = WINDOW

    @pl.when(i == 0)
    def _():
        rel = (lax.broadcasted_iota(jnp.int32, (bq, bq), 0)
               - lax.broadcasted_iota(jnp.int32, (bq, bq), 1))
        relf = rel.astype(F32)
        for g in range(NSA_GROUP):
            c = c_ref[hkv * NSA_GROUP + g]
            bias_ref[g, 0] = jnp.where(rel < 0, -c * (relf + WINDOW), NEG)
            bias_ref[g, 1] = jnp.where(rel >= 0, -c * relf, NEG)

    q = q_ref[...].reshape(NSA_GROUP * bq, HEAD_DIM)
    sp_all = lax.dot_general(q, kp_ref[...], NT_DIMS, preferred_element_type=F32)
    sc_all = lax.dot_general(q, kc_ref[...], NT_DIMS, preferred_element_type=F32)
    first = jnp.where(i == 0, NEG, 0.0)
    one = one_ref[...]
    vp = jnp.concatenate([vp_ref[...], one], axis=1)
    vc = jnp.concatenate([vc_ref[...], one], axis=1)
    gates = gate_ref[...]
    for g in range(NSA_GROUP):
        rows = slice(g * bq, (g + 1) * bq)
        cols = slice(g * HEAD_DIM, (g + 1) * HEAD_DIM)
        sp = sp_all[rows] + bias_ref[g, 0] + first
        sc = sc_all[rows] + bias_ref[g, 1]
        mx = jnp.maximum(jnp.max(sp, axis=1, keepdims=True), jnp.max(sc, axis=1, keepdims=True))
        pp = jnp.exp2(sp - mx).astype(BF16)
        pc = jnp.exp2(sc - mx).astype(BF16)
        pv = (jnp.dot(pp, vp, preferred_element_type=F32)
              + jnp.dot(pc, vc, preferred_element_type=F32))
        ow = pv[:, :HEAD_DIM] / pv[:, HEAD_DIM:HEAD_DIM + 1]
        gc = gates[:, 3 * g + 0:3 * g + 1]
        gs = gates[:, 3 * g + 1:3 * g + 2]
        gw = gates[:, 3 * g + 2:3 * g + 3]
        o_ref[:, cols] = (gc * oc_ref[:, cols] + gs * os_ref[:, cols] + gw * ow).astype(o_ref.dtype)


def _win_combine(proj, o_cmp, o_sel, gates, cvals):
    s = proj.shape[1]
    bq = WINDOW
    gw = NSA_GROUP * HEAD_DIM
    d = HEAD_DIM
    prev = lambda i: jnp.maximum(i - 1, 0)
    grid_spec = pltpu.PrefetchScalarGridSpec(
        num_scalar_prefetch=1,
        grid=(NSA_KV_HEADS, s // bq),
        in_specs=[
            pl.BlockSpec((NSA_GROUP, bq, d), lambda h, i, c: (OFF_NQ // gw + h, i, 0)),
            pl.BlockSpec((None, bq, d), lambda h, i, c: (OFF_KW // d + h, prev(i), 0)),
            pl.BlockSpec((None, bq, d), lambda h, i, c: (OFF_KW // d + h, i, 0)),
            pl.BlockSpec((None, bq, d), lambda h, i, c: (OFF_VW // d + h, prev(i), 0)),
            pl.BlockSpec((None, bq, d), lambda h, i, c: (OFF_VW // d + h, i, 0)),
            pl.BlockSpec((bq, LANES), lambda h, i, c: (0, 0)),
            pl.BlockSpec((bq, gw), lambda h, i, c: (i, h)),
            pl.BlockSpec((bq, gw), lambda h, i, c: (i, h)),
            pl.BlockSpec((bq, LANES), lambda h, i, c: (i, h)),
        ],
        out_specs=pl.BlockSpec((bq, gw), lambda h, i, c: (i, h)),
        scratch_shapes=[pltpu.VMEM((NSA_GROUP, 2, bq, bq), F32)],
    )
    return pl.pallas_call(
        _win_kernel,
        grid_spec=grid_spec,
        out_shape=jax.ShapeDtypeStruct((s, NSA_WIDTH), BF16),
        compiler_params=_params(2),
        name="nsa_window_combine",
    )(cvals, proj, proj, proj, proj, proj, _one_col(bq), o_cmp, o_sel, gates)


def _ffn(h, norm_g, w_gate, w_up, w_down):
    u = _rmsnorm(h, norm_g, BF16)
    act = _ffn_up(u, w_gate, w_up)
    return _mm_res(act, w_down.astype(BF16), h, 0.5)


def _overlap_matrix(s, rows):
    n_cmp = (s - CMP_BLOCK) // CMP_STRIDE + 1
    n_sel = s // SEL_BLOCK
    cs = np.arange(rows)[:, None] * CMP_STRIDE
    ss = np.arange(LANES)[None, :] * SEL_BLOCK
    ov = (cs <= ss + SEL_BLOCK - 1) & (cs + CMP_BLOCK - 1 >= ss)
    ov &= (np.arange(rows)[:, None] < n_cmp) & (np.arange(LANES)[None, :] < n_sel)
    return jnp.asarray(ov, BF16)


def _query_colscale():
    cs = np.ones((1, OFF_G), np.float32)
    cs[:, OFF_DQ:OFF_DK] = QSCALE
    cs[:, OFF_NQ:OFF_KC] = QSCALE
    return jnp.asarray(cs)


def _layer(h, layer, ffn1_norm, ffn1_w_gate, ffn1_w_up, ffn1_w_down, mix_norm, w_in, gate_bias,
           lambda_q1, lambda_k1, lambda_q2, lambda_k2, diff_norm,
           cmp_pos_k, cmp_w1_k, cmp_w2_k, cmp_pos_v, cmp_w1_v, cmp_w2_v,
           w_out, ffn2_norm, ffn2_w_gate, ffn2_w_up, ffn2_w_down):
    s = h.shape[0]
    assert s % WINDOW == 0 and s // SEL_BLOCK <= LANES and OFF_G % 512 == 0
    h = _ffn(h, ffn1_norm, ffn1_w_gate, ffn1_w_up, ffn1_w_down)

    u = _rmsnorm(h, mix_norm, BF16)
    proj = _proj_slabs(u, w_in, _query_colscale())
    wg = w_in[:, OFF_G:].reshape(-1, NSA_KV_HEADS, GATES_PER_KV)
    wg = jnp.pad(wg, ((0, 0), (0, 0), (0, LANES - GATES_PER_KV))).reshape(-1, NSA_KV_HEADS * LANES)
    gb = jnp.pad(gate_bias.astype(F32).reshape(NSA_KV_HEADS, GATES_PER_KV),
                 ((0, 0), (0, LANES - GATES_PER_KV))).reshape(1, NSA_KV_HEADS * LANES)
    gates = _gate_proj(u, wg.astype(BF16), gb)

    lam_rows = jnp.pad(jnp.stack([lambda_q1, lambda_k1, lambda_q2, lambda_k2]).astype(F32),
                       ((0, 4), (0, 0)))
    o_diff = _diff_attention(proj, lam_rows, diff_norm.reshape(1, -1).astype(F32), layer,
                             _pick(s, 512))

    nsa_c, nsa_qext = _alibi_consts(NSA_HEADS)
    rows = s // CMP_STRIDE
    kv_cmp = proj[OFF_KC // LANES:OFF_KS // LANES].reshape(
        2, NSA_KV_HEADS, rows, CMP_STRIDE * HEAD_DIM)
    pos = jnp.stack([cmp_pos_k, cmp_pos_v]).astype(F32).reshape(2, 1, CMP_BLOCK * HEAD_DIM)
    w1 = jnp.stack([cmp_w1_k, cmp_w1_v]).astype(BF16)
    w2 = jnp.stack([cmp_w2_k, cmp_w2_v]).astype(BF16)
    kvc = _compress(kv_cmp, pos, w1, w2)
    o_cmp, sel = _cmp_select(proj, kvc, _overlap_matrix(s, rows), nsa_c, _pick(s, 256))
    o_sel = _sel_attention(proj, sel, nsa_c, nsa_qext, _pick(s, 512))
    o_nsa = _win_combine(proj, o_cmp, o_sel, gates, nsa_c)

    wo = w_out.astype(BF16)
    h = _mm2_res(o_diff, o_nsa, wo[:DIFF_WIDTH], wo[DIFF_WIDTH:], h)
    return _ffn(h, ffn2_norm, ffn2_w_gate, ffn2_w_up, ffn2_w_down)


def kernel(x, ffn1_norm, ffn1_w_gate, ffn1_w_up, ffn1_w_down, mix_norm, w_in, gate_bias, lambda_q1, lambda_k1, lambda_q2, lambda_k2, diff_norm, cmp_pos_k, cmp_w1_k, cmp_w2_k, cmp_pos_v, cmp_w1_v, cmp_w2_v, w_out, ffn2_norm, ffn2_w_gate, ffn2_w_up, ffn2_w_down, final_norm):
    b, s, d = x.shape
    per_layer = (ffn1_norm, ffn1_w_gate, ffn1_w_up, ffn1_w_down, mix_norm, w_in, gate_bias,
                 lambda_q1, lambda_k1, lambda_q2, lambda_k2, diff_norm,
                 cmp_pos_k, cmp_w1_k, cmp_w2_k, cmp_pos_v, cmp_w1_v, cmp_w2_v,
                 w_out, ffn2_norm, ffn2_w_gate, ffn2_w_up, ffn2_w_down)
    outs = []
    for bi in range(b):
        h = x.reshape(s, d) if b == 1 else x[bi]
        for layer in range(DEPTH):
            h = _layer(h, layer, *[p[layer] for p in per_layer])
        outs.append(_rmsnorm(h, final_norm, x.dtype))
    return outs[0].reshape(1, s, d) if b == 1 else jnp.stack(outs)
```

```python
import functools
import math

import ml_dtypes
import numpy as np
import jax
import jax.numpy as jnp
from jax import lax
from jax.experimental import pallas as pl
from jax.experimental.pallas import tpu as pltpu

D_MODEL = 4096
DEPTH = 1
HEAD_DIM = 128
DIFF_V_DIM = 2 * HEAD_DIM
DIFF_HEADS = (D_MODEL // 2) // DIFF_V_DIM
DIFF_WIDTH = DIFF_HEADS * DIFF_V_DIM
NSA_HEADS = (D_MODEL - DIFF_WIDTH) // HEAD_DIM
NSA_KV_HEADS = 4
NSA_GROUP = NSA_HEADS // NSA_KV_HEADS
NSA_WIDTH = NSA_HEADS * HEAD_DIM
CMP_BLOCK = 32
CMP_STRIDE = 16
CMP_HIDDEN = 256
SEL_BLOCK = 64
SEL_TOPN = 16
WINDOW = 512
EPS = 1e-6
NEG = -1e30
FORCE_SCORE = 1e4

LANES = 128
VMEM_LIMIT = 56 * 1024 * 1024
LOG2E = 1.4426950408889634
QSCALE = HEAD_DIM ** -0.5 * LOG2E
MASK_BIG = 2.0 ** 100
KPOS_SPLIT = 32
OFF_DQ = 0
OFF_DK = OFF_DQ + DIFF_HEADS * 2 * HEAD_DIM
OFF_DV = OFF_DK + DIFF_HEADS * 2 * HEAD_DIM
OFF_NQ = OFF_DV + DIFF_HEADS * DIFF_V_DIM
OFF_KC = OFF_NQ + NSA_HEADS * HEAD_DIM
OFF_VC = OFF_KC + NSA_KV_HEADS * HEAD_DIM
OFF_KS = OFF_VC + NSA_KV_HEADS * HEAD_DIM
OFF_VS = OFF_KS + NSA_KV_HEADS * HEAD_DIM
OFF_KW = OFF_VS + NSA_KV_HEADS * HEAD_DIM
OFF_VW = OFF_KW + NSA_KV_HEADS * HEAD_DIM
OFF_G = OFF_VW + NSA_KV_HEADS * HEAD_DIM
GATES_PER_KV = 3 * NSA_GROUP

F32 = jnp.float32
BF16 = jnp.bfloat16
NT_DIMS = (((1,), (1,)), ((), ()))


def _params(n_axes):
    return pltpu.CompilerParams(dimension_semantics=("arbitrary",) * n_axes,
                                vmem_limit_bytes=VMEM_LIMIT)


def _pick(n, pref):
    b = min(pref, n)
    while n % b:
        b //= 2
    return b


def _rmsnorm_kernel(x_ref, g_ref, o_ref):
    x = x_ref[...]
    ms = jnp.mean(x * x, axis=-1, keepdims=True)
    o_ref[...] = (x * lax.rsqrt(ms + EPS) * g_ref[...]).astype(o_ref.dtype)


def _rmsnorm(x, g, out_dtype):
    s, d = x.shape
    bm = _pick(s, 256)
    return pl.pallas_call(
        _rmsnorm_kernel,
        grid=(s // bm,),
        in_specs=[pl.BlockSpec((bm, d), lambda i: (i, 0)),
                  pl.BlockSpec((1, d), lambda i: (0, 0))],
        out_specs=pl.BlockSpec((bm, d), lambda i: (i, 0)),
        out_shape=jax.ShapeDtypeStruct((s, d), out_dtype),
        compiler_params=_params(1),
        name="rmsnorm",
    )(x, g.reshape(1, d).astype(F32))


def _ffn_up_kernel(u_ref, wg_ref, wu_ref, o_ref, wgb_ref, wub_ref):
    @pl.when(pl.program_id(1) == 0)
    def _():
        wgb_ref[...] = wg_ref[...].astype(BF16)
        wub_ref[...] = wu_ref[...].astype(BF16)

    u = u_ref[...]
    g = jnp.dot(u, wgb_ref[...], preferred_element_type=F32)
    up = jnp.dot(u, wub_ref[...], preferred_element_type=F32)
    o_ref[...] = (g * jax.nn.sigmoid(g) * up).astype(o_ref.dtype)


def _ffn_up(u, wg, wu):
    s, d = u.shape
    f = wg.shape[1]
    bm, bn = _pick(s, 1024), min(256, f)
    return pl.pallas_call(
        _ffn_up_kernel,
        grid=(pl.cdiv(f, bn), s // bm),
        in_specs=[pl.BlockSpec((bm, d), lambda j, i: (i, 0)),
                  pl.BlockSpec((d, bn), lambda j, i: (0, j)),
                  pl.BlockSpec((d, bn), lambda j, i: (0, j))],
        out_specs=pl.BlockSpec((bm, bn), lambda j, i: (i, j)),
        out_shape=jax.ShapeDtypeStruct((s, f), BF16),
        scratch_shapes=[pltpu.VMEM((d, bn), BF16), pltpu.VMEM((d, bn), BF16)],
        compiler_params=_params(2),
        name="ffn_up",
    )(u, wg, wu)


def _mm_res_kernel(a_ref, b_ref, r_ref, o_ref, *, alpha):
    o_ref[...] = r_ref[...] + alpha * jnp.dot(a_ref[...], b_ref[...], preferred_element_type=F32)


def _mm_res(a, b, res, alpha):
    s, kdim = a.shape
    n = b.shape[1]
    bm, bn = _pick(s, 512), _pick(n, 512)
    return pl.pallas_call(
        functools.partial(_mm_res_kernel, alpha=alpha),
        grid=(s // bm, n // bn),
        in_specs=[pl.BlockSpec((bm, kdim), lambda i, j: (i, 0)),
                  pl.BlockSpec((kdim, bn), lambda i, j: (0, j)),
                  pl.BlockSpec((bm, bn), lambda i, j: (i, j))],
        out_specs=pl.BlockSpec((bm, bn), lambda i, j: (i, j)),
        out_shape=jax.ShapeDtypeStruct((s, n), F32),
        compiler_params=_params(2),
        name="mm_res",
    )(a, b, res)


def _proj_kernel(a_ref, bt_ref, cs_ref, o_ref, bb_ref):
    @pl.when(pl.program_id(1) == 0)
    def _():
        bb_ref[...] = bt_ref[...].astype(BF16)

    acc = lax.dot_general(a_ref[...], bb_ref[...], NT_DIMS, preferred_element_type=F32) * cs_ref[...]
    for t in range(o_ref.shape[0]):
        o_ref[t] = acc[:, t * LANES:(t + 1) * LANES].astype(o_ref.dtype)


def _proj_slabs(a, bt, colscale):
    s, kdim = a.shape
    n = colscale.shape[1]
    bm, bn = _pick(s, 1024), _pick(n, 512)
    return pl.pallas_call(
        _proj_kernel,
        grid=(n // bn, s // bm),
        in_specs=[pl.BlockSpec((bm, kdim), lambda j, i: (i, 0)),
                  pl.BlockSpec((bn, kdim), lambda j, i: (j, 0)),
                  pl.BlockSpec((1, bn), lambda j, i: (0, j))],
        out_specs=pl.BlockSpec((bn // LANES, bm, LANES), lambda j, i: (j, i, 0)),
        out_shape=jax.ShapeDtypeStruct((n // LANES, s, LANES), BF16),
        scratch_shapes=[pltpu.VMEM((bn, kdim), BF16)],
        compiler_params=_params(2),
        name="proj_slabs",
    )(a, bt, colscale)


def _gate_kernel(a_ref, b_ref, bias_ref, o_ref):
    z = jnp.dot(a_ref[...], b_ref[...], preferred_element_type=F32) + bias_ref[...]
    o_ref[...] = jax.nn.sigmoid(z)


def _gate_proj(u, wg, bias):
    s, kdim = u.shape
    n = wg.shape[1]
    bm = _pick(s, 1024)
    return pl.pallas_call(
        _gate_kernel,
        grid=(s // bm,),
        in_specs=[pl.BlockSpec((bm, kdim), lambda i: (i, 0)),
                  pl.BlockSpec((kdim, n), lambda i: (0, 0)),
                  pl.BlockSpec((1, n), lambda i: (0, 0))],
        out_specs=pl.BlockSpec((bm, n), lambda i: (i, 0)),
        out_shape=jax.ShapeDtypeStruct((s, n), F32),
        compiler_params=_params(1),
        name="gate_proj",
    )(u, wg, bias)


def _mm2_res_kernel(a1_ref, a2_ref, b1_ref, b2_ref, r_ref, o_ref):
    acc = jnp.dot(a1_ref[...], b1_ref[...], preferred_element_type=F32)
    acc += jnp.dot(a2_ref[...], b2_ref[...], preferred_element_type=F32)
    o_ref[...] = r_ref[...] + acc


def _mm2_res(a1, a2, b1, b2, res):
    s, k1 = a1.shape
    k2 = a2.shape[1]
    n = b1.shape[1]
    bm, bn = _pick(s, 1024), _pick(n, 512)
    return pl.pallas_call(
        _mm2_res_kernel,
        grid=(s // bm, n // bn),
        in_specs=[pl.BlockSpec((bm, k1), lambda i, j: (i, 0)),
                  pl.BlockSpec((bm, k2), lambda i, j: (i, 0)),
                  pl.BlockSpec((k1, bn), lambda i, j: (0, j)),
                  pl.BlockSpec((k2, bn), lambda i, j: (0, j)),
                  pl.BlockSpec((bm, bn), lambda i, j: (i, j))],
        out_specs=pl.BlockSpec((bm, bn), lambda i, j: (i, j)),
        out_shape=jax.ShapeDtypeStruct((s, n), F32),
        compiler_params=_params(2),
        name="mm2_res",
    )(a1, a2, b1, b2, res)


def _tri_pairs(n):
    qi = np.repeat(np.arange(n), np.arange(1, n + 1))
    kj = np.concatenate([np.arange(i + 1) for i in range(n)])
    return jnp.asarray(qi, jnp.int32), jnp.asarray(kj, jnp.int32)


def _bf16_round(x):
    return np.asarray(x, np.float32).astype(ml_dtypes.bfloat16).astype(np.float64)


def _alibi_consts(n_heads):
    c = np.float32(2.0 ** (-8.0 * np.arange(1, n_heads + 1) / n_heads) * LOG2E).astype(np.float64)
    hi = _bf16_round(c)
    mid = _bf16_round(c - hi)
    lo = _bf16_round(c - hi - mid)
    rows = np.zeros((n_heads, 1, LANES), np.float32)
    for col, piece in enumerate((hi, hi, mid, mid, lo, lo)):
        rows[:, 0, col] = piece
    return jnp.asarray(c, F32), jnp.asarray(rows, BF16)


def _kpos_ext(bk):
    pos = np.arange(bk)
    a = (pos // KPOS_SPLIT) * KPOS_SPLIT
    b = pos % KPOS_SPLIT
    ext = np.zeros((bk, LANES), np.float32)
    for col in range(0, 6, 2):
        ext[:, col] = a
        ext[:, col + 1] = b
    return jnp.asarray(ext, BF16)


def _one_col(rows):
    x = np.zeros((rows, LANES), np.float32)
    x[:, 0] = 1.0
    return jnp.asarray(x, BF16)


def _causal_tile(b):
    r = np.arange(b)
    return jnp.asarray(np.where(r[:, None] >= r[None, :], 0.0, NEG), F32)


def _lanes(x, width):
    if width == LANES:
        return x
    return jnp.tile(x, (1, width // LANES))


def _flash_update(s, c, m_ref, rows):
    m_prev = m_ref[rows, :]
    m_cur = jnp.max(s, axis=1, keepdims=True) + c
    m_next = jnp.maximum(m_prev, m_cur)
    m_ref[rows, :] = m_next
    p = jnp.exp2(s - _lanes(m_next - c, s.shape[1]))
    return p, jnp.exp2(m_prev - m_next)


def _diff_kernel(qi_ref, kj_ref, c_ref, q_ref, k_ref, v_ref, qext_ref, kext_ref, causal_ref,
                 lam_ref, g_ref, o_ref, qa_ref, m_ref, l_ref, acc_ref, *, bq, lam0):
    h = pl.program_id(0)
    p = pl.program_id(1)
    qi = qi_ref[p]
    kj = kj_ref[p]

    @pl.when(kj == 0)
    def _():
        m_ref[...] = jnp.full_like(m_ref, NEG)
        l_ref[...] = jnp.zeros_like(l_ref)
        acc_ref[...] = jnp.zeros_like(acc_ref)
        ext = jnp.broadcast_to(qext_ref[...], (bq, LANES))
        for mp in range(2):
            qa_ref[mp, :, :HEAD_DIM] = q_ref[mp]
            qa_ref[mp, :, HEAD_DIM:] = ext

    def step(diag):
        c = 0.0 if diag else -c_ref[h] * ((qi - kj) * bq).astype(F32)
        v = jnp.concatenate([v_ref[0], v_ref[1]], axis=1)
        kext = kext_ref[...]
        for mp in range(2):
            ka = jnp.concatenate([k_ref[mp], kext], axis=1)
            s = lax.dot_general(qa_ref[mp], ka, NT_DIMS, preferred_element_type=F32)
            if diag:
                s = s + causal_ref[...]
            rows = pl.ds(mp * bq, bq)
            pr, alpha = _flash_update(s, c, m_ref, rows)
            l_ref[rows, :] = alpha * l_ref[rows, :] + jnp.sum(pr, axis=1, keepdims=True)
            pv = jnp.dot(pr.astype(BF16), v, preferred_element_type=F32)
            acc_ref[rows, :] = _lanes(alpha, DIFF_V_DIM) * acc_ref[rows, :] + pv

    @pl.when(kj < qi)
    def _():
        step(False)

    @pl.when(kj == qi)
    def _():
        step(True)
        lam_rows = lam_ref[...]
        d1 = jnp.sum(lam_rows[0:1] * lam_rows[1:2], axis=1, keepdims=True)
        d2 = jnp.sum(lam_rows[2:3] * lam_rows[3:4], axis=1, keepdims=True)
        lam = jnp.exp(d1) - jnp.exp(d2) + lam0
        o1 = acc_ref[0:bq, :] / _lanes(l_ref[0:bq, :], DIFF_V_DIM)
        o2 = acc_ref[bq:2 * bq, :] / _lanes(l_ref[bq:2 * bq, :], DIFF_V_DIM)
        o = o1 - lam * o2
        ms = jnp.mean(o * o, axis=-1, keepdims=True)
        y = o * lax.rsqrt(ms + EPS) * g_ref[...]
        o_ref[...] = (y * (1.0 - lam0)).astype(o_ref.dtype)


def _diff_attention(proj, lam_rows, gain, layer, bq):
    s = proj.shape[1]
    qi, kj = _tri_pairs(s // bq)
    cvals, qext = _alibi_consts(DIFF_HEADS)
    lam0 = 0.8 - 0.6 * math.exp(-0.3 * layer)
    w = DIFF_V_DIM
    const2 = lambda h, p, qi, kj, c: (0, 0)
    grid_spec = pltpu.PrefetchScalarGridSpec(
        num_scalar_prefetch=3,
        grid=(DIFF_HEADS, qi.shape[0]),
        in_specs=[
            pl.BlockSpec((2, bq, LANES), lambda h, p, qi, kj, c: (OFF_DQ // w + h, qi[p], 0)),
            pl.BlockSpec((2, bq, LANES), lambda h, p, qi, kj, c: (OFF_DK // w + h, kj[p], 0)),
            pl.BlockSpec((2, bq, LANES), lambda h, p, qi, kj, c: (OFF_DV // w + h, kj[p], 0)),
            pl.BlockSpec((None, 1, LANES), lambda h, p, qi, kj, c: (h, 0, 0)),
            pl.BlockSpec((bq, LANES), const2),
            pl.BlockSpec((bq, bq), const2),
            pl.BlockSpec((8, HEAD_DIM), const2),
            pl.BlockSpec((1, w), const2),
        ],
        out_specs=pl.BlockSpec((bq, w), lambda h, p, qi, kj, c: (qi[p], h)),
        scratch_shapes=[pltpu.VMEM((2, bq, 2 * HEAD_DIM), BF16),
                        pltpu.VMEM((2 * bq, LANES), F32),
                        pltpu.VMEM((2 * bq, LANES), F32),
                        pltpu.VMEM((2 * bq, w), F32)],
    )
    return pl.pallas_call(
        functools.partial(_diff_kernel, bq=bq, lam0=lam0),
        grid_spec=grid_spec,
        out_shape=jax.ShapeDtypeStruct((s, DIFF_WIDTH), BF16),
        compiler_params=_params(2),
        name="diff_attention",
    )(qi, kj, cvals, proj, proj, proj, qext, _kpos_ext(bq), _causal_tile(bq), lam_rows, gain)


def _compress_kernel(a_ref, pos_ref, w1_ref, w2_ref, o_ref):
    half = CMP_STRIDE * HEAD_DIM
    a = a_ref[...].astype(F32)
    pos = pos_ref[...]
    top = (a + pos[:, :half]).astype(BF16)
    bot = (a + pos[:, half:]).astype(BF16)
    t = jnp.dot(top, w1_ref[:half, :], preferred_element_type=F32)
    b = jnp.dot(bot, w1_ref[half:, :], preferred_element_type=F32)
    rows = a.shape[0]
    hid = t + pltpu.roll(b, rows - 1, 0)
    act = jax.nn.gelu(hid)
    o_ref[...] = jnp.dot(act.astype(BF16), w2_ref[...], preferred_element_type=F32).astype(o_ref.dtype)


def _compress(a, pos, w1, w2):
    _, hkv, rows, width = a.shape
    return pl.pallas_call(
        _compress_kernel,
        grid=(2, hkv),
        in_specs=[pl.BlockSpec((None, None, rows, width), lambda t, h: (t, h, 0, 0)),
                  pl.BlockSpec((None, 1, 2 * width), lambda t, h: (t, 0, 0)),
                  pl.BlockSpec((None, 2 * width, CMP_HIDDEN), lambda t, h: (t, 0, 0)),
                  pl.BlockSpec((None, CMP_HIDDEN, HEAD_DIM), lambda t, h: (t, 0, 0))],
        out_specs=pl.BlockSpec((None, None, rows, HEAD_DIM), lambda t, h: (t, h, 0, 0)),
        out_shape=jax.ShapeDtypeStruct((2, hkv, rows, HEAD_DIM), BF16),
        compiler_params=_params(2),
        name="nsa_compress",
    )(a, pos, w1, w2)


def _split3(x):
    hi = x.astype(BF16)
    r1 = x - hi.astype(F32)
    mid = r1.astype(BF16)
    lo = (r1 - mid.astype(F32)).astype(BF16)
    return hi, mid, lo


def _cmp_select_kernel(c_ref, q_ref, kc_ref, vc_ref, ov_ref, oc_ref, sel_ref, any_ref,
                       *, bq, n_cmp, topn):
    hkv = pl.program_id(0)
    i = pl.program_id(1)
    ncp = kc_ref.shape[0]
    q = q_ref[...].reshape(NSA_GROUP * bq, HEAD_DIM)
    s = lax.dot_general(q, kc_ref[...], NT_DIMS, preferred_element_type=F32)
    t = i * bq + lax.broadcasted_iota(jnp.int32, (bq, ncp), 0)
    cidx = lax.broadcasted_iota(jnp.int32, (bq, ncp), 1)
    dist = t - (cidx * CMP_STRIDE + CMP_BLOCK - 1)
    ok = (dist >= 0) & (cidx < n_cmp)
    distf = dist.astype(F32)
    vc = vc_ref[...]
    psum = jnp.zeros((bq, ncp), F32)
    for g in range(NSA_GROUP):
        sg = jnp.where(ok, s[g * bq:(g + 1) * bq] - c_ref[hkv * NSA_GROUP + g] * distf, NEG)
        mx = jnp.max(sg, axis=1, keepdims=True)
        e = jnp.where(ok, jnp.exp2(sg - mx), 0.0)
        den = jnp.sum(e, axis=1, keepdims=True)
        pg = e / jnp.where(den > 0.0, den, 1.0)
        oc_ref[:, g * HEAD_DIM:(g + 1) * HEAD_DIM] = jnp.dot(
            pg.astype(BF16), vc, preferred_element_type=F32)
        psum = psum + pg
    ov = ov_ref[...]
    imp = jnp.zeros((bq, LANES), F32)
    for piece in _split3(psum):
        imp = imp + jnp.dot(piece, ov, preferred_element_type=F32)
    imp_t = imp.T
    tq = i * bq + lax.broadcasted_iota(jnp.int32, (LANES, bq), 1)
    blk = lax.broadcasted_iota(jnp.int32, (LANES, bq), 0)
    cur = lax.shift_right_arithmetic(tq, SEL_BLOCK.bit_length() - 1)
    forced = (blk == 0) | (blk == cur) | (blk == cur - 1)
    score = jnp.where(forced, FORCE_SCORE, jnp.where(blk <= cur, imp_t, -1.0))
    removed = -3.0e38
    blkf = blk.astype(F32)

    def pick(_, carry):
        sc, sel = carry
        mx = jnp.max(sc, axis=0, keepdims=True)
        first = jnp.min(jnp.where(sc == mx, blkf, float(LANES)), axis=0, keepdims=True)
        hit = blkf == first
        return jnp.where(hit, removed, sc), jnp.where(hit, 1.0, sel)

    _, sel_t = lax.fori_loop(0, topn, pick, (score, jnp.zeros((LANES, bq), F32)))
    sel = sel_t.T
    sel_ref[...] = sel.astype(sel_ref.dtype)
    any_ref[...] = jnp.max(sel, axis=0, keepdims=True)


def _cmp_select(proj, kvc, overlap, cvals, bq):
    s = proj.shape[1]
    n_cmp = (s - CMP_BLOCK) // CMP_STRIDE + 1
    n_sel = s // SEL_BLOCK
    topn = min(SEL_TOPN, n_sel)
    ncp = kvc.shape[2]
    gw = NSA_GROUP * HEAD_DIM
    grid_spec = pltpu.PrefetchScalarGridSpec(
        num_scalar_prefetch=1,
        grid=(NSA_KV_HEADS, s // bq),
        in_specs=[
            pl.BlockSpec((NSA_GROUP, bq, LANES), lambda h, i, c: (OFF_NQ // gw + h, i, 0)),
            pl.BlockSpec((None, None, ncp, HEAD_DIM), lambda h, i, c: (0, h, 0, 0)),
            pl.BlockSpec((None, None, ncp, HEAD_DIM), lambda h, i, c: (1, h, 0, 0)),
            pl.BlockSpec((ncp, LANES), lambda h, i, c: (0, 0)),
        ],
        out_specs=[pl.BlockSpec((bq, gw), lambda h, i, c: (i, h)),
                   pl.BlockSpec((None, bq, LANES), lambda h, i, c: (h, i, 0)),
                   pl.BlockSpec((None, None, 1, LANES), lambda h, i, c: (h, i, 0, 0))],
    )
    return pl.pallas_call(
        functools.partial(_cmp_select_kernel, bq=bq, n_cmp=n_cmp, topn=topn),
        grid_spec=grid_spec,
        out_shape=[jax.ShapeDtypeStruct((s, NSA_WIDTH), F32),
                   jax.ShapeDtypeStruct((NSA_KV_HEADS, s, LANES), BF16),
                   jax.ShapeDtypeStruct((NSA_KV_HEADS, s // bq, 1, LANES), F32)],
        compiler_params=_params(2),
        name="nsa_cmp_select",
    )(cvals, proj, kvc, kvc, overlap)


def _sel_kernel(qi_ref, kj_ref, nact_ref, c_ref, q_ref, k_ref, v_ref, sel_ref, exp_ref, qext_ref,
                kext_ref, one_ref, causal_ref, o_ref, qa_ref, m_ref, acc_ref, *, bq, n_pairs):
    hkv = pl.program_id(0)
    p = pl.program_id(1)
    qi = qi_ref[hkv * n_pairs + p]
    kj = kj_ref[hkv * n_pairs + p]
    live = p < nact_ref[hkv]

    @pl.when(live & (kj == 0))
    def _():
        m_ref[...] = jnp.full_like(m_ref, NEG)
        acc_ref[...] = jnp.zeros_like(acc_ref)
        for g in range(NSA_GROUP):
            qa_ref[g * bq:(g + 1) * bq, :HEAD_DIM] = q_ref[g]
            qa_ref[g * bq:(g + 1) * bq, HEAD_DIM:] = jnp.broadcast_to(qext_ref[g], (bq, LANES))

    def step(diag):
        one = one_ref[...]
        maskb = jnp.dot(jnp.concatenate([sel_ref[...], one], axis=1), exp_ref[...],
                        preferred_element_type=F32)
        if diag:
            maskb = maskb + causal_ref[...]
        ka = jnp.concatenate([k_ref[...], kext_ref[...]], axis=1)
        va = jnp.concatenate([v_ref[...], one], axis=1)
        s_all = lax.dot_general(qa_ref[...], ka, NT_DIMS, preferred_element_type=F32)
        dtile = ((qi - kj) * bq).astype(F32)
        for g in range(NSA_GROUP):
            c = 0.0 if diag else -c_ref[hkv * NSA_GROUP + g] * dtile
            rows = pl.ds(g * bq, bq)
            pr, alpha = _flash_update(s_all[g * bq:(g + 1) * bq] + maskb, c, m_ref, rows)
            pv = jnp.dot(pr.astype(BF16), va, preferred_element_type=F32)
            acc_ref[rows, :] = _lanes(alpha, 2 * HEAD_DIM) * acc_ref[rows, :] + pv

    @pl.when(live & (kj < qi))
    def _():
        step(False)

    @pl.when(live & (kj == qi))
    def _():
        step(True)
        for g in range(NSA_GROUP):
            rows = pl.ds(g * bq, bq)
            o_ref[:, g * HEAD_DIM:(g + 1) * HEAD_DIM] = (
                acc_ref[rows, :HEAD_DIM] / acc_ref[rows, HEAD_DIM:HEAD_DIM + 1])


def _expand_aug(s, bk):
    key_blk = np.arange(s) // SEL_BLOCK
    e = np.zeros((2 * LANES, s), np.float32)
    e[key_blk, np.arange(s)] = MASK_BIG
    e[LANES, :] = -MASK_BIG
    return jnp.asarray(e.reshape(2 * LANES, s // bk, bk).transpose(1, 0, 2), BF16)


def _sel_schedule(any_rows, s, bq):
    nq = s // bq
    per_tile = bq // SEL_BLOCK
    a = any_rows.reshape(NSA_KV_HEADS, nq, -1, LANES).max(axis=2)
    a = a[:, :, :nq * per_tile].reshape(NSA_KV_HEADS, nq, nq, per_tile).max(axis=-1) > 0.0
    tq, tk = _tri_pairs(nq)
    act = a[:, tq, tk] | (tk == 0)[None, :] | (tk == tq)[None, :]
    order = jnp.argsort(jnp.logical_not(act), axis=1, stable=True)
    nact = jnp.sum(act, axis=1).astype(jnp.int32)
    pos = jnp.minimum(jnp.arange(tq.shape[0], dtype=jnp.int32)[None, :], nact[:, None] - 1)
    idx = jnp.take_along_axis(order, pos, axis=1)
    return tq[idx].reshape(-1), tk[idx].reshape(-1), nact


def _sel_attention(proj, sel, any_rows, cvals, qext, bq):
    s = proj.shape[1]
    qi, kj, nact = _sel_schedule(any_rows, s, bq)
    n_pairs = qi.shape[0] // NSA_KV_HEADS
    gw = NSA_GROUP * HEAD_DIM
    d = HEAD_DIM
    const2 = lambda h, p, qi, kj, na, c: (0, 0)
    grid_spec = pltpu.PrefetchScalarGridSpec(
        num_scalar_prefetch=4,
        grid=(NSA_KV_HEADS, n_pairs),
        in_specs=[
            pl.BlockSpec((NSA_GROUP, bq, d),
                         lambda h, p, qi, kj, na, c: (OFF_NQ // gw + h, qi[h * n_pairs + p], 0)),
            pl.BlockSpec((None, bq, d),
                         lambda h, p, qi, kj, na, c: (OFF_KS // d + h, kj[h * n_pairs + p], 0)),
            pl.BlockSpec((None, bq, d),
                         lambda h, p, qi, kj, na, c: (OFF_VS // d + h, kj[h * n_pairs + p], 0)),
            pl.BlockSpec((None, bq, LANES), lambda h, p, qi, kj, na, c: (h, qi[h * n_pairs + p], 0)),
            pl.BlockSpec((None, 2 * LANES, bq), lambda h, p, qi, kj, na, c: (kj[h * n_pairs + p], 0, 0)),
            pl.BlockSpec((None, NSA_GROUP, 1, LANES), lambda h, p, qi, kj, na, c: (h, 0, 0, 0)),
            pl.BlockSpec((bq, LANES), const2),
            pl.BlockSpec((bq, LANES), const2),
            pl.BlockSpec((bq, bq), const2),
        ],
        out_specs=pl.BlockSpec((bq, gw), lambda h, p, qi, kj, na, c: (qi[h * n_pairs + p], h)),
        scratch_shapes=[pltpu.VMEM((NSA_GROUP * bq, 2 * d), BF16),
                        pltpu.VMEM((NSA_GROUP * bq, LANES), F32),
                        pltpu.VMEM((NSA_GROUP * bq, 2 * d), F32)],
    )
    return pl.pallas_call(
        functools.partial(_sel_kernel, bq=bq, n_pairs=n_pairs),
        grid_spec=grid_spec,
        out_shape=jax.ShapeDtypeStruct((s, NSA_WIDTH), F32),
        compiler_params=_params(2),
        name="nsa_selected",
    )(qi, kj, nact, cvals, proj, proj, proj, sel, _expand_aug(s, bq),
      qext.reshape(NSA_KV_HEADS, NSA_GROUP, 1, LANES), _kpos_ext(bq), _one_col(bq), _causal_tile(bq))


def _win_kernel(c_ref, q_ref, kp_ref, kc_ref, vp_ref, vc_ref, one_ref, oc_ref, os_ref, gate_ref, o_ref,
                bias_ref):
    hkv = pl.program_id(0)
    i = pl.program_id(1)
    bq = WINDOW

    @pl.when(i == 0)
    def _():
        rel = (lax.broadcasted_iota(jnp.int32, (bq, bq), 0)
               - lax.broadcasted_iota(jnp.int32, (bq, bq), 1))
        relf = rel.astype(F32)
        for g in range(NSA_GROUP):
            c = c_ref[hkv * NSA_GROUP + g]
            bias_ref[g, 0] = jnp.where(rel < 0, -c * (relf + WINDOW), NEG)
            bias_ref[g, 1] = jnp.where(rel >= 0, -c * relf, NEG)

    q = q_ref[...].reshape(NSA_GROUP * bq, HEAD_DIM)
    sp_all = lax.dot_general(q, kp_ref[...], NT_DIMS, preferred_element_type=F32)
    sc_all = lax.dot_general(q, kc_ref[...], NT_DIMS, preferred_element_type=F32)
    first = jnp.where(i == 0, NEG, 0.0)
    one = one_ref[...]
    vp = jnp.concatenate([vp_ref[...], one], axis=1)
    vc = jnp.concatenate([vc_ref[...], one], axis=1)
    gates = gate_ref[...]
    for g in range(NSA_GROUP):
        rows = slice(g * bq, (g + 1) * bq)
        cols = slice(g * HEAD_DIM, (g + 1) * HEAD_DIM)
        sp = sp_all[rows] + bias_ref[g, 0] + first
        sc = sc_all[rows] + bias_ref[g, 1]
        mx = jnp.maximum(jnp.max(sp, axis=1, keepdims=True), jnp.max(sc, axis=1, keepdims=True))
        pp = jnp.exp2(sp - mx).astype(BF16)
        pc = jnp.exp2(sc - mx).astype(BF16)
        pv = (jnp.dot(pp, vp, preferred_element_type=F32)
              + jnp.dot(pc, vc, preferred_element_type=F32))
        ow = pv[:, :HEAD_DIM] / pv[:, HEAD_DIM:HEAD_DIM + 1]
        gc = gates[:, 3 * g + 0:3 * g + 1]
        gs = gates[:, 3 * g + 1:3 * g + 2]
        gw = gates[:, 3 * g + 2:3 * g + 3]
        o_ref[:, cols] = (gc * oc_ref[:, cols] + gs * os_ref[:, cols] + gw * ow).astype(o_ref.dtype)


def _win_combine(proj, o_cmp, o_sel, gates, cvals):
    s = proj.shape[1]
    bq = WINDOW
    gw = NSA_GROUP * HEAD_DIM
    d = HEAD_DIM
    prev = lambda i: jnp.maximum(i - 1, 0)
    grid_spec = pltpu.PrefetchScalarGridSpec(
        num_scalar_prefetch=1,
        grid=(NSA_KV_HEADS, s // bq),
        in_specs=[
            pl.BlockSpec((NSA_GROUP, bq, d), lambda h, i, c: (OFF_NQ // gw + h, i, 0)),
            pl.BlockSpec((None, bq, d), lambda h, i, c: (OFF_KW // d + h, prev(i), 0)),
            pl.BlockSpec((None, bq, d), lambda h, i, c: (OFF_KW // d + h, i, 0)),
            pl.BlockSpec((None, bq, d), lambda h, i, c: (OFF_VW // d + h, prev(i), 0)),
            pl.BlockSpec((None, bq, d), lambda h, i, c: (OFF_VW // d + h, i, 0)),
            pl.BlockSpec((bq, LANES), lambda h, i, c: (0, 0)),
            pl.BlockSpec((bq, gw), lambda h, i, c: (i, h)),
            pl.BlockSpec((bq, gw), lambda h, i, c: (i, h)),
            pl.BlockSpec((bq, LANES), lambda h, i, c: (i, h)),
        ],
        out_specs=pl.BlockSpec((bq, gw), lambda h, i, c: (i, h)),
        scratch_shapes=[pltpu.VMEM((NSA_GROUP, 2, bq, bq), F32)],
    )
    return pl.pallas_call(
        _win_kernel,
        grid_spec=grid_spec,
        out_shape=jax.ShapeDtypeStruct((s, NSA_WIDTH), BF16),
        compiler_params=_params(2),
        name="nsa_window_combine",
    )(cvals, proj, proj, proj, proj, proj, _one_col(bq), o_cmp, o_sel, gates)


def _ffn(h, norm_g, w_gate, w_up, w_down):
    u = _rmsnorm(h, norm_g, BF16)
    act = _ffn_up(u, w_gate, w_up)
    return _mm_res(act, w_down.astype(BF16), h, 0.5)


def _overlap_matrix(s, rows):
    n_cmp = (s - CMP_BLOCK) // CMP_STRIDE + 1
    n_sel = s // SEL_BLOCK
    cs = np.arange(rows)[:, None] * CMP_STRIDE
    ss = np.arange(LANES)[None, :] * SEL_BLOCK
    ov = (cs <= ss + SEL_BLOCK - 1) & (cs + CMP_BLOCK - 1 >= ss)
    ov &= (np.arange(rows)[:, None] < n_cmp) & (np.arange(LANES)[None, :] < n_sel)
    return jnp.asarray(ov, BF16)


def _query_colscale():
    cs = np.ones((1, OFF_G), np.float32)
    cs[:, OFF_DQ:OFF_DK] = QSCALE
    cs[:, OFF_NQ:OFF_KC] = QSCALE
    return jnp.asarray(cs)


def _layer(h, layer, ffn1_norm, ffn1_w_gate, ffn1_w_up, ffn1_w_down, mix_norm, w_in, gate_bias,
           lambda_q1, lambda_k1, lambda_q2, lambda_k2, diff_norm,
           cmp_pos_k, cmp_w1_k, cmp_w2_k, cmp_pos_v, cmp_w1_v, cmp_w2_v,
           w_out, ffn2_norm, ffn2_w_gate, ffn2_w_up, ffn2_w_down):
    s = h.shape[0]
    assert s % WINDOW == 0 and s // SEL_BLOCK <= LANES and OFF_G % 512 == 0
    h = _ffn(h, ffn1_norm, ffn1_w_gate, ffn1_w_up, ffn1_w_down)

    u = _rmsnorm(h, mix_norm, BF16)
    w_in_t = w_in.T
    proj = _proj_slabs(u, w_in_t, _query_colscale())
    wg = w_in_t[OFF_G:].T.reshape(-1, NSA_KV_HEADS, GATES_PER_KV)
    wg = jnp.pad(wg, ((0, 0), (0, 0), (0, LANES - GATES_PER_KV))).reshape(-1, NSA_KV_HEADS * LANES)
    gb = jnp.pad(gate_bias.astype(F32).reshape(NSA_KV_HEADS, GATES_PER_KV),
                 ((0, 0), (0, LANES - GATES_PER_KV))).reshape(1, NSA_KV_HEADS * LANES)
    gates = _gate_proj(u, wg.astype(BF16), gb)

    lam_rows = jnp.pad(jnp.stack([lambda_q1, lambda_k1, lambda_q2, lambda_k2]).astype(F32),
                       ((0, 4), (0, 0)))
    o_diff = _diff_attention(proj, lam_rows, diff_norm.reshape(1, -1).astype(F32), layer,
                             _pick(s, 512))

    nsa_c, nsa_qext = _alibi_consts(NSA_HEADS)
    rows = s // CMP_STRIDE
    kv_cmp = proj[OFF_KC // LANES:OFF_KS // LANES].reshape(
        2, NSA_KV_HEADS, rows, CMP_STRIDE * HEAD_DIM)
    pos = jnp.stack([cmp_pos_k, cmp_pos_v]).astype(F32).reshape(2, 1, CMP_BLOCK * HEAD_DIM)
    w1 = jnp.stack([cmp_w1_k, cmp_w1_v]).astype(BF16)
    w2 = jnp.stack([cmp_w2_k, cmp_w2_v]).astype(BF16)
    kvc = _compress(kv_cmp, pos, w1, w2)
    o_cmp, sel, any_rows = _cmp_select(proj, kvc, _overlap_matrix(s, rows), nsa_c, _pick(s, 256))
    o_sel = _sel_attention(proj, sel, any_rows, nsa_c, nsa_qext, _pick(s, 512))
    o_nsa = _win_combine(proj, o_cmp, o_sel, gates, nsa_c)

    wo = w_out.astype(BF16)
    h = _mm2_res(o_diff, o_nsa, wo[:DIFF_WIDTH], wo[DIFF_WIDTH:], h)
    return _ffn(h, ffn2_norm, ffn2_w_gate, ffn2_w_up, ffn2_w_down)


def kernel(x, ffn1_norm, ffn1_w_gate, ffn1_w_up, ffn1_w_down, mix_norm, w_in, gate_bias, lambda_q1, lambda_k1, lambda_q2, lambda_k2, diff_norm, cmp_pos_k, cmp_w1_k, cmp_w2_k, cmp_pos_v, cmp_w1_v, cmp_w2_v, w_out, ffn2_norm, ffn2_w_gate, ffn2_w_up, ffn2_w_down, final_norm):
    b, s, d = x.shape
    per_layer = (ffn1_norm, ffn1_w_gate, ffn1_w_up, ffn1_w_down, mix_norm, w_in, gate_bias,
                 lambda_q1, lambda_k1, lambda_q2, lambda_k2, diff_norm,
                 cmp_pos_k, cmp_w1_k, cmp_w2_k, cmp_pos_v, cmp_w1_v, cmp_w2_v,
                 w_out, ffn2_norm, ffn2_w_gate, ffn2_w_up, ffn2_w_down)
    outs = []
    for bi in range(b):
        h = x.reshape(s, d) if b == 1 else x[bi]
        for layer in range(DEPTH):
            h = _layer(h, layer, *[p[layer] for p in per_layer])
        outs.append(_rmsnorm(h, final_norm, x.dtype))
    return outs[0].reshape(1, s, d) if b == 1 else jnp.stack(outs)
```

```python
import functools
import math

import ml_dtypes
import numpy as np
import jax
import jax.numpy as jnp
from jax import lax
from jax.experimental import pallas as pl
from jax.experimental.pallas import tpu as pltpu

D_MODEL = 4096
DEPTH = 1
HEAD_DIM = 128
DIFF_V_DIM = 2 * HEAD_DIM
DIFF_HEADS = (D_MODEL // 2) // DIFF_V_DIM
DIFF_WIDTH = DIFF_HEADS * DIFF_V_DIM
NSA_HEADS = (D_MODEL - DIFF_WIDTH) // HEAD_DIM
NSA_KV_HEADS = 4
NSA_GROUP = NSA_HEADS // NSA_KV_HEADS
NSA_WIDTH = NSA_HEADS * HEAD_DIM
CMP_BLOCK = 32
CMP_STRIDE = 16
CMP_HIDDEN = 256
SEL_BLOCK = 64
SEL_TOPN = 16
WINDOW = 512
EPS = 1e-6
NEG = -1e30
FORCE_SCORE = 1e4

LANES = 128
VMEM_LIMIT = 56 * 1024 * 1024
LOG2E = 1.4426950408889634
QSCALE = HEAD_DIM ** -0.5 * LOG2E
MASK_BIG = 2.0 ** 100
KPOS_SPLIT = 32
OFF_DQ = 0
OFF_DK = OFF_DQ + DIFF_HEADS * 2 * HEAD_DIM
OFF_DV = OFF_DK + DIFF_HEADS * 2 * HEAD_DIM
OFF_NQ = OFF_DV + DIFF_HEADS * DIFF_V_DIM
OFF_KC = OFF_NQ + NSA_HEADS * HEAD_DIM
OFF_VC = OFF_KC + NSA_KV_HEADS * HEAD_DIM
OFF_KS = OFF_VC + NSA_KV_HEADS * HEAD_DIM
OFF_VS = OFF_KS + NSA_KV_HEADS * HEAD_DIM
OFF_KW = OFF_VS + NSA_KV_HEADS * HEAD_DIM
OFF_VW = OFF_KW + NSA_KV_HEADS * HEAD_DIM
OFF_G = OFF_VW + NSA_KV_HEADS * HEAD_DIM
GATES_PER_KV = 3 * NSA_GROUP

F32 = jnp.float32
BF16 = jnp.bfloat16
NT_DIMS = (((1,), (1,)), ((), ()))


def _params(n_axes):
    return pltpu.CompilerParams(dimension_semantics=("arbitrary",) * n_axes,
                                vmem_limit_bytes=VMEM_LIMIT)


def _pick(n, pref):
    b = min(pref, n)
    while n % b:
        b //= 2
    return b


def _rmsnorm_kernel(x_ref, g_ref, o_ref):
    x = x_ref[...]
    ms = jnp.mean(x * x, axis=-1, keepdims=True)
    o_ref[...] = (x * lax.rsqrt(ms + EPS) * g_ref[...]).astype(o_ref.dtype)


def _rmsnorm(x, g, out_dtype):
    s, d = x.shape
    bm = _pick(s, 256)
    return pl.pallas_call(
        _rmsnorm_kernel,
        grid=(s // bm,),
        in_specs=[pl.BlockSpec((bm, d), lambda i: (i, 0)),
                  pl.BlockSpec((1, d), lambda i: (0, 0))],
        out_specs=pl.BlockSpec((bm, d), lambda i: (i, 0)),
        out_shape=jax.ShapeDtypeStruct((s, d), out_dtype),
        compiler_params=_params(1),
        name="rmsnorm",
    )(x, g.reshape(1, d).astype(F32))


def _ffn_up_kernel(u_ref, wg_ref, wu_ref, o_ref):
    u = u_ref[...]
    g = jnp.dot(u, wg_ref[...].astype(BF16), preferred_element_type=F32)
    up = jnp.dot(u, wu_ref[...].astype(BF16), preferred_element_type=F32)
    o_ref[...] = (g * jax.nn.sigmoid(g) * up).astype(o_ref.dtype)


def _ffn_up(u, wg, wu):
    s, d = u.shape
    f = wg.shape[1]
    bm, bn = _pick(s, 2048), _pick(f, 256)
    return pl.pallas_call(
        _ffn_up_kernel,
        grid=(s // bm, f // bn),
        in_specs=[pl.BlockSpec((bm, d), lambda i, j: (i, 0), pipeline_mode=pl.Buffered(1)),
                  pl.BlockSpec((d, bn), lambda i, j: (0, j)),
                  pl.BlockSpec((d, bn), lambda i, j: (0, j))],
        out_specs=pl.BlockSpec((bm, bn), lambda i, j: (i, j)),
        out_shape=jax.ShapeDtypeStruct((s, f), BF16),
        compiler_params=_params(2),
        name="ffn_up",
    )(u, wg, wu)


def _mm_res_kernel(a_ref, b_ref, r_ref, o_ref, *, alpha):
    o_ref[...] = r_ref[...] + alpha * jnp.dot(a_ref[...], b_ref[...], preferred_element_type=F32)


def _mm_res(a, b, res, alpha):
    s, kdim = a.shape
    n = b.shape[1]
    bm, bn = _pick(s, 512), _pick(n, 512)
    return pl.pallas_call(
        functools.partial(_mm_res_kernel, alpha=alpha),
        grid=(s // bm, n // bn),
        in_specs=[pl.BlockSpec((bm, kdim), lambda i, j: (i, 0)),
                  pl.BlockSpec((kdim, bn), lambda i, j: (0, j)),
                  pl.BlockSpec((bm, bn), lambda i, j: (i, j))],
        out_specs=pl.BlockSpec((bm, bn), lambda i, j: (i, j)),
        out_shape=jax.ShapeDtypeStruct((s, n), F32),
        compiler_params=_params(2),
        name="mm_res",
    )(a, b, res)


def _proj_kernel(a_ref, bt_ref, cs_ref, o_ref):
    acc = lax.dot_general(a_ref[...], bt_ref[...].astype(BF16), NT_DIMS,
                          preferred_element_type=F32) * cs_ref[...]
    for t in range(o_ref.shape[0]):
        o_ref[t] = acc[:, t * LANES:(t + 1) * LANES].astype(o_ref.dtype)


def _proj_slabs(a, bt, colscale):
    s, kdim = a.shape
    n = colscale.shape[1]
    bm, bn = _pick(s, 2048), _pick(n, 512)
    return pl.pallas_call(
        _proj_kernel,
        grid=(s // bm, n // bn),
        in_specs=[pl.BlockSpec((bm, kdim), lambda i, j: (i, 0), pipeline_mode=pl.Buffered(1)),
                  pl.BlockSpec((bn, kdim), lambda i, j: (j, 0)),
                  pl.BlockSpec((1, bn), lambda i, j: (0, j))],
        out_specs=pl.BlockSpec((bn // LANES, bm, LANES), lambda i, j: (j, i, 0)),
        out_shape=jax.ShapeDtypeStruct((n // LANES, s, LANES), BF16),
        compiler_params=_params(2),
        name="proj_slabs",
    )(a, bt, colscale)


def _gate_kernel(a_ref, b_ref, bias_ref, o_ref):
    z = jnp.dot(a_ref[...], b_ref[...], preferred_element_type=F32) + bias_ref[...]
    o_ref[...] = jax.nn.sigmoid(z)


def _gate_proj(u, wg, bias):
    s, kdim = u.shape
    n = wg.shape[1]
    bm = _pick(s, 1024)
    return pl.pallas_call(
        _gate_kernel,
        grid=(s // bm,),
        in_specs=[pl.BlockSpec((bm, kdim), lambda i: (i, 0)),
                  pl.BlockSpec((kdim, n), lambda i: (0, 0)),
                  pl.BlockSpec((1, n), lambda i: (0, 0))],
        out_specs=pl.BlockSpec((bm, n), lambda i: (i, 0)),
        out_shape=jax.ShapeDtypeStruct((s, n), F32),
        compiler_params=_params(1),
        name="gate_proj",
    )(u, wg, bias)


def _mm2_res_kernel(a1_ref, a2_ref, b1_ref, b2_ref, r_ref, o_ref):
    acc = jnp.dot(a1_ref[...], b1_ref[...], preferred_element_type=F32)
    acc += jnp.dot(a2_ref[...], b2_ref[...], preferred_element_type=F32)
    o_ref[...] = r_ref[...] + acc


def _mm2_res(a1, a2, b1, b2, res):
    s, k1 = a1.shape
    k2 = a2.shape[1]
    n = b1.shape[1]
    bm, bn = _pick(s, 1024), _pick(n, 512)
    return pl.pallas_call(
        _mm2_res_kernel,
        grid=(s // bm, n // bn),
        in_specs=[pl.BlockSpec((bm, k1), lambda i, j: (i, 0)),
                  pl.BlockSpec((bm, k2), lambda i, j: (i, 0)),
                  pl.BlockSpec((k1, bn), lambda i, j: (0, j)),
                  pl.BlockSpec((k2, bn), lambda i, j: (0, j)),
                  pl.BlockSpec((bm, bn), lambda i, j: (i, j))],
        out_specs=pl.BlockSpec((bm, bn), lambda i, j: (i, j)),
        out_shape=jax.ShapeDtypeStruct((s, n), F32),
        compiler_params=_params(2),
        name="mm2_res",
    )(a1, a2, b1, b2, res)


def _tri_pairs(n):
    qi = np.repeat(np.arange(n), np.arange(1, n + 1))
    kj = np.concatenate([np.arange(i + 1) for i in range(n)])
    return jnp.asarray(qi, jnp.int32), jnp.asarray(kj, jnp.int32)


def _bf16_round(x):
    return np.asarray(x, np.float32).astype(ml_dtypes.bfloat16).astype(np.float64)


def _alibi_consts(n_heads):
    c = np.float32(2.0 ** (-8.0 * np.arange(1, n_heads + 1) / n_heads) * LOG2E).astype(np.float64)
    hi = _bf16_round(c)
    mid = _bf16_round(c - hi)
    lo = _bf16_round(c - hi - mid)
    rows = np.zeros((n_heads, 1, LANES), np.float32)
    for col, piece in enumerate((hi, hi, mid, mid, lo, lo)):
        rows[:, 0, col] = piece
    return jnp.asarray(c, F32), jnp.asarray(rows, BF16)


def _kpos_ext(bk):
    pos = np.arange(bk)
    a = (pos // KPOS_SPLIT) * KPOS_SPLIT
    b = pos % KPOS_SPLIT
    ext = np.zeros((bk, LANES), np.float32)
    for col in range(0, 6, 2):
        ext[:, col] = a
        ext[:, col + 1] = b
    return jnp.asarray(ext, BF16)


def _one_col(rows):
    x = np.zeros((rows, LANES), np.float32)
    x[:, 0] = 1.0
    return jnp.asarray(x, BF16)


def _causal_tile(b):
    r = np.arange(b)
    return jnp.asarray(np.where(r[:, None] >= r[None, :], 0.0, NEG), F32)


def _lanes(x, width):
    if width == LANES:
        return x
    return jnp.tile(x, (1, width // LANES))


def _flash_update(s, c, m_ref, rows):
    m_prev = m_ref[rows, :]
    m_cur = jnp.max(s, axis=1, keepdims=True) + c
    m_next = jnp.maximum(m_prev, m_cur)
    m_ref[rows, :] = m_next
    p = jnp.exp2(s - _lanes(m_next - c, s.shape[1]))
    return p, jnp.exp2(m_prev - m_next)


def _diff_kernel(qi_ref, kj_ref, c_ref, q_ref, k_ref, v_ref, qext_ref, kext_ref, causal_ref,
                 lam_ref, g_ref, o_ref, qa_ref, m_ref, l_ref, acc_ref, *, bq, lam0):
    h = pl.program_id(0)
    p = pl.program_id(1)
    qi = qi_ref[p]
    kj = kj_ref[p]

    @pl.when(kj == 0)
    def _():
        m_ref[...] = jnp.full_like(m_ref, NEG)
        l_ref[...] = jnp.zeros_like(l_ref)
        acc_ref[...] = jnp.zeros_like(acc_ref)
        ext = jnp.broadcast_to(qext_ref[...], (bq, LANES))
        for mp in range(2):
            qa_ref[mp, :, :HEAD_DIM] = q_ref[mp]
            qa_ref[mp, :, HEAD_DIM:] = ext

    def step(diag):
        c = 0.0 if diag else -c_ref[h] * ((qi - kj) * bq).astype(F32)
        v = jnp.concatenate([v_ref[0], v_ref[1]], axis=1)
        kext = kext_ref[...]
        for mp in range(2):
            ka = jnp.concatenate([k_ref[mp], kext], axis=1)
            s = lax.dot_general(qa_ref[mp], ka, NT_DIMS, preferred_element_type=F32)
            if diag:
                s = s + causal_ref[...]
            rows = pl.ds(mp * bq, bq)
            pr, alpha = _flash_update(s, c, m_ref, rows)
            l_ref[rows, :] = alpha * l_ref[rows, :] + jnp.sum(pr, axis=1, keepdims=True)
            pv = jnp.dot(pr.astype(BF16), v, preferred_element_type=F32)
            acc_ref[rows, :] = _lanes(alpha, DIFF_V_DIM) * acc_ref[rows, :] + pv

    @pl.when(kj < qi)
    def _():
        step(False)

    @pl.when(kj == qi)
    def _():
        step(True)
        lam_rows = lam_ref[...]
        d1 = jnp.sum(lam_rows[0:1] * lam_rows[1:2], axis=1, keepdims=True)
        d2 = jnp.sum(lam_rows[2:3] * lam_rows[3:4], axis=1, keepdims=True)
        lam = jnp.exp(d1) - jnp.exp(d2) + lam0
        o1 = acc_ref[0:bq, :] / _lanes(l_ref[0:bq, :], DIFF_V_DIM)
        o2 = acc_ref[bq:2 * bq, :] / _lanes(l_ref[bq:2 * bq, :], DIFF_V_DIM)
        o = o1 - lam * o2
        ms = jnp.mean(o * o, axis=-1, keepdims=True)
        y = o * lax.rsqrt(ms + EPS) * g_ref[...]
        o_ref[...] = (y * (1.0 - lam0)).astype(o_ref.dtype)


def _diff_attention(proj, lam_rows, gain, layer, bq):
    s = proj.shape[1]
    qi, kj = _tri_pairs(s // bq)
    cvals, qext = _alibi_consts(DIFF_HEADS)
    lam0 = 0.8 - 0.6 * math.exp(-0.3 * layer)
    w = DIFF_V_DIM
    const2 = lambda h, p, qi, kj, c: (0, 0)
    grid_spec = pltpu.PrefetchScalarGridSpec(
        num_scalar_prefetch=3,
        grid=(DIFF_HEADS, qi.shape[0]),
        in_specs=[
            pl.BlockSpec((2, bq, LANES), lambda h, p, qi, kj, c: (OFF_DQ // w + h, qi[p], 0)),
            pl.BlockSpec((2, bq, LANES), lambda h, p, qi, kj, c: (OFF_DK // w + h, kj[p], 0)),
            pl.BlockSpec((2, bq, LANES), lambda h, p, qi, kj, c: (OFF_DV // w + h, kj[p], 0)),
            pl.BlockSpec((None, 1, LANES), lambda h, p, qi, kj, c: (h, 0, 0)),
            pl.BlockSpec((bq, LANES), const2),
            pl.BlockSpec((bq, bq), const2),
            pl.BlockSpec((8, HEAD_DIM), const2),
            pl.BlockSpec((1, w), const2),
        ],
        out_specs=pl.BlockSpec((bq, w), lambda h, p, qi, kj, c: (qi[p], h)),
        scratch_shapes=[pltpu.VMEM((2, bq, 2 * HEAD_DIM), BF16),
                        pltpu.VMEM((2 * bq, LANES), F32),
                        pltpu.VMEM((2 * bq, LANES), F32),
                        pltpu.VMEM((2 * bq, w), F32)],
    )
    return pl.pallas_call(
        functools.partial(_diff_kernel, bq=bq, lam0=lam0),
        grid_spec=grid_spec,
        out_shape=jax.ShapeDtypeStruct((s, DIFF_WIDTH), BF16),
        compiler_params=_params(2),
        name="diff_attention",
    )(qi, kj, cvals, proj, proj, proj, qext, _kpos_ext(bq), _causal_tile(bq), lam_rows, gain)


def _compress_kernel(a_ref, pos_ref, w1_ref, w2_ref, o_ref):
    half = CMP_STRIDE * HEAD_DIM
    a = a_ref[...].astype(F32)
    pos = pos_ref[...]
    top = (a + pos[:, :half]).astype(BF16)
    bot = (a + pos[:, half:]).astype(BF16)
    t = jnp.dot(top, w1_ref[:half, :], preferred_element_type=F32)
    b = jnp.dot(bot, w1_ref[half:, :], preferred_element_type=F32)
    rows = a.shape[0]
    hid = t + pltpu.roll(b, rows - 1, 0)
    act = jax.nn.gelu(hid)
    o_ref[...] = jnp.dot(act.astype(BF16), w2_ref[...], preferred_element_type=F32).astype(o_ref.dtype)


def _compress(a, pos, w1, w2):
    _, hkv, rows, width = a.shape
    return pl.pallas_call(
        _compress_kernel,
        grid=(2, hkv),
        in_specs=[pl.BlockSpec((None, None, rows, width), lambda t, h: (t, h, 0, 0)),
                  pl.BlockSpec((None, 1, 2 * width), lambda t, h: (t, 0, 0)),
                  pl.BlockSpec((None, 2 * width, CMP_HIDDEN), lambda t, h: (t, 0, 0)),
                  pl.BlockSpec((None, CMP_HIDDEN, HEAD_DIM), lambda t, h: (t, 0, 0))],
        out_specs=pl.BlockSpec((None, None, rows, HEAD_DIM), lambda t, h: (t, h, 0, 0)),
        out_shape=jax.ShapeDtypeStruct((2, hkv, rows, HEAD_DIM), BF16),
        compiler_params=_params(2),
        name="nsa_compress",
    )(a, pos, w1, w2)


def _split3(x):
    hi = x.astype(BF16)
    r1 = x - hi.astype(F32)
    mid = r1.astype(BF16)
    lo = (r1 - mid.astype(F32)).astype(BF16)
    return hi, mid, lo


def _cmp_select_kernel(c_ref, q_ref, kc_ref, vc_ref, ov_ref, oc_ref, sel_ref, any_ref,
                       *, bq, n_cmp, topn):
    hkv = pl.program_id(0)
    i = pl.program_id(1)
    ncp = kc_ref.shape[0]
    q = q_ref[...].reshape(NSA_GROUP * bq, HEAD_DIM)
    s = lax.dot_general(q, kc_ref[...], NT_DIMS, preferred_element_type=F32)
    t = i * bq + lax.broadcasted_iota(jnp.int32, (bq, ncp), 0)
    cidx = lax.broadcasted_iota(jnp.int32, (bq, ncp), 1)
    dist = t - (cidx * CMP_STRIDE + CMP_BLOCK - 1)
    ok = (dist >= 0) & (cidx < n_cmp)
    distf = dist.astype(F32)
    vc = vc_ref[...]
    psum = jnp.zeros((bq, ncp), F32)
    for g in range(NSA_GROUP):
        sg = jnp.where(ok, s[g * bq:(g + 1) * bq] - c_ref[hkv * NSA_GROUP + g] * distf, NEG)
        mx = jnp.max(sg, axis=1, keepdims=True)
        e = jnp.where(ok, jnp.exp2(sg - mx), 0.0)
        den = jnp.sum(e, axis=1, keepdims=True)
        pg = e / jnp.where(den > 0.0, den, 1.0)
        oc_ref[:, g * HEAD_DIM:(g + 1) * HEAD_DIM] = jnp.dot(
            pg.astype(BF16), vc, preferred_element_type=F32)
        psum = psum + pg
    ov = ov_ref[...]
    imp = jnp.zeros((bq, LANES), F32)
    for piece in _split3(psum):
        imp = imp + jnp.dot(piece, ov, preferred_element_type=F32)
    imp_t = imp.T
    tq = i * bq + lax.broadcasted_iota(jnp.int32, (LANES, bq), 1)
    blk = lax.broadcasted_iota(jnp.int32, (LANES, bq), 0)
    cur = lax.shift_right_arithmetic(tq, SEL_BLOCK.bit_length() - 1)
    forced = (blk == 0) | (blk == cur) | (blk == cur - 1)
    score = jnp.where(forced, FORCE_SCORE, jnp.where(blk <= cur, imp_t, -1.0))
    removed = -3.0e38
    blkf = blk.astype(F32)

    def pick(_, carry):
        sc, sel = carry
        mx = jnp.max(sc, axis=0, keepdims=True)
        first = jnp.min(jnp.where(sc == mx, blkf, float(LANES)), axis=0, keepdims=True)
        hit = blkf == first
        return jnp.where(hit, removed, sc), jnp.where(hit, 1.0, sel)

    _, sel_t = lax.fori_loop(0, topn, pick, (score, jnp.zeros((LANES, bq), F32)))
    sel = sel_t.T
    sel_ref[...] = sel.astype(sel_ref.dtype)
    any_ref[...] = jnp.max(sel, axis=0, keepdims=True)


def _cmp_select(proj, kvc, overlap, cvals, bq):
    s = proj.shape[1]
    n_cmp = (s - CMP_BLOCK) // CMP_STRIDE + 1
    n_sel = s // SEL_BLOCK
    topn = min(SEL_TOPN, n_sel)
    ncp = kvc.shape[2]
    gw = NSA_GROUP * HEAD_DIM
    grid_spec = pltpu.PrefetchScalarGridSpec(
        num_scalar_prefetch=1,
        grid=(NSA_KV_HEADS, s // bq),
        in_specs=[
            pl.BlockSpec((NSA_GROUP, bq, LANES), lambda h, i, c: (OFF_NQ // gw + h, i, 0)),
            pl.BlockSpec((None, None, ncp, HEAD_DIM), lambda h, i, c: (0, h, 0, 0)),
            pl.BlockSpec((None, None, ncp, HEAD_DIM), lambda h, i, c: (1, h, 0, 0)),
            pl.BlockSpec((ncp, LANES), lambda h, i, c: (0, 0)),
        ],
        out_specs=[pl.BlockSpec((bq, gw), lambda h, i, c: (i, h)),
                   pl.BlockSpec((None, bq, LANES), lambda h, i, c: (h, i, 0)),
                   pl.BlockSpec((None, None, 1, LANES), lambda h, i, c: (h, i, 0, 0))],
    )
    return pl.pallas_call(
        functools.partial(_cmp_select_kernel, bq=bq, n_cmp=n_cmp, topn=topn),
        grid_spec=grid_spec,
        out_shape=[jax.ShapeDtypeStruct((s, NSA_WIDTH), F32),
                   jax.ShapeDtypeStruct((NSA_KV_HEADS, s, LANES), BF16),
                   jax.ShapeDtypeStruct((NSA_KV_HEADS, s // bq, 1, LANES), F32)],
        compiler_params=_params(2),
        name="nsa_cmp_select",
    )(cvals, proj, kvc, kvc, overlap)


def _sel_kernel(qi_ref, kj_ref, nact_ref, c_ref, q_ref, k_ref, v_ref, sel_ref, exp_ref, qext_ref,
                kext_ref, one_ref, causal_ref, o_ref, qa_ref, m_ref, acc_ref, *, bq, n_pairs):
    hkv = pl.program_id(0)
    p = pl.program_id(1)
    qi = qi_ref[hkv * n_pairs + p]
    kj = kj_ref[hkv * n_pairs + p]
    live = p < nact_ref[hkv]

    @pl.when(live & (kj == 0))
    def _():
        m_ref[...] = jnp.full_like(m_ref, NEG)
        acc_ref[...] = jnp.zeros_like(acc_ref)
        for g in range(NSA_GROUP):
            qa_ref[g * bq:(g + 1) * bq, :HEAD_DIM] = q_ref[g]
            qa_ref[g * bq:(g + 1) * bq, HEAD_DIM:] = jnp.broadcast_to(qext_ref[g], (bq, LANES))

    def step(diag):
        one = one_ref[...]
        maskb = jnp.dot(jnp.concatenate([sel_ref[...], one], axis=1), exp_ref[...],
                        preferred_element_type=F32)
        if diag:
            maskb = maskb + causal_ref[...]
        ka = jnp.concatenate([k_ref[...], kext_ref[...]], axis=1)
        va = jnp.concatenate([v_ref[...], one], axis=1)
        s_all = lax.dot_general(qa_ref[...], ka, NT_DIMS, preferred_element_type=F32)
        dtile = ((qi - kj) * bq).astype(F32)
        for g in range(NSA_GROUP):
            c = 0.0 if diag else -c_ref[hkv * NSA_GROUP + g] * dtile
            rows = pl.ds(g * bq, bq)
            pr, alpha = _flash_update(s_all[g * bq:(g + 1) * bq] + maskb, c, m_ref, rows)
            pv = jnp.dot(pr.astype(BF16), va, preferred_element_type=F32)
            acc_ref[rows, :] = _lanes(alpha, 2 * HEAD_DIM) * acc_ref[rows, :] + pv

    @pl.when(live & (kj < qi))
    def _():
        step(False)

    @pl.when(live & (kj == qi))
    def _():
        step(True)
        for g in range(NSA_GROUP):
            rows = pl.ds(g * bq, bq)
            o_ref[:, g * HEAD_DIM:(g + 1) * HEAD_DIM] = (
                acc_ref[rows, :HEAD_DIM] / acc_ref[rows, HEAD_DIM:HEAD_DIM + 1])


def _expand_aug(s, bk):
    key_blk = np.arange(s) // SEL_BLOCK
    e = np.zeros((2 * LANES, s), np.float32)
    e[key_blk, np.arange(s)] = MASK_BIG
    e[LANES, :] = -MASK_BIG
    return jnp.asarray(e.reshape(2 * LANES, s // bk, bk).transpose(1, 0, 2), BF16)


def _sel_schedule(any_rows, s, bq):
    nq = s // bq
    per_tile = bq // SEL_BLOCK
    a = any_rows.reshape(NSA_KV_HEADS, nq, -1, LANES).max(axis=2)
    a = a[:, :, :nq * per_tile].reshape(NSA_KV_HEADS, nq, nq, per_tile).max(axis=-1) > 0.0
    tq, tk = _tri_pairs(nq)
    act = a[:, tq, tk] | (tk == 0)[None, :] | (tk == tq)[None, :]
    order = jnp.argsort(jnp.logical_not(act), axis=1, stable=True)
    nact = jnp.sum(act, axis=1).astype(jnp.int32)
    pos = jnp.minimum(jnp.arange(tq.shape[0], dtype=jnp.int32)[None, :], nact[:, None] - 1)
    idx = jnp.take_along_axis(order, pos, axis=1)
    return tq[idx].reshape(-1), tk[idx].reshape(-1), nact


def _sel_attention(proj, sel, any_rows, cvals, qext, bq):
    s = proj.shape[1]
    qi, kj, nact = _sel_schedule(any_rows, s, bq)
    n_pairs = qi.shape[0] // NSA_KV_HEADS
    gw = NSA_GROUP * HEAD_DIM
    d = HEAD_DIM
    const2 = lambda h, p, qi, kj, na, c: (0, 0)
    grid_spec = pltpu.PrefetchScalarGridSpec(
        num_scalar_prefetch=4,
        grid=(NSA_KV_HEADS, n_pairs),
        in_specs=[
            pl.BlockSpec((NSA_GROUP, bq, d),
                         lambda h, p, qi, kj, na, c: (OFF_NQ // gw + h, qi[h * n_pairs + p], 0)),
            pl.BlockSpec((None, bq, d),
                         lambda h, p, qi, kj, na, c: (OFF_KS // d + h, kj[h * n_pairs + p], 0)),
            pl.BlockSpec((None, bq, d),
                         lambda h, p, qi, kj, na, c: (OFF_VS // d + h, kj[h * n_pairs + p], 0)),
            pl.BlockSpec((None, bq, LANES), lambda h, p, qi, kj, na, c: (h, qi[h * n_pairs + p], 0)),
            pl.BlockSpec((None, 2 * LANES, bq), lambda h, p, qi, kj, na, c: (kj[h * n_pairs + p], 0, 0)),
            pl.BlockSpec((None, NSA_GROUP, 1, LANES), lambda h, p, qi, kj, na, c: (h, 0, 0, 0)),
            pl.BlockSpec((bq, LANES), const2),
            pl.BlockSpec((bq, LANES), const2),
            pl.BlockSpec((bq, bq), const2),
        ],
        out_specs=pl.BlockSpec((bq, gw), lambda h, p, qi, kj, na, c: (qi[h * n_pairs + p], h)),
        scratch_shapes=[pltpu.VMEM((NSA_GROUP * bq, 2 * d), BF16),
                        pltpu.VMEM((NSA_GROUP * bq, LANES), F32),
                        pltpu.VMEM((NSA_GROUP * bq, 2 * d), F32)],
    )
    return pl.pallas_call(
        functools.partial(_sel_kernel, bq=bq, n_pairs=n_pairs),
        grid_spec=grid_spec,
        out_shape=jax.ShapeDtypeStruct((s, NSA_WIDTH), F32),
        compiler_params=_params(2),
        name="nsa_selected",
    )(qi, kj, nact, cvals, proj, proj, proj, sel, _expand_aug(s, bq),
      qext.reshape(NSA_KV_HEADS, NSA_GROUP, 1, LANES), _kpos_ext(bq), _one_col(bq), _causal_tile(bq))


def _win_kernel(c_ref, q_ref, kp_ref, kc_ref, vp_ref, vc_ref, one_ref, oc_ref, os_ref, gate_ref, o_ref,
                bias_ref):
    hkv = pl.program_id(0)
    i = pl.program_id(1)
    bq = WINDOW

    @pl.when(i == 0)
    def _():
        rel = (lax.broadcasted_iota(jnp.int32, (bq, bq), 0)
               - lax.broadcasted_iota(jnp.int32, (bq, bq), 1))
        relf = rel.astype(F32)
        for g in range(NSA_GROUP):
            c = c_ref[hkv * NSA_GROUP + g]
            bias_ref[g, 0] = jnp.where(rel < 0, -c * (relf + WINDOW), NEG)
            bias_ref[g, 1] = jnp.where(rel >= 0, -c * relf, NEG)

    q = q_ref[...].reshape(NSA_GROUP * bq, HEAD_DIM)
    sp_all = lax.dot_general(q, kp_ref[...], NT_DIMS, preferred_element_type=F32)
    sc_all = lax.dot_general(q, kc_ref[...], NT_DIMS, preferred_element_type=F32)
    first = jnp.where(i == 0, NEG, 0.0)
    one = one_ref[...]
    vp = jnp.concatenate([vp_ref[...], one], axis=1)
    vc = jnp.concatenate([vc_ref[...], one], axis=1)
    gates = gate_ref[...]
    for g in range(NSA_GROUP):
        rows = slice(g * bq, (g + 1) * bq)
        cols = slice(g * HEAD_DIM, (g + 1) * HEAD_DIM)
        sp = sp_all[rows] + bias_ref[g, 0] + first
        sc = sc_all[rows] + bias_ref[g, 1]
        mx = jnp.maximum(jnp.max(sp, axis=1, keepdims=True), jnp.max(sc, axis=1, keepdims=True))
        pp = jnp.exp2(sp - mx).astype(BF16)
        pc = jnp.exp2(sc - mx).astype(BF16)
        pv = (jnp.dot(pp, vp, preferred_element_type=F32)
              + jnp.dot(pc, vc, preferred_element_type=F32))
        ow = pv[:, :HEAD_DIM] / pv[:, HEAD_DIM:HEAD_DIM + 1]
        gc = gates[:, 3 * g + 0:3 * g + 1]
        gs = gates[:, 3 * g + 1:3 * g + 2]
        gw = gates[:, 3 * g + 2:3 * g + 3]
        o_ref[:, cols] = (gc * oc_ref[:, cols] + gs * os_ref[:, cols] + gw * ow).astype(o_ref.dtype)


def _win_combine(proj, o_cmp, o_sel, gates, cvals):
    s = proj.shape[1]
    bq = WINDOW
    gw = NSA_GROUP * HEAD_DIM
    d = HEAD_DIM
    prev = lambda i: jnp.maximum(i - 1, 0)
    grid_spec = pltpu.PrefetchScalarGridSpec(
        num_scalar_prefetch=1,
        grid=(NSA_KV_HEADS, s // bq),
        in_specs=[
            pl.BlockSpec((NSA_GROUP, bq, d), lambda h, i, c: (OFF_NQ // gw + h, i, 0)),
            pl.BlockSpec((None, bq, d), lambda h, i, c: (OFF_KW // d + h, prev(i), 0)),
            pl.BlockSpec((None, bq, d), lambda h, i, c: (OFF_KW // d + h, i, 0)),
            pl.BlockSpec((None, bq, d), lambda h, i, c: (OFF_VW // d + h, prev(i), 0)),
            pl.BlockSpec((None, bq, d), lambda h, i, c: (OFF_VW // d + h, i, 0)),
            pl.BlockSpec((bq, LANES), lambda h, i, c: (0, 0)),
            pl.BlockSpec((bq, gw), lambda h, i, c: (i, h)),
            pl.BlockSpec((bq, gw), lambda h, i, c: (i, h)),
            pl.BlockSpec((bq, LANES), lambda h, i, c: (i, h)),
        ],
        out_specs=pl.BlockSpec((bq, gw), lambda h, i, c: (i, h)),
        scratch_shapes=[pltpu.VMEM((NSA_GROUP, 2, bq, bq), F32)],
    )
    return pl.pallas_call(
        _win_kernel,
        grid_spec=grid_spec,
        out_shape=jax.ShapeDtypeStruct((s, NSA_WIDTH), BF16),
        compiler_params=_params(2),
        name="nsa_window_combine",
    )(cvals, proj, proj, proj, proj, proj, _one_col(bq), o_cmp, o_sel, gates)


def _ffn(h, norm_g, w_gate, w_up, w_down):
    u = _rmsnorm(h, norm_g, BF16)
    act = _ffn_up(u, w_gate, w_up)
    return _mm_res(act, w_down.astype(BF16), h, 0.5)


def _overlap_matrix(s, rows):
    n_cmp = (s - CMP_BLOCK) // CMP_STRIDE + 1
    n_sel = s // SEL_BLOCK
    cs = np.arange(rows)[:, None] * CMP_STRIDE
    ss = np.arange(LANES)[None, :] * SEL_BLOCK
    ov = (cs <= ss + SEL_BLOCK - 1) & (cs + CMP_BLOCK - 1 >= ss)
    ov &= (np.arange(rows)[:, None] < n_cmp) & (np.arange(LANES)[None, :] < n_sel)
    return jnp.asarray(ov, BF16)


def _query_colscale():
    cs = np.ones((1, OFF_G), np.float32)
    cs[:, OFF_DQ:OFF_DK] = QSCALE
    cs[:, OFF_NQ:OFF_KC] = QSCALE
    return jnp.asarray(cs)


def _layer(h, layer, ffn1_norm, ffn1_w_gate, ffn1_w_up, ffn1_w_down, mix_norm, w_in, gate_bias,
           lambda_q1, lambda_k1, lambda_q2, lambda_k2, diff_norm,
           cmp_pos_k, cmp_w1_k, cmp_w2_k, cmp_pos_v, cmp_w1_v, cmp_w2_v,
           w_out, ffn2_norm, ffn2_w_gate, ffn2_w_up, ffn2_w_down):
    s = h.shape[0]
    assert s % WINDOW == 0 and s // SEL_BLOCK <= LANES and OFF_G % 512 == 0
    h = _ffn(h, ffn1_norm, ffn1_w_gate, ffn1_w_up, ffn1_w_down)

    u = _rmsnorm(h, mix_norm, BF16)
    w_in_t = w_in.T
    proj = _proj_slabs(u, w_in_t, _query_colscale())
    wg = w_in_t[OFF_G:].T.reshape(-1, NSA_KV_HEADS, GATES_PER_KV)
    wg = jnp.pad(wg, ((0, 0), (0, 0), (0, LANES - GATES_PER_KV))).reshape(-1, NSA_KV_HEADS * LANES)
    gb = jnp.pad(gate_bias.astype(F32).reshape(NSA_KV_HEADS, GATES_PER_KV),
                 ((0, 0), (0, LANES - GATES_PER_KV))).reshape(1, NSA_KV_HEADS * LANES)
    gates = _gate_proj(u, wg.astype(BF16), gb)

    lam_rows = jnp.pad(jnp.stack([lambda_q1, lambda_k1, lambda_q2, lambda_k2]).astype(F32),
                       ((0, 4), (0, 0)))
    o_diff = _diff_attention(proj, lam_rows, diff_norm.reshape(1, -1).astype(F32), layer,
                             _pick(s, 1024))

    nsa_c, nsa_qext = _alibi_consts(NSA_HEADS)
    rows = s // CMP_STRIDE
    kv_cmp = proj[OFF_KC // LANES:OFF_KS // LANES].reshape(
        2, NSA_KV_HEADS, rows, CMP_STRIDE * HEAD_DIM)
    pos = jnp.stack([cmp_pos_k, cmp_pos_v]).astype(F32).reshape(2, 1, CMP_BLOCK * HEAD_DIM)
    w1 = jnp.stack([cmp_w1_k, cmp_w1_v]).astype(BF16)
    w2 = jnp.stack([cmp_w2_k, cmp_w2_v]).astype(BF16)
    kvc = _compress(kv_cmp, pos, w1, w2)
    o_cmp, sel, any_rows = _cmp_select(proj, kvc, _overlap_matrix(s, rows), nsa_c, _pick(s, 256))
    o_sel = _sel_attention(proj, sel, any_rows, nsa_c, nsa_qext, _pick(s, 512))
    o_nsa = _win_combine(proj, o_cmp, o_sel, gates, nsa_c)

    wo = w_out.astype(BF16)
    h = _mm2_res(o_diff, o_nsa, wo[:DIFF_WIDTH], wo[DIFF_WIDTH:], h)
    return _ffn(h, ffn2_norm, ffn2_w_gate, ffn2_w_up, ffn2_w_down)


def kernel(x, ffn1_norm, ffn1_w_gate, ffn1_w_up, ffn1_w_down, mix_norm, w_in, gate_bias, lambda_q1, lambda_k1, lambda_q2, lambda_k2, diff_norm, cmp_pos_k, cmp_w1_k, cmp_w2_k, cmp_pos_v, cmp_w1_v, cmp_w2_v, w_out, ffn2_norm, ffn2_w_gate, ffn2_w_up, ffn2_w_down, final_norm):
    b, s, d = x.shape
    per_layer = (ffn1_norm, ffn1_w_gate, ffn1_w_up, ffn1_w_down, mix_norm, w_in, gate_bias,
                 lambda_q1, lambda_k1, lambda_q2, lambda_k2, diff_norm,
                 cmp_pos_k, cmp_w1_k, cmp_w2_k, cmp_pos_v, cmp_w1_v, cmp_w2_v,
                 w_out, ffn2_norm, ffn2_w_gate, ffn2_w_up, ffn2_w_down)
    outs = []
    for bi in range(b):
        h = x.reshape(s, d) if b == 1 else x[bi]
        for layer in range(DEPTH):
            h = _layer(h, layer, *[p[layer] for p in per_layer])
        outs.append(_rmsnorm(h, final_norm, x.dtype))
    return outs[0].reshape(1, s, d) if b == 1 else jnp.stack(outs)
```

```python
import functools
import math

import ml_dtypes
import numpy as np
import jax
import jax.numpy as jnp
from jax import lax
from jax.experimental import pallas as pl
from jax.experimental.pallas import tpu as pltpu

D_MODEL = 4096
DEPTH = 1
HEAD_DIM = 128
DIFF_V_DIM = 2 * HEAD_DIM
DIFF_HEADS = (D_MODEL // 2) // DIFF_V_DIM
DIFF_WIDTH = DIFF_HEADS * DIFF_V_DIM
NSA_HEADS = (D_MODEL - DIFF_WIDTH) // HEAD_DIM
NSA_KV_HEADS = 4
NSA_GROUP = NSA_HEADS // NSA_KV_HEADS
NSA_WIDTH = NSA_HEADS * HEAD_DIM
CMP_BLOCK = 32
CMP_STRIDE = 16
CMP_HIDDEN = 256
SEL_BLOCK = 64
SEL_TOPN = 16
WINDOW = 512
EPS = 1e-6
NEG = -1e30
FORCE_SCORE = 1e4

LANES = 128
VMEM_LIMIT = 56 * 1024 * 1024
LOG2E = 1.4426950408889634
QSCALE = HEAD_DIM ** -0.5 * LOG2E
MASK_BIG = 2.0 ** 100
KPOS_SPLIT = 32
OFF_DQ = 0
OFF_DK = OFF_DQ + DIFF_HEADS * 2 * HEAD_DIM
OFF_DV = OFF_DK + DIFF_HEADS * 2 * HEAD_DIM
OFF_NQ = OFF_DV + DIFF_HEADS * DIFF_V_DIM
OFF_KC = OFF_NQ + NSA_HEADS * HEAD_DIM
OFF_VC = OFF_KC + NSA_KV_HEADS * HEAD_DIM
OFF_KS = OFF_VC + NSA_KV_HEADS * HEAD_DIM
OFF_VS = OFF_KS + NSA_KV_HEADS * HEAD_DIM
OFF_KW = OFF_VS + NSA_KV_HEADS * HEAD_DIM
OFF_VW = OFF_KW + NSA_KV_HEADS * HEAD_DIM
OFF_G = OFF_VW + NSA_KV_HEADS * HEAD_DIM
GATES_PER_KV = 3 * NSA_GROUP

F32 = jnp.float32
BF16 = jnp.bfloat16
NT_DIMS = (((1,), (1,)), ((), ()))


def _params(n_axes):
    return pltpu.CompilerParams(dimension_semantics=("arbitrary",) * n_axes,
                                vmem_limit_bytes=VMEM_LIMIT)


def _pick(n, pref):
    b = min(pref, n)
    while n % b:
        b //= 2
    return b


def _rmsnorm_kernel(x_ref, g_ref, o_ref):
    x = x_ref[...]
    ms = jnp.mean(x * x, axis=-1, keepdims=True)
    o_ref[...] = (x * lax.rsqrt(ms + EPS) * g_ref[...]).astype(o_ref.dtype)


def _rmsnorm(x, g, out_dtype):
    s, d = x.shape
    bm = _pick(s, 256)
    return pl.pallas_call(
        _rmsnorm_kernel,
        grid=(s // bm,),
        in_specs=[pl.BlockSpec((bm, d), lambda i: (i, 0)),
                  pl.BlockSpec((1, d), lambda i: (0, 0))],
        out_specs=pl.BlockSpec((bm, d), lambda i: (i, 0)),
        out_shape=jax.ShapeDtypeStruct((s, d), out_dtype),
        compiler_params=_params(1),
        name="rmsnorm",
    )(x, g.reshape(1, d).astype(F32))


def _ffn_up_kernel(u_ref, wg_ref, wu_ref, o_ref):
    u = u_ref[...]
    g = jnp.dot(u, wg_ref[...].astype(BF16), preferred_element_type=F32)
    up = jnp.dot(u, wu_ref[...].astype(BF16), preferred_element_type=F32)
    o_ref[...] = (g * jax.nn.sigmoid(g) * up).astype(o_ref.dtype)


def _ffn_up(u, wg, wu):
    s, d = u.shape
    f = wg.shape[1]
    bm, bn = _pick(s, 2048), _pick(f, 256)
    return pl.pallas_call(
        _ffn_up_kernel,
        grid=(s // bm, f // bn),
        in_specs=[pl.BlockSpec((bm, d), lambda i, j: (i, 0), pipeline_mode=pl.Buffered(1)),
                  pl.BlockSpec((d, bn), lambda i, j: (0, j)),
                  pl.BlockSpec((d, bn), lambda i, j: (0, j))],
        out_specs=pl.BlockSpec((bm, bn), lambda i, j: (i, j)),
        out_shape=jax.ShapeDtypeStruct((s, f), BF16),
        compiler_params=_params(2),
        name="ffn_up",
    )(u, wg, wu)


def _mm_res_kernel(a_ref, b_ref, r_ref, o_ref, *, alpha):
    o_ref[...] = r_ref[...] + alpha * jnp.dot(a_ref[...], b_ref[...], preferred_element_type=F32)


def _mm_res(a, b, res, alpha):
    s, kdim = a.shape
    n = b.shape[1]
    bm, bn = _pick(s, 1024), _pick(n, 256)
    return pl.pallas_call(
        functools.partial(_mm_res_kernel, alpha=alpha),
        grid=(s // bm, n // bn),
        in_specs=[pl.BlockSpec((bm, kdim), lambda i, j: (i, 0), pipeline_mode=pl.Buffered(1)),
                  pl.BlockSpec((kdim, bn), lambda i, j: (0, j)),
                  pl.BlockSpec((bm, bn), lambda i, j: (i, j))],
        out_specs=pl.BlockSpec((bm, bn), lambda i, j: (i, j)),
        out_shape=jax.ShapeDtypeStruct((s, n), F32),
        compiler_params=_params(2),
        name="mm_res",
    )(a, b, res)


def _proj_kernel(a_ref, bt_ref, cs_ref, o_ref):
    acc = lax.dot_general(a_ref[...], bt_ref[...].astype(BF16), NT_DIMS,
                          preferred_element_type=F32) * cs_ref[...]
    for t in range(o_ref.shape[0]):
        o_ref[t] = acc[:, t * LANES:(t + 1) * LANES].astype(o_ref.dtype)


def _proj_slabs(a, bt, colscale):
    s, kdim = a.shape
    n = colscale.shape[1]
    bm, bn = _pick(s, 2048), _pick(n, 512)
    return pl.pallas_call(
        _proj_kernel,
        grid=(s // bm, n // bn),
        in_specs=[pl.BlockSpec((bm, kdim), lambda i, j: (i, 0), pipeline_mode=pl.Buffered(1)),
                  pl.BlockSpec((bn, kdim), lambda i, j: (j, 0)),
                  pl.BlockSpec((1, bn), lambda i, j: (0, j))],
        out_specs=pl.BlockSpec((bn // LANES, bm, LANES), lambda i, j: (j, i, 0)),
        out_shape=jax.ShapeDtypeStruct((n // LANES, s, LANES), BF16),
        compiler_params=_params(2),
        name="proj_slabs",
    )(a, bt, colscale)


def _gate_kernel(a_ref, b_ref, bias_ref, o_ref):
    z = jnp.dot(a_ref[...], b_ref[...], preferred_element_type=F32) + bias_ref[...]
    o_ref[...] = jax.nn.sigmoid(z)


def _gate_proj(u, wg, bias):
    s, kdim = u.shape
    n = wg.shape[1]
    bm = _pick(s, 1024)
    return pl.pallas_call(
        _gate_kernel,
        grid=(s // bm,),
        in_specs=[pl.BlockSpec((bm, kdim), lambda i: (i, 0)),
                  pl.BlockSpec((kdim, n), lambda i: (0, 0)),
                  pl.BlockSpec((1, n), lambda i: (0, 0))],
        out_specs=pl.BlockSpec((bm, n), lambda i: (i, 0)),
        out_shape=jax.ShapeDtypeStruct((s, n), F32),
        compiler_params=_params(1),
        name="gate_proj",
    )(u, wg, bias)


def _mm2_res_kernel(a1_ref, a2_ref, b1_ref, b2_ref, r_ref, o_ref):
    acc = jnp.dot(a1_ref[...], b1_ref[...], preferred_element_type=F32)
    acc += jnp.dot(a2_ref[...], b2_ref[...], preferred_element_type=F32)
    o_ref[...] = r_ref[...] + acc


def _mm2_res(a1, a2, b1, b2, res):
    s, k1 = a1.shape
    k2 = a2.shape[1]
    n = b1.shape[1]
    bm, bn = _pick(s, 1024), _pick(n, 512)
    return pl.pallas_call(
        _mm2_res_kernel,
        grid=(s // bm, n // bn),
        in_specs=[pl.BlockSpec((bm, k1), lambda i, j: (i, 0)),
                  pl.BlockSpec((bm, k2), lambda i, j: (i, 0)),
                  pl.BlockSpec((k1, bn), lambda i, j: (0, j)),
                  pl.BlockSpec((k2, bn), lambda i, j: (0, j)),
                  pl.BlockSpec((bm, bn), lambda i, j: (i, j))],
        out_specs=pl.BlockSpec((bm, bn), lambda i, j: (i, j)),
        out_shape=jax.ShapeDtypeStruct((s, n), F32),
        compiler_params=_params(2),
        name="mm2_res",
    )(a1, a2, b1, b2, res)


def _tri_pairs(n):
    qi = np.repeat(np.arange(n), np.arange(1, n + 1))
    kj = np.concatenate([np.arange(i + 1) for i in range(n)])
    return jnp.asarray(qi, jnp.int32), jnp.asarray(kj, jnp.int32)


def _bf16_round(x):
    return np.asarray(x, np.float32).astype(ml_dtypes.bfloat16).astype(np.float64)


def _alibi_consts(n_heads):
    c = np.float32(2.0 ** (-8.0 * np.arange(1, n_heads + 1) / n_heads) * LOG2E).astype(np.float64)
    hi = _bf16_round(c)
    mid = _bf16_round(c - hi)
    lo = _bf16_round(c - hi - mid)
    rows = np.zeros((n_heads, 1, LANES), np.float32)
    for col, piece in enumerate((hi, hi, mid, mid, lo, lo)):
        rows[:, 0, col] = piece
    return jnp.asarray(c, F32), jnp.asarray(rows, BF16)


def _kpos_ext(bk):
    pos = np.arange(bk)
    a = (pos // KPOS_SPLIT) * KPOS_SPLIT
    b = pos % KPOS_SPLIT
    ext = np.zeros((bk, LANES), np.float32)
    for col in range(0, 6, 2):
        ext[:, col] = a
        ext[:, col + 1] = b
    return jnp.asarray(ext, BF16)


def _ones_block(rows):
    return jnp.ones((rows, LANES), BF16)


def _causal_tile(b):
    r = np.arange(b)
    return jnp.asarray(np.where(r[:, None] >= r[None, :], 0.0, NEG), F32)


def _lanes(x, width):
    if width == LANES:
        return x
    return jnp.tile(x, (1, width // LANES))


def _flash_update(s, c, m_ref, rows):
    m_prev = m_ref[rows, :]
    m_cur = jnp.max(s, axis=1, keepdims=True) + c
    m_next = jnp.maximum(m_prev, m_cur)
    m_ref[rows, :] = m_next
    p = jnp.exp2(s - _lanes(m_next - c, s.shape[1]))
    return p, jnp.exp2(m_prev - m_next)


def _diff_kernel(qi_ref, kj_ref, c_ref, q_ref, k_ref, v_ref, qext_ref, kext_ref, one_ref, causal_ref,
                 lam_ref, g_ref, o_ref, qa_ref, m_ref, acc_ref, *, bq, lam0):
    h = pl.program_id(0)
    p = pl.program_id(1)
    qi = qi_ref[p]
    kj = kj_ref[p]
    acc_w = DIFF_V_DIM + LANES

    @pl.when(kj == 0)
    def _():
        m_ref[...] = jnp.full_like(m_ref, NEG)
        acc_ref[...] = jnp.zeros_like(acc_ref)
        ext = jnp.broadcast_to(qext_ref[...], (bq, LANES))
        for mp in range(2):
            qa_ref[mp, :, :HEAD_DIM] = q_ref[mp]
            qa_ref[mp, :, HEAD_DIM:] = ext

    def step(diag):
        c = 0.0 if diag else -c_ref[h] * ((qi - kj) * bq).astype(F32)
        va = jnp.concatenate([v_ref[0], v_ref[1], one_ref[...]], axis=1)
        kext = kext_ref[...]
        for mp in range(2):
            ka = jnp.concatenate([k_ref[mp], kext], axis=1)
            s = lax.dot_general(qa_ref[mp], ka, NT_DIMS, preferred_element_type=F32)
            if diag:
                s = s + causal_ref[...]
            rows = pl.ds(mp * bq, bq)
            pr, alpha = _flash_update(s, c, m_ref, rows)
            pv = jnp.dot(pr.astype(BF16), va, preferred_element_type=F32)
            acc_ref[rows, :] = _lanes(alpha, acc_w) * acc_ref[rows, :] + pv

    @pl.when(kj < qi)
    def _():
        step(False)

    @pl.when(kj == qi)
    def _():
        step(True)
        lam_rows = lam_ref[...]
        d1 = jnp.sum(lam_rows[0:1] * lam_rows[1:2], axis=1, keepdims=True)
        d2 = jnp.sum(lam_rows[2:3] * lam_rows[3:4], axis=1, keepdims=True)
        lam = jnp.exp(d1) - jnp.exp(d2) + lam0
        o1 = acc_ref[0:bq, :DIFF_V_DIM] / _lanes(acc_ref[0:bq, DIFF_V_DIM:], DIFF_V_DIM)
        o2 = acc_ref[bq:2 * bq, :DIFF_V_DIM] / _lanes(acc_ref[bq:2 * bq, DIFF_V_DIM:], DIFF_V_DIM)
        o = o1 - lam * o2
        ms = jnp.mean(o * o, axis=-1, keepdims=True)
        y = o * lax.rsqrt(ms + EPS) * g_ref[...]
        o_ref[...] = (y * (1.0 - lam0)).astype(o_ref.dtype)


def _diff_attention(proj, lam_rows, gain, layer, bq):
    s = proj.shape[1]
    qi, kj = _tri_pairs(s // bq)
    cvals, qext = _alibi_consts(DIFF_HEADS)
    lam0 = 0.8 - 0.6 * math.exp(-0.3 * layer)
    w = DIFF_V_DIM
    const2 = lambda h, p, qi, kj, c: (0, 0)
    grid_spec = pltpu.PrefetchScalarGridSpec(
        num_scalar_prefetch=3,
        grid=(DIFF_HEADS, qi.shape[0]),
        in_specs=[
            pl.BlockSpec((2, bq, LANES), lambda h, p, qi, kj, c: (OFF_DQ // w + h, qi[p], 0)),
            pl.BlockSpec((2, bq, LANES), lambda h, p, qi, kj, c: (OFF_DK // w + h, kj[p], 0)),
            pl.BlockSpec((2, bq, LANES), lambda h, p, qi, kj, c: (OFF_DV // w + h, kj[p], 0)),
            pl.BlockSpec((None, 1, LANES), lambda h, p, qi, kj, c: (h, 0, 0)),
            pl.BlockSpec((bq, LANES), const2),
            pl.BlockSpec((bq, LANES), const2),
            pl.BlockSpec((bq, bq), const2),
            pl.BlockSpec((8, HEAD_DIM), const2),
            pl.BlockSpec((1, w), const2),
        ],
        out_specs=pl.BlockSpec((bq, w), lambda h, p, qi, kj, c: (qi[p], h)),
        scratch_shapes=[pltpu.VMEM((2, bq, 2 * HEAD_DIM), BF16),
                        pltpu.VMEM((2 * bq, LANES), F32),
                        pltpu.VMEM((2 * bq, w + LANES), F32)],
    )
    return pl.pallas_call(
        functools.partial(_diff_kernel, bq=bq, lam0=lam0),
        grid_spec=grid_spec,
        out_shape=jax.ShapeDtypeStruct((s, DIFF_WIDTH), BF16),
        compiler_params=_params(2),
        name="diff_attention",
    )(qi, kj, cvals, proj, proj, proj, qext, _kpos_ext(bq), _ones_block(bq), _causal_tile(bq),
      lam_rows, gain)


def _compress_kernel(a_ref, pos_ref, w1_ref, w2_ref, o_ref):
    half = CMP_STRIDE * HEAD_DIM
    a = a_ref[...].astype(F32)
    pos = pos_ref[...]
    top = (a + pos[:, :half]).astype(BF16)
    bot = (a + pos[:, half:]).astype(BF16)
    t = jnp.dot(top, w1_ref[:half, :], preferred_element_type=F32)
    b = jnp.dot(bot, w1_ref[half:, :], preferred_element_type=F32)
    rows = a.shape[0]
    hid = t + pltpu.roll(b, rows - 1, 0)
    act = jax.nn.gelu(hid)
    o_ref[...] = jnp.dot(act.astype(BF16), w2_ref[...], preferred_element_type=F32).astype(o_ref.dtype)


def _compress(a, pos, w1, w2):
    _, hkv, rows, width = a.shape
    return pl.pallas_call(
        _compress_kernel,
        grid=(2, hkv),
        in_specs=[pl.BlockSpec((None, None, rows, width), lambda t, h: (t, h, 0, 0)),
                  pl.BlockSpec((None, 1, 2 * width), lambda t, h: (t, 0, 0)),
                  pl.BlockSpec((None, 2 * width, CMP_HIDDEN), lambda t, h: (t, 0, 0)),
                  pl.BlockSpec((None, CMP_HIDDEN, HEAD_DIM), lambda t, h: (t, 0, 0))],
        out_specs=pl.BlockSpec((None, None, rows, HEAD_DIM), lambda t, h: (t, h, 0, 0)),
        out_shape=jax.ShapeDtypeStruct((2, hkv, rows, HEAD_DIM), BF16),
        compiler_params=_params(2),
        name="nsa_compress",
    )(a, pos, w1, w2)


def _split3(x):
    hi = x.astype(BF16)
    r1 = x - hi.astype(F32)
    mid = r1.astype(BF16)
    lo = (r1 - mid.astype(F32)).astype(BF16)
    return hi, mid, lo


def _cmp_select_kernel(c_ref, q_ref, kc_ref, vc_ref, ov_ref, oc_ref, sel_ref, any_ref,
                       *, bq, n_cmp, topn):
    hkv = pl.program_id(0)
    i = pl.program_id(1)
    ncp = kc_ref.shape[0]
    q = q_ref[...].reshape(NSA_GROUP * bq, HEAD_DIM)
    s = lax.dot_general(q, kc_ref[...], NT_DIMS, preferred_element_type=F32)
    t = i * bq + lax.broadcasted_iota(jnp.int32, (bq, ncp), 0)
    cidx = lax.broadcasted_iota(jnp.int32, (bq, ncp), 1)
    dist = t - (cidx * CMP_STRIDE + CMP_BLOCK - 1)
    ok = (dist >= 0) & (cidx < n_cmp)
    distf = dist.astype(F32)
    vc = vc_ref[...]
    psum = jnp.zeros((bq, ncp), F32)
    for g in range(NSA_GROUP):
        sg = jnp.where(ok, s[g * bq:(g + 1) * bq] - c_ref[hkv * NSA_GROUP + g] * distf, NEG)
        mx = jnp.max(sg, axis=1, keepdims=True)
        e = jnp.where(ok, jnp.exp2(sg - mx), 0.0)
        den = jnp.sum(e, axis=1, keepdims=True)
        pg = e / jnp.where(den > 0.0, den, 1.0)
        oc_ref[:, g * HEAD_DIM:(g + 1) * HEAD_DIM] = jnp.dot(
            pg.astype(BF16), vc, preferred_element_type=F32)
        psum = psum + pg
    ov = ov_ref[...]
    imp = jnp.zeros((bq, LANES), F32)
    for piece in _split3(psum):
        imp = imp + jnp.dot(piece, ov, preferred_element_type=F32)
    imp_t = imp.T
    tq = i * bq + lax.broadcasted_iota(jnp.int32, (LANES, bq), 1)
    blk = lax.broadcasted_iota(jnp.int32, (LANES, bq), 0)
    cur = lax.shift_right_arithmetic(tq, SEL_BLOCK.bit_length() - 1)
    forced = (blk == 0) | (blk == cur) | (blk == cur - 1)
    score = jnp.where(forced, FORCE_SCORE, jnp.where(blk <= cur, imp_t, -1.0))
    removed = -3.0e38
    blkf = blk.astype(F32)

    def pick(_, carry):
        sc, sel = carry
        mx = jnp.max(sc, axis=0, keepdims=True)
        first = jnp.min(jnp.where(sc == mx, blkf, float(LANES)), axis=0, keepdims=True)
        hit = blkf == first
        return jnp.where(hit, removed, sc), jnp.where(hit, 1.0, sel)

    _, sel_t = lax.fori_loop(0, topn, pick, (score, jnp.zeros((LANES, bq), F32)))
    sel = sel_t.T
    sel_ref[...] = sel.astype(sel_ref.dtype)
    any_ref[...] = jnp.max(sel, axis=0, keepdims=True)


def _cmp_select(proj, kvc, overlap, cvals, bq):
    s = proj.shape[1]
    n_cmp = (s - CMP_BLOCK) // CMP_STRIDE + 1
    n_sel = s // SEL_BLOCK
    topn = min(SEL_TOPN, n_sel)
    ncp = kvc.shape[2]
    gw = NSA_GROUP * HEAD_DIM
    grid_spec = pltpu.PrefetchScalarGridSpec(
        num_scalar_prefetch=1,
        grid=(NSA_KV_HEADS, s // bq),
        in_specs=[
            pl.BlockSpec((NSA_GROUP, bq, LANES), lambda h, i, c: (OFF_NQ // gw + h, i, 0)),
            pl.BlockSpec((None, None, ncp, HEAD_DIM), lambda h, i, c: (0, h, 0, 0)),
            pl.BlockSpec((None, None, ncp, HEAD_DIM), lambda h, i, c: (1, h, 0, 0)),
            pl.BlockSpec((ncp, LANES), lambda h, i, c: (0, 0)),
        ],
        out_specs=[pl.BlockSpec((bq, gw), lambda h, i, c: (i, h)),
                   pl.BlockSpec((None, bq, LANES), lambda h, i, c: (h, i, 0)),
                   pl.BlockSpec((None, None, 1, LANES), lambda h, i, c: (h, i, 0, 0))],
    )
    return pl.pallas_call(
        functools.partial(_cmp_select_kernel, bq=bq, n_cmp=n_cmp, topn=topn),
        grid_spec=grid_spec,
        out_shape=[jax.ShapeDtypeStruct((s, NSA_WIDTH), F32),
                   jax.ShapeDtypeStruct((NSA_KV_HEADS, s, LANES), BF16),
                   jax.ShapeDtypeStruct((NSA_KV_HEADS, s // bq, 1, LANES), F32)],
        compiler_params=_params(2),
        name="nsa_cmp_select",
    )(cvals, proj, kvc, kvc, overlap)


def _sel_kernel(qi_ref, kj_ref, nact_ref, c_ref, q_ref, k_ref, v_ref, sel_ref, exp_ref, qext_ref,
                kext_ref, one_ref, causal_ref, o_ref, qa_ref, m_ref, acc_ref, *, bq, n_pairs):
    hkv = pl.program_id(0)
    p = pl.program_id(1)
    qi = qi_ref[hkv * n_pairs + p]
    kj = kj_ref[hkv * n_pairs + p]
    live = p < nact_ref[hkv]

    @pl.when(live & (kj == 0))
    def _():
        m_ref[...] = jnp.full_like(m_ref, NEG)
        acc_ref[...] = jnp.zeros_like(acc_ref)
        for g in range(NSA_GROUP):
            qa_ref[g * bq:(g + 1) * bq, :HEAD_DIM] = q_ref[g]
            qa_ref[g * bq:(g + 1) * bq, HEAD_DIM:] = jnp.broadcast_to(qext_ref[g], (bq, LANES))

    def step(diag):
        one = one_ref[...]
        maskb = jnp.dot(jnp.concatenate([sel_ref[...], one], axis=1), exp_ref[...],
                        preferred_element_type=F32)
        if diag:
            maskb = maskb + causal_ref[...]
        ka = jnp.concatenate([k_ref[...], kext_ref[...]], axis=1)
        va = jnp.concatenate([v_ref[...], one], axis=1)
        s_all = lax.dot_general(qa_ref[...], ka, NT_DIMS, preferred_element_type=F32)
        dtile = ((qi - kj) * bq).astype(F32)
        for g in range(NSA_GROUP):
            c = 0.0 if diag else -c_ref[hkv * NSA_GROUP + g] * dtile
            rows = pl.ds(g * bq, bq)
            pr, alpha = _flash_update(s_all[g * bq:(g + 1) * bq] + maskb, c, m_ref, rows)
            pv = jnp.dot(pr.astype(BF16), va, preferred_element_type=F32)
            acc_ref[rows, :] = _lanes(alpha, 2 * HEAD_DIM) * acc_ref[rows, :] + pv

    @pl.when(live & (kj < qi))
    def _():
        step(False)

    @pl.when(live & (kj == qi))
    def _():
        step(True)
        for g in range(NSA_GROUP):
            rows = pl.ds(g * bq, bq)
            o_ref[:, g * HEAD_DIM:(g + 1) * HEAD_DIM] = (
                acc_ref[rows, :HEAD_DIM] / acc_ref[rows, HEAD_DIM:])


def _expand_aug(s, bk):
    key_blk = np.arange(s) // SEL_BLOCK
    e = np.zeros((2 * LANES, s), np.float32)
    e[key_blk, np.arange(s)] = MASK_BIG
    e[LANES, :] = -MASK_BIG
    return jnp.asarray(e.reshape(2 * LANES, s // bk, bk).transpose(1, 0, 2), BF16)


def _sel_schedule(any_rows, s, bq):
    nq = s // bq
    per_tile = bq // SEL_BLOCK
    a = any_rows.reshape(NSA_KV_HEADS, nq, -1, LANES).max(axis=2)
    a = a[:, :, :nq * per_tile].reshape(NSA_KV_HEADS, nq, nq, per_tile).max(axis=-1) > 0.0
    tq, tk = _tri_pairs(nq)
    act = a[:, tq, tk] | (tk == 0)[None, :] | (tk == tq)[None, :]
    order = jnp.argsort(jnp.logical_not(act), axis=1, stable=True)
    nact = jnp.sum(act, axis=1).astype(jnp.int32)
    pos = jnp.minimum(jnp.arange(tq.shape[0], dtype=jnp.int32)[None, :], nact[:, None] - 1)
    idx = jnp.take_along_axis(order, pos, axis=1)
    return tq[idx].reshape(-1), tk[idx].reshape(-1), nact


def _sel_attention(proj, sel, any_rows, cvals, qext, bq):
    s = proj.shape[1]
    qi, kj, nact = _sel_schedule(any_rows, s, bq)
    n_pairs = qi.shape[0] // NSA_KV_HEADS
    gw = NSA_GROUP * HEAD_DIM
    d = HEAD_DIM
    const2 = lambda h, p, qi, kj, na, c: (0, 0)
    grid_spec = pltpu.PrefetchScalarGridSpec(
        num_scalar_prefetch=4,
        grid=(NSA_KV_HEADS, n_pairs),
        in_specs=[
            pl.BlockSpec((NSA_GROUP, bq, d),
                         lambda h, p, qi, kj, na, c: (OFF_NQ // gw + h, qi[h * n_pairs + p], 0)),
            pl.BlockSpec((None, bq, d),
                         lambda h, p, qi, kj, na, c: (OFF_KS // d + h, kj[h * n_pairs + p], 0)),
            pl.BlockSpec((None, bq, d),
                         lambda h, p, qi, kj, na, c: (OFF_VS // d + h, kj[h * n_pairs + p], 0)),
            pl.BlockSpec((None, bq, LANES), lambda h, p, qi, kj, na, c: (h, qi[h * n_pairs + p], 0)),
            pl.BlockSpec((None, 2 * LANES, bq), lambda h, p, qi, kj, na, c: (kj[h * n_pairs + p], 0, 0)),
            pl.BlockSpec((None, NSA_GROUP, 1, LANES), lambda h, p, qi, kj, na, c: (h, 0, 0, 0)),
            pl.BlockSpec((bq, LANES), const2),
            pl.BlockSpec((bq, LANES), const2),
            pl.BlockSpec((bq, bq), const2),
        ],
        out_specs=pl.BlockSpec((bq, gw), lambda h, p, qi, kj, na, c: (qi[h * n_pairs + p], h)),
        scratch_shapes=[pltpu.VMEM((NSA_GROUP * bq, 2 * d), BF16),
                        pltpu.VMEM((NSA_GROUP * bq, LANES), F32),
                        pltpu.VMEM((NSA_GROUP * bq, 2 * d), F32)],
    )
    return pl.pallas_call(
        functools.partial(_sel_kernel, bq=bq, n_pairs=n_pairs),
        grid_spec=grid_spec,
        out_shape=jax.ShapeDtypeStruct((s, NSA_WIDTH), F32),
        compiler_params=_params(2),
        name="nsa_selected",
    )(qi, kj, nact, cvals, proj, proj, proj, sel, _expand_aug(s, bq),
      qext.reshape(NSA_KV_HEADS, NSA_GROUP, 1, LANES), _kpos_ext(bq), _ones_block(bq), _causal_tile(bq))


def _win_kernel(c_ref, q_ref, kp_ref, kc_ref, vp_ref, vc_ref, one_ref, oc_ref, os_ref, gate_ref, o_ref,
                bias_ref):
    hkv = pl.program_id(0)
    i = pl.program_id(1)
    bq = WINDOW

    @pl.when(i == 0)
    def _():
        rel = (lax.broadcasted_iota(jnp.int32, (bq, bq), 0)
               - lax.broadcasted_iota(jnp.int32, (bq, bq), 1))
        relf = rel.astype(F32)
        for g in range(NSA_GROUP):
            c = c_ref[hkv * NSA_GROUP + g]
            bias_ref[g, 0] = jnp.where(rel < 0, -c * (relf + WINDOW), NEG)
            bias_ref[g, 1] = jnp.where(rel >= 0, -c * relf, NEG)

    q = q_ref[...].reshape(NSA_GROUP * bq, HEAD_DIM)
    sp_all = lax.dot_general(q, kp_ref[...], NT_DIMS, preferred_element_type=F32)
    sc_all = lax.dot_general(q, kc_ref[...], NT_DIMS, preferred_element_type=F32)
    first = jnp.where(i == 0, NEG, 0.0)
    one = one_ref[...]
    vp = jnp.concatenate([vp_ref[...], one], axis=1)
    vc = jnp.concatenate([vc_ref[...], one], axis=1)
    gates = gate_ref[...]
    for g in range(NSA_GROUP):
        rows = slice(g * bq, (g + 1) * bq)
        cols = slice(g * HEAD_DIM, (g + 1) * HEAD_DIM)
        sp = sp_all[rows] + bias_ref[g, 0] + first
        sc = sc_all[rows] + bias_ref[g, 1]
        mx = jnp.maximum(jnp.max(sp, axis=1, keepdims=True), jnp.max(sc, axis=1, keepdims=True))
        pp = jnp.exp2(sp - mx).astype(BF16)
        pc = jnp.exp2(sc - mx).astype(BF16)
        pv = (jnp.dot(pp, vp, preferred_element_type=F32)
              + jnp.dot(pc, vc, preferred_element_type=F32))
        ow = pv[:, :HEAD_DIM] / pv[:, HEAD_DIM:]
        gc = gates[:, 3 * g + 0:3 * g + 1]
        gs = gates[:, 3 * g + 1:3 * g + 2]
        gw = gates[:, 3 * g + 2:3 * g + 3]
        o_ref[:, cols] = (gc * oc_ref[:, cols] + gs * os_ref[:, cols] + gw * ow).astype(o_ref.dtype)


def _win_combine(proj, o_cmp, o_sel, gates, cvals):
    s = proj.shape[1]
    bq = WINDOW
    gw = NSA_GROUP * HEAD_DIM
    d = HEAD_DIM
    prev = lambda i: jnp.maximum(i - 1, 0)
    grid_spec = pltpu.PrefetchScalarGridSpec(
        num_scalar_prefetch=1,
        grid=(NSA_KV_HEADS, s // bq),
        in_specs=[
            pl.BlockSpec((NSA_GROUP, bq, d), lambda h, i, c: (OFF_NQ // gw + h, i, 0)),
            pl.BlockSpec((None, bq, d), lambda h, i, c: (OFF_KW // d + h, prev(i), 0)),
            pl.BlockSpec((None, bq, d), lambda h, i, c: (OFF_KW // d + h, i, 0)),
            pl.BlockSpec((None, bq, d), lambda h, i, c: (OFF_VW // d + h, prev(i), 0)),
            pl.BlockSpec((None, bq, d), lambda h, i, c: (OFF_VW // d + h, i, 0)),
            pl.BlockSpec((bq, LANES), lambda h, i, c: (0, 0)),
            pl.BlockSpec((bq, gw), lambda h, i, c: (i, h)),
            pl.BlockSpec((bq, gw), lambda h, i, c: (i, h)),
            pl.BlockSpec((bq, LANES), lambda h, i, c: (i, h)),
        ],
        out_specs=pl.BlockSpec((bq, gw), lambda h, i, c: (i, h)),
        scratch_shapes=[pltpu.VMEM((NSA_GROUP, 2, bq, bq), F32)],
    )
    return pl.pallas_call(
        _win_kernel,
        grid_spec=grid_spec,
        out_shape=jax.ShapeDtypeStruct((s, NSA_WIDTH), BF16),
        compiler_params=_params(2),
        name="nsa_window_combine",
    )(cvals, proj, proj, proj, proj, proj, _ones_block(bq), o_cmp, o_sel, gates)


def _ffn(h, norm_g, w_gate, w_up, w_down):
    u = _rmsnorm(h, norm_g, BF16)
    act = _ffn_up(u, w_gate, w_up)
    return _mm_res(act, w_down.astype(BF16), h, 0.5)


def _overlap_matrix(s, rows):
    n_cmp = (s - CMP_BLOCK) // CMP_STRIDE + 1
    n_sel = s // SEL_BLOCK
    cs = np.arange(rows)[:, None] * CMP_STRIDE
    ss = np.arange(LANES)[None, :] * SEL_BLOCK
    ov = (cs <= ss + SEL_BLOCK - 1) & (cs + CMP_BLOCK - 1 >= ss)
    ov &= (np.arange(rows)[:, None] < n_cmp) & (np.arange(LANES)[None, :] < n_sel)
    return jnp.asarray(ov, BF16)


def _query_colscale():
    cs = np.ones((1, OFF_G), np.float32)
    cs[:, OFF_DQ:OFF_DK] = QSCALE
    cs[:, OFF_NQ:OFF_KC] = QSCALE
    return jnp.asarray(cs)


def _layer(h, layer, ffn1_norm, ffn1_w_gate, ffn1_w_up, ffn1_w_down, mix_norm, w_in, gate_bias,
           lambda_q1, lambda_k1, lambda_q2, lambda_k2, diff_norm,
           cmp_pos_k, cmp_w1_k, cmp_w2_k, cmp_pos_v, cmp_w1_v, cmp_w2_v,
           w_out, ffn2_norm, ffn2_w_gate, ffn2_w_up, ffn2_w_down):
    s = h.shape[0]
    assert s % WINDOW == 0 and s // SEL_BLOCK <= LANES and OFF_G % 512 == 0
    h = _ffn(h, ffn1_norm, ffn1_w_gate, ffn1_w_up, ffn1_w_down)

    u = _rmsnorm(h, mix_norm, BF16)
    w_in_t = w_in.T
    proj = _proj_slabs(u, w_in_t, _query_colscale())
    wg = w_in_t[OFF_G:].T.reshape(-1, NSA_KV_HEADS, GATES_PER_KV)
    wg = jnp.pad(wg, ((0, 0), (0, 0), (0, LANES - GATES_PER_KV))).reshape(-1, NSA_KV_HEADS * LANES)
    gb = jnp.pad(gate_bias.astype(F32).reshape(NSA_KV_HEADS, GATES_PER_KV),
                 ((0, 0), (0, LANES - GATES_PER_KV))).reshape(1, NSA_KV_HEADS * LANES)
    gates = _gate_proj(u, wg.astype(BF16), gb)

    lam_rows = jnp.pad(jnp.stack([lambda_q1, lambda_k1, lambda_q2, lambda_k2]).astype(F32),
                       ((0, 4), (0, 0)))
    o_diff = _diff_attention(proj, lam_rows, diff_norm.reshape(1, -1).astype(F32), layer,
                             _pick(s, 1024))

    nsa_c, nsa_qext = _alibi_consts(NSA_HEADS)
    rows = s // CMP_STRIDE
    kv_cmp = proj[OFF_KC // LANES:OFF_KS // LANES].reshape(
        2, NSA_KV_HEADS, rows, CMP_STRIDE * HEAD_DIM)
    pos = jnp.stack([cmp_pos_k, cmp_pos_v]).astype(F32).reshape(2, 1, CMP_BLOCK * HEAD_DIM)
    w1 = jnp.stack([cmp_w1_k, cmp_w1_v]).astype(BF16)
    w2 = jnp.stack([cmp_w2_k, cmp_w2_v]).astype(BF16)
    kvc = _compress(kv_cmp, pos, w1, w2)
    o_cmp, sel, any_rows = _cmp_select(proj, kvc, _overlap_matrix(s, rows), nsa_c, _pick(s, 256))
    o_sel = _sel_attention(proj, sel, any_rows, nsa_c, nsa_qext, _pick(s, 512))
    o_nsa = _win_combine(proj, o_cmp, o_sel, gates, nsa_c)

    wo = w_out.astype(BF16)
    h = _mm2_res(o_diff, o_nsa, wo[:DIFF_WIDTH], wo[DIFF_WIDTH:], h)
    return _ffn(h, ffn2_norm, ffn2_w_gate, ffn2_w_up, ffn2_w_down)


def kernel(x, ffn1_norm, ffn1_w_gate, ffn1_w_up, ffn1_w_down, mix_norm, w_in, gate_bias, lambda_q1, lambda_k1, lambda_q2, lambda_k2, diff_norm, cmp_pos_k, cmp_w1_k, cmp_w2_k, cmp_pos_v, cmp_w1_v, cmp_w2_v, w_out, ffn2_norm, ffn2_w_gate, ffn2_w_up, ffn2_w_down, final_norm):
    b, s, d = x.shape
    per_layer = (ffn1_norm, ffn1_w_gate, ffn1_w_up, ffn1_w_down, mix_norm, w_in, gate_bias,
                 lambda_q1, lambda_k1, lambda_q2, lambda_k2, diff_norm,
                 cmp_pos_k, cmp_w1_k, cmp_w2_k, cmp_pos_v, cmp_w1_v, cmp_w2_v,
                 w_out, ffn2_norm, ffn2_w_gate, ffn2_w_up, ffn2_w_down)
    outs = []
    for bi in range(b):
        h = x.reshape(s, d) if b == 1 else x[bi]
        for layer in range(DEPTH):
            h = _layer(h, layer, *[p[layer] for p in per_layer])
        outs.append(_rmsnorm(h, final_norm, x.dtype))
    return outs[0].reshape(1, s, d) if b == 1 else jnp.stack(outs)
```

```python
import functools
import math

import ml_dtypes
import numpy as np
import jax
import jax.numpy as jnp
from jax import lax
from jax.experimental import pallas as pl
from jax.experimental.pallas import tpu as pltpu

D_MODEL = 4096
DEPTH = 1
HEAD_DIM = 128
DIFF_V_DIM = 2 * HEAD_DIM
DIFF_HEADS = (D_MODEL // 2) // DIFF_V_DIM
DIFF_WIDTH = DIFF_HEADS * DIFF_V_DIM
NSA_HEADS = (D_MODEL - DIFF_WIDTH) // HEAD_DIM
NSA_KV_HEADS = 4
NSA_GROUP = NSA_HEADS // NSA_KV_HEADS
NSA_WIDTH = NSA_HEADS * HEAD_DIM
CMP_BLOCK = 32
CMP_STRIDE = 16
CMP_HIDDEN = 256
SEL_BLOCK = 64
SEL_TOPN = 16
WINDOW = 512
EPS = 1e-6
NEG = -1e30
FORCE_SCORE = 1e4

LANES = 128
VMEM_LIMIT = 56 * 1024 * 1024
LOG2E = 1.4426950408889634
QSCALE = HEAD_DIM ** -0.5 * LOG2E
MASK_BIG = 2.0 ** 100
KPOS_SPLIT = 32
OFF_DQ = 0
OFF_DK = OFF_DQ + DIFF_HEADS * 2 * HEAD_DIM
OFF_DV = OFF_DK + DIFF_HEADS * 2 * HEAD_DIM
OFF_NQ = OFF_DV + DIFF_HEADS * DIFF_V_DIM
OFF_KC = OFF_NQ + NSA_HEADS * HEAD_DIM
OFF_VC = OFF_KC + NSA_KV_HEADS * HEAD_DIM
OFF_KS = OFF_VC + NSA_KV_HEADS * HEAD_DIM
OFF_VS = OFF_KS + NSA_KV_HEADS * HEAD_DIM
OFF_KW = OFF_VS + NSA_KV_HEADS * HEAD_DIM
OFF_VW = OFF_KW + NSA_KV_HEADS * HEAD_DIM
OFF_G = OFF_VW + NSA_KV_HEADS * HEAD_DIM
GATES_PER_KV = 3 * NSA_GROUP

F32 = jnp.float32
BF16 = jnp.bfloat16
NT_DIMS = (((1,), (1,)), ((), ()))


def _params(n_axes):
    return pltpu.CompilerParams(dimension_semantics=("arbitrary",) * n_axes,
                                vmem_limit_bytes=VMEM_LIMIT)


def _pick(n, pref):
    b = min(pref, n)
    while n % b:
        b //= 2
    return b


def _rmsnorm_kernel(x_ref, g_ref, o_ref):
    x = x_ref[...]
    ms = jnp.mean(x * x, axis=-1, keepdims=True)
    o_ref[...] = (x * lax.rsqrt(ms + EPS) * g_ref[...]).astype(o_ref.dtype)


def _rmsnorm(x, g, out_dtype):
    s, d = x.shape
    bm = _pick(s, 256)
    return pl.pallas_call(
        _rmsnorm_kernel,
        grid=(s // bm,),
        in_specs=[pl.BlockSpec((bm, d), lambda i: (i, 0)),
                  pl.BlockSpec((1, d), lambda i: (0, 0))],
        out_specs=pl.BlockSpec((bm, d), lambda i: (i, 0)),
        out_shape=jax.ShapeDtypeStruct((s, d), out_dtype),
        compiler_params=_params(1),
        name="rmsnorm",
    )(x, g.reshape(1, d).astype(F32))


def _ffn_up_kernel(u_ref, wg_ref, wu_ref, o_ref):
    u = u_ref[...]
    g = jnp.dot(u, wg_ref[...].astype(BF16), preferred_element_type=F32)
    up = jnp.dot(u, wu_ref[...].astype(BF16), preferred_element_type=F32)
    o_ref[...] = (g * jax.nn.sigmoid(g) * up).astype(o_ref.dtype)


def _ffn_up(u, wg, wu):
    s, d = u.shape
    f = wg.shape[1]
    bm, bn = _pick(s, 2048), _pick(f, 256)
    return pl.pallas_call(
        _ffn_up_kernel,
        grid=(s // bm, f // bn),
        in_specs=[pl.BlockSpec((bm, d), lambda i, j: (i, 0), pipeline_mode=pl.Buffered(1)),
                  pl.BlockSpec((d, bn), lambda i, j: (0, j)),
                  pl.BlockSpec((d, bn), lambda i, j: (0, j))],
        out_specs=pl.BlockSpec((bm, bn), lambda i, j: (i, j)),
        out_shape=jax.ShapeDtypeStruct((s, f), BF16),
        compiler_params=_params(2),
        name="ffn_up",
    )(u, wg, wu)


def _mm_res_kernel(a_ref, b_ref, r_ref, o_ref, *, alpha):
    o_ref[...] = r_ref[...] + alpha * jnp.dot(a_ref[...], b_ref[...], preferred_element_type=F32)


def _mm_res(a, b, res, alpha):
    s, kdim = a.shape
    n = b.shape[1]
    bm, bn = _pick(s, 512), _pick(n, 512)
    return pl.pallas_call(
        functools.partial(_mm_res_kernel, alpha=alpha),
        grid=(s // bm, n // bn),
        in_specs=[pl.BlockSpec((bm, kdim), lambda i, j: (i, 0)),
                  pl.BlockSpec((kdim, bn), lambda i, j: (0, j)),
                  pl.BlockSpec((bm, bn), lambda i, j: (i, j))],
        out_specs=pl.BlockSpec((bm, bn), lambda i, j: (i, j)),
        out_shape=jax.ShapeDtypeStruct((s, n), F32),
        compiler_params=_params(2),
        name="mm_res",
    )(a, b, res)


def _proj_kernel(a_ref, bt_ref, cs_ref, o_ref):
    acc = lax.dot_general(a_ref[...], bt_ref[...].astype(BF16), NT_DIMS,
                          preferred_element_type=F32) * cs_ref[...]
    for t in range(o_ref.shape[0]):
        o_ref[t] = acc[:, t * LANES:(t + 1) * LANES].astype(o_ref.dtype)


def _proj_slabs(a, bt, colscale):
    s, kdim = a.shape
    n = colscale.shape[1]
    bm, bn = _pick(s, 2048), _pick(n, 512)
    return pl.pallas_call(
        _proj_kernel,
        grid=(s // bm, n // bn),
        in_specs=[pl.BlockSpec((bm, kdim), lambda i, j: (i, 0), pipeline_mode=pl.Buffered(1)),
                  pl.BlockSpec((bn, kdim), lambda i, j: (j, 0)),
                  pl.BlockSpec((1, bn), lambda i, j: (0, j))],
        out_specs=pl.BlockSpec((bn // LANES, bm, LANES), lambda i, j: (j, i, 0)),
        out_shape=jax.ShapeDtypeStruct((n // LANES, s, LANES), BF16),
        compiler_params=_params(2),
        name="proj_slabs",
    )(a, bt, colscale)


def _gate_kernel(a_ref, b_ref, bias_ref, o_ref):
    z = jnp.dot(a_ref[...], b_ref[...], preferred_element_type=F32) + bias_ref[...]
    o_ref[...] = jax.nn.sigmoid(z)


def _gate_proj(u, wg, bias):
    s, kdim = u.shape
    n = wg.shape[1]
    bm = _pick(s, 1024)
    return pl.pallas_call(
        _gate_kernel,
        grid=(s // bm,),
        in_specs=[pl.BlockSpec((bm, kdim), lambda i: (i, 0)),
                  pl.BlockSpec((kdim, n), lambda i: (0, 0)),
                  pl.BlockSpec((1, n), lambda i: (0, 0))],
        out_specs=pl.BlockSpec((bm, n), lambda i: (i, 0)),
        out_shape=jax.ShapeDtypeStruct((s, n), F32),
        compiler_params=_params(1),
        name="gate_proj",
    )(u, wg, bias)


def _mm2_res_kernel(a1_ref, a2_ref, b1_ref, b2_ref, r_ref, o_ref):
    acc = jnp.dot(a1_ref[...], b1_ref[...], preferred_element_type=F32)
    acc += jnp.dot(a2_ref[...], b2_ref[...], preferred_element_type=F32)
    o_ref[...] = r_ref[...] + acc


def _mm2_res(a1, a2, b1, b2, res):
    s, k1 = a1.shape
    k2 = a2.shape[1]
    n = b1.shape[1]
    bm, bn = _pick(s, 1024), _pick(n, 512)
    return pl.pallas_call(
        _mm2_res_kernel,
        grid=(s // bm, n // bn),
        in_specs=[pl.BlockSpec((bm, k1), lambda i, j: (i, 0)),
                  pl.BlockSpec((bm, k2), lambda i, j: (i, 0)),
                  pl.BlockSpec((k1, bn), lambda i, j: (0, j)),
                  pl.BlockSpec((k2, bn), lambda i, j: (0, j)),
                  pl.BlockSpec((bm, bn), lambda i, j: (i, j))],
        out_specs=pl.BlockSpec((bm, bn), lambda i, j: (i, j)),
        out_shape=jax.ShapeDtypeStruct((s, n), F32),
        compiler_params=_params(2),
        name="mm2_res",
    )(a1, a2, b1, b2, res)


def _tri_pairs(n):
    qi = np.repeat(np.arange(n), np.arange(1, n + 1))
    kj = np.concatenate([np.arange(i + 1) for i in range(n)])
    return jnp.asarray(qi, jnp.int32), jnp.asarray(kj, jnp.int32)


def _bf16_round(x):
    return np.asarray(x, np.float32).astype(ml_dtypes.bfloat16).astype(np.float64)


def _alibi_consts(n_heads):
    c = np.float32(2.0 ** (-8.0 * np.arange(1, n_heads + 1) / n_heads) * LOG2E).astype(np.float64)
    hi = _bf16_round(c)
    mid = _bf16_round(c - hi)
    lo = _bf16_round(c - hi - mid)
    rows = np.zeros((n_heads, 1, LANES), np.float32)
    for col, piece in enumerate((hi, hi, mid, mid, lo, lo)):
        rows[:, 0, col] = piece
    return jnp.asarray(c, F32), jnp.asarray(rows, BF16)


def _kpos_ext(bk):
    pos = np.arange(bk)
    a = (pos // KPOS_SPLIT) * KPOS_SPLIT
    b = pos % KPOS_SPLIT
    ext = np.zeros((bk, LANES), np.float32)
    for col in range(0, 6, 2):
        ext[:, col] = a
        ext[:, col + 1] = b
    return jnp.asarray(ext, BF16)


def _ones_block(rows):
    return jnp.ones((rows, LANES), BF16)


def _causal_tile(b):
    r = np.arange(b)
    return jnp.asarray(np.where(r[:, None] >= r[None, :], 0.0, NEG), F32)


def _lanes(x, width):
    if width == LANES:
        return x
    return jnp.tile(x, (1, width // LANES))


def _flash_update(s, c, m_ref, rows):
    m_prev = m_ref[rows, :]
    m_cur = jnp.max(s, axis=1, keepdims=True) + c
    m_next = jnp.maximum(m_prev, m_cur)
    m_ref[rows, :] = m_next
    p = jnp.exp2(s - _lanes(m_next - c, s.shape[1]))
    return p, jnp.exp2(m_prev - m_next)


def _diff_kernel(qi_ref, kj_ref, c_ref, q_ref, k_ref, v_ref, qext_ref, kext_ref, one_ref, causal_ref,
                 lam_ref, g_ref, o_ref, qa_ref, m_ref, acc_ref, *, bq, lam0):
    h = pl.program_id(0)
    p = pl.program_id(1)
    qi = qi_ref[p]
    kj = kj_ref[p]
    acc_w = DIFF_V_DIM + LANES

    @pl.when(kj == 0)
    def _():
        m_ref[...] = jnp.full_like(m_ref, NEG)
        acc_ref[...] = jnp.zeros_like(acc_ref)
        ext = jnp.broadcast_to(qext_ref[...], (bq, LANES))
        for mp in range(2):
            qa_ref[mp, :, :HEAD_DIM] = q_ref[mp]
            qa_ref[mp, :, HEAD_DIM:] = ext

    def step(diag):
        c = 0.0 if diag else -c_ref[h] * ((qi - kj) * bq).astype(F32)
        va = jnp.concatenate([v_ref[0], v_ref[1], one_ref[...]], axis=1)
        kext = kext_ref[...]
        blocks = ((0, bq // 2, bq // 2), (bq // 2, bq // 2, bq)) if diag else ((0, bq, bq),)
        for mp in range(2):
            ka = jnp.concatenate([k_ref[mp], kext], axis=1)
            for r0, nr, nk in blocks:
                s = lax.dot_general(qa_ref[mp, r0:r0 + nr, :], ka[:nk], NT_DIMS,
                                    preferred_element_type=F32)
                if diag:
                    s = s + causal_ref[r0:r0 + nr, :nk]
                rows = pl.ds(mp * bq + r0, nr)
                pr, alpha = _flash_update(s, c, m_ref, rows)
                pv = jnp.dot(pr.astype(BF16), va[:nk], preferred_element_type=F32)
                acc_ref[rows, :] = _lanes(alpha, acc_w) * acc_ref[rows, :] + pv

    @pl.when(kj < qi)
    def _():
        step(False)

    @pl.when(kj == qi)
    def _():
        step(True)
        lam_rows = lam_ref[...]
        d1 = jnp.sum(lam_rows[0:1] * lam_rows[1:2], axis=1, keepdims=True)
        d2 = jnp.sum(lam_rows[2:3] * lam_rows[3:4], axis=1, keepdims=True)
        lam = jnp.exp(d1) - jnp.exp(d2) + lam0
        o1 = acc_ref[0:bq, :DIFF_V_DIM] / _lanes(acc_ref[0:bq, DIFF_V_DIM:], DIFF_V_DIM)
        o2 = acc_ref[bq:2 * bq, :DIFF_V_DIM] / _lanes(acc_ref[bq:2 * bq, DIFF_V_DIM:], DIFF_V_DIM)
        o = o1 - lam * o2
        ms = jnp.mean(o * o, axis=-1, keepdims=True)
        y = o * lax.rsqrt(ms + EPS) * g_ref[...]
        o_ref[...] = (y * (1.0 - lam0)).astype(o_ref.dtype)


def _diff_attention(proj, lam_rows, gain, layer, bq):
    s = proj.shape[1]
    qi, kj = _tri_pairs(s // bq)
    cvals, qext = _alibi_consts(DIFF_HEADS)
    lam0 = 0.8 - 0.6 * math.exp(-0.3 * layer)
    w = DIFF_V_DIM
    const2 = lambda h, p, qi, kj, c: (0, 0)
    grid_spec = pltpu.PrefetchScalarGridSpec(
        num_scalar_prefetch=3,
        grid=(DIFF_HEADS, qi.shape[0]),
        in_specs=[
            pl.BlockSpec((2, bq, LANES), lambda h, p, qi, kj, c: (OFF_DQ // w + h, qi[p], 0)),
            pl.BlockSpec((2, bq, LANES), lambda h, p, qi, kj, c: (OFF_DK // w + h, kj[p], 0)),
            pl.BlockSpec((2, bq, LANES), lambda h, p, qi, kj, c: (OFF_DV // w + h, kj[p], 0)),
            pl.BlockSpec((None, 1, LANES), lambda h, p, qi, kj, c: (h, 0, 0)),
            pl.BlockSpec((bq, LANES), const2),
            pl.BlockSpec((bq, LANES), const2),
            pl.BlockSpec((bq, bq), const2),
            pl.BlockSpec((8, HEAD_DIM), const2),
            pl.BlockSpec((1, w), const2),
        ],
        out_specs=pl.BlockSpec((bq, w), lambda h, p, qi, kj, c: (qi[p], h)),
        scratch_shapes=[pltpu.VMEM((2, bq, 2 * HEAD_DIM), BF16),
                        pltpu.VMEM((2 * bq, LANES), F32),
                        pltpu.VMEM((2 * bq, w + LANES), F32)],
    )
    return pl.pallas_call(
        functools.partial(_diff_kernel, bq=bq, lam0=lam0),
        grid_spec=grid_spec,
        out_shape=jax.ShapeDtypeStruct((s, DIFF_WIDTH), BF16),
        compiler_params=_params(2),
        name="diff_attention",
    )(qi, kj, cvals, proj, proj, proj, qext, _kpos_ext(bq), _ones_block(bq), _causal_tile(bq),
      lam_rows, gain)


def _compress_kernel(a_ref, pos_ref, w1_ref, w2_ref, o_ref):
    half = CMP_STRIDE * HEAD_DIM
    a = a_ref[...].astype(F32)
    pos = pos_ref[...]
    top = (a + pos[:, :half]).astype(BF16)
    bot = (a + pos[:, half:]).astype(BF16)
    t = jnp.dot(top, w1_ref[:half, :], preferred_element_type=F32)
    b = jnp.dot(bot, w1_ref[half:, :], preferred_element_type=F32)
    rows = a.shape[0]
    hid = t + pltpu.roll(b, rows - 1, 0)
    act = jax.nn.gelu(hid)
    o_ref[...] = jnp.dot(act.astype(BF16), w2_ref[...], preferred_element_type=F32).astype(o_ref.dtype)


def _compress(a, pos, w1, w2):
    _, hkv, rows, width = a.shape
    return pl.pallas_call(
        _compress_kernel,
        grid=(2, hkv),
        in_specs=[pl.BlockSpec((None, None, rows, width), lambda t, h: (t, h, 0, 0)),
                  pl.BlockSpec((None, 1, 2 * width), lambda t, h: (t, 0, 0)),
                  pl.BlockSpec((None, 2 * width, CMP_HIDDEN), lambda t, h: (t, 0, 0)),
                  pl.BlockSpec((None, CMP_HIDDEN, HEAD_DIM), lambda t, h: (t, 0, 0))],
        out_specs=pl.BlockSpec((None, None, rows, HEAD_DIM), lambda t, h: (t, h, 0, 0)),
        out_shape=jax.ShapeDtypeStruct((2, hkv, rows, HEAD_DIM), BF16),
        compiler_params=_params(2),
        name="nsa_compress",
    )(a, pos, w1, w2)


def _split3(x):
    hi = x.astype(BF16)
    r1 = x - hi.astype(F32)
    mid = r1.astype(BF16)
    lo = (r1 - mid.astype(F32)).astype(BF16)
    return hi, mid, lo


def _cmp_select_kernel(c_ref, q_ref, kc_ref, vc_ref, ov_ref, oc_ref, sel_ref, any_ref,
                       *, bq, n_cmp, topn):
    hkv = pl.program_id(0)
    i = pl.program_id(1)
    ncp = kc_ref.shape[0]
    q = q_ref[...].reshape(NSA_GROUP * bq, HEAD_DIM)
    s = lax.dot_general(q, kc_ref[...], NT_DIMS, preferred_element_type=F32)
    t = i * bq + lax.broadcasted_iota(jnp.int32, (bq, ncp), 0)
    cidx = lax.broadcasted_iota(jnp.int32, (bq, ncp), 1)
    dist = t - (cidx * CMP_STRIDE + CMP_BLOCK - 1)
    ok = (dist >= 0) & (cidx < n_cmp)
    distf = dist.astype(F32)
    vc = vc_ref[...]
    psum = jnp.zeros((bq, ncp), F32)
    for g in range(NSA_GROUP):
        sg = jnp.where(ok, s[g * bq:(g + 1) * bq] - c_ref[hkv * NSA_GROUP + g] * distf, NEG)
        mx = jnp.max(sg, axis=1, keepdims=True)
        e = jnp.where(ok, jnp.exp2(sg - mx), 0.0)
        den = jnp.sum(e, axis=1, keepdims=True)
        pg = e / jnp.where(den > 0.0, den, 1.0)
        oc_ref[:, g * HEAD_DIM:(g + 1) * HEAD_DIM] = jnp.dot(
            pg.astype(BF16), vc, preferred_element_type=F32)
        psum = psum + pg
    ov = ov_ref[...]
    imp = jnp.zeros((bq, LANES), F32)
    for piece in _split3(psum):
        imp = imp + jnp.dot(piece, ov, preferred_element_type=F32)
    imp_t = imp.T
    tq = i * bq + lax.broadcasted_iota(jnp.int32, (LANES, bq), 1)
    blk = lax.broadcasted_iota(jnp.int32, (LANES, bq), 0)
    cur = lax.shift_right_arithmetic(tq, SEL_BLOCK.bit_length() - 1)
    forced = (blk == 0) | (blk == cur) | (blk == cur - 1)
    score = jnp.where(forced, FORCE_SCORE, jnp.where(blk <= cur, imp_t, -1.0))
    removed = -3.0e38
    blkf = blk.astype(F32)

    def pick(_, sc):
        mx = jnp.max(sc, axis=0, keepdims=True)
        first = jnp.min(jnp.where(sc == mx, blkf, float(LANES)), axis=0, keepdims=True)
        return jnp.where(blkf == first, removed, sc)

    picked = lax.fori_loop(0, topn, pick, score)
    sel = jnp.where(picked == removed, 1.0, 0.0).T
    sel_ref[...] = sel.astype(sel_ref.dtype)
    any_ref[...] = jnp.max(sel, axis=0, keepdims=True)


def _cmp_select(proj, kvc, overlap, cvals, bq):
    s = proj.shape[1]
    n_cmp = (s - CMP_BLOCK) // CMP_STRIDE + 1
    n_sel = s // SEL_BLOCK
    topn = min(SEL_TOPN, n_sel)
    ncp = kvc.shape[2]
    gw = NSA_GROUP * HEAD_DIM
    grid_spec = pltpu.PrefetchScalarGridSpec(
        num_scalar_prefetch=1,
        grid=(NSA_KV_HEADS, s // bq),
        in_specs=[
            pl.BlockSpec((NSA_GROUP, bq, LANES), lambda h, i, c: (OFF_NQ // gw + h, i, 0)),
            pl.BlockSpec((None, None, ncp, HEAD_DIM), lambda h, i, c: (0, h, 0, 0)),
            pl.BlockSpec((None, None, ncp, HEAD_DIM), lambda h, i, c: (1, h, 0, 0)),
            pl.BlockSpec((ncp, LANES), lambda h, i, c: (0, 0)),
        ],
        out_specs=[pl.BlockSpec((bq, gw), lambda h, i, c: (i, h)),
                   pl.BlockSpec((None, bq, LANES), lambda h, i, c: (h, i, 0)),
                   pl.BlockSpec((None, None, 1, LANES), lambda h, i, c: (h, i, 0, 0))],
    )
    return pl.pallas_call(
        functools.partial(_cmp_select_kernel, bq=bq, n_cmp=n_cmp, topn=topn),
        grid_spec=grid_spec,
        out_shape=[jax.ShapeDtypeStruct((s, NSA_WIDTH), F32),
                   jax.ShapeDtypeStruct((NSA_KV_HEADS, s, LANES), BF16),
                   jax.ShapeDtypeStruct((NSA_KV_HEADS, s // bq, 1, LANES), F32)],
        compiler_params=_params(2),
        name="nsa_cmp_select",
    )(cvals, proj, kvc, kvc, overlap)


def _sel_kernel(qi_ref, kj_ref, nact_ref, c_ref, q_ref, k_ref, v_ref, sel_ref, exp_ref, qext_ref,
                kext_ref, one_ref, causal_ref, o_ref, qa_ref, m_ref, acc_ref, *, bq, n_pairs):
    hkv = pl.program_id(0)
    p = pl.program_id(1)
    qi = qi_ref[hkv * n_pairs + p]
    kj = kj_ref[hkv * n_pairs + p]
    live = p < nact_ref[hkv]

    @pl.when(live & (kj == 0))
    def _():
        m_ref[...] = jnp.full_like(m_ref, NEG)
        acc_ref[...] = jnp.zeros_like(acc_ref)
        for g in range(NSA_GROUP):
            qa_ref[g * bq:(g + 1) * bq, :HEAD_DIM] = q_ref[g]
            qa_ref[g * bq:(g + 1) * bq, HEAD_DIM:] = jnp.broadcast_to(qext_ref[g], (bq, LANES))

    def step(diag):
        one = one_ref[...]
        maskb = jnp.dot(jnp.concatenate([sel_ref[...], one], axis=1), exp_ref[...],
                        preferred_element_type=F32)
        if diag:
            maskb = maskb + causal_ref[...]
        ka = jnp.concatenate([k_ref[...], kext_ref[...]], axis=1)
        va = jnp.concatenate([v_ref[...], one], axis=1)
        dtile = ((qi - kj) * bq).astype(F32)
        s_all = lax.dot_general(qa_ref[...], ka, NT_DIMS, preferred_element_type=F32)
        blocks = ((0, bq // 2, bq // 2), (bq // 2, bq // 2, bq)) if diag else ((0, bq, bq),)
        for g in range(NSA_GROUP):
            c = 0.0 if diag else -c_ref[hkv * NSA_GROUP + g] * dtile
            for r0, nr, nk in blocks:
                rows = pl.ds(g * bq + r0, nr)
                s = s_all[g * bq + r0:g * bq + r0 + nr, :nk] + maskb[r0:r0 + nr, :nk]
                pr, alpha = _flash_update(s, c, m_ref, rows)
                pv = jnp.dot(pr.astype(BF16), va[:nk], preferred_element_type=F32)
                acc_ref[rows, :] = _lanes(alpha, 2 * HEAD_DIM) * acc_ref[rows, :] + pv

    @pl.when(live & (kj < qi))
    def _():
        step(False)

    @pl.when(live & (kj == qi))
    def _():
        step(True)
        for g in range(NSA_GROUP):
            rows = pl.ds(g * bq, bq)
            o_ref[:, g * HEAD_DIM:(g + 1) * HEAD_DIM] = (
                acc_ref[rows, :HEAD_DIM] / acc_ref[rows, HEAD_DIM:])


def _expand_aug(s, bk):
    key_blk = np.arange(s) // SEL_BLOCK
    e = np.zeros((2 * LANES, s), np.float32)
    e[key_blk, np.arange(s)] = MASK_BIG
    e[LANES, :] = -MASK_BIG
    return jnp.asarray(e.reshape(2 * LANES, s // bk, bk).transpose(1, 0, 2), BF16)


def _sel_schedule(any_rows, s, bq):
    nq = s // bq
    per_tile = bq // SEL_BLOCK
    a = any_rows.reshape(NSA_KV_HEADS, nq, -1, LANES).max(axis=2)
    a = a[:, :, :nq * per_tile].reshape(NSA_KV_HEADS, nq, nq, per_tile).max(axis=-1) > 0.0
    tq, tk = _tri_pairs(nq)
    act = a[:, tq, tk] | (tk == 0)[None, :] | (tk == tq)[None, :]
    order = jnp.argsort(jnp.logical_not(act), axis=1, stable=True)
    nact = jnp.sum(act, axis=1).astype(jnp.int32)
    pos = jnp.minimum(jnp.arange(tq.shape[0], dtype=jnp.int32)[None, :], nact[:, None] - 1)
    idx = jnp.take_along_axis(order, pos, axis=1)
    return tq[idx].reshape(-1), tk[idx].reshape(-1), nact


def _sel_attention(proj, sel, any_rows, cvals, qext, bq):
    s = proj.shape[1]
    qi, kj, nact = _sel_schedule(any_rows, s, bq)
    n_pairs = qi.shape[0] // NSA_KV_HEADS
    gw = NSA_GROUP * HEAD_DIM
    d = HEAD_DIM
    const2 = lambda h, p, qi, kj, na, c: (0, 0)
    grid_spec = pltpu.PrefetchScalarGridSpec(
        num_scalar_prefetch=4,
        grid=(NSA_KV_HEADS, n_pairs),
        in_specs=[
            pl.BlockSpec((NSA_GROUP, bq, d),
                         lambda h, p, qi, kj, na, c: (OFF_NQ // gw + h, qi[h * n_pairs + p], 0)),
            pl.BlockSpec((None, bq, d),
                         lambda h, p, qi, kj, na, c: (OFF_KS // d + h, kj[h * n_pairs + p], 0)),
            pl.BlockSpec((None, bq, d),
                         lambda h, p, qi, kj, na, c: (OFF_VS // d + h, kj[h * n_pairs + p], 0)),
            pl.BlockSpec((None, bq, LANES), lambda h, p, qi, kj, na, c: (h, qi[h * n_pairs + p], 0)),
            pl.BlockSpec((None, 2 * LANES, bq), lambda h, p, qi, kj, na, c: (kj[h * n_pairs + p], 0, 0)),
            pl.BlockSpec((None, NSA_GROUP, 1, LANES), lambda h, p, qi, kj, na, c: (h, 0, 0, 0)),
            pl.BlockSpec((bq, LANES), const2),
            pl.BlockSpec((bq, LANES), const2),
            pl.BlockSpec((bq, bq), const2),
        ],
        out_specs=pl.BlockSpec((bq, gw), lambda h, p, qi, kj, na, c: (qi[h * n_pairs + p], h)),
        scratch_shapes=[pltpu.VMEM((NSA_GROUP * bq, 2 * d), BF16),
                        pltpu.VMEM((NSA_GROUP * bq, LANES), F32),
                        pltpu.VMEM((NSA_GROUP * bq, 2 * d), F32)],
    )
    return pl.pallas_call(
        functools.partial(_sel_kernel, bq=bq, n_pairs=n_pairs),
        grid_spec=grid_spec,
        out_shape=jax.ShapeDtypeStruct((s, NSA_WIDTH), F32),
        compiler_params=_params(2),
        name="nsa_selected",
    )(qi, kj, nact, cvals, proj, proj, proj, sel, _expand_aug(s, bq),
      qext.reshape(NSA_KV_HEADS, NSA_GROUP, 1, LANES), _kpos_ext(bq), _ones_block(bq), _causal_tile(bq))


def _win_kernel(c_ref, q_ref, kp_ref, kc_ref, vp_ref, vc_ref, one_ref, oc_ref, os_ref, gate_ref, o_ref,
                bias_ref):
    hkv = pl.program_id(0)
    i = pl.program_id(1)
    bq = WINDOW

    @pl.when(i == 0)
    def _():
        rel = (lax.broadcasted_iota(jnp.int32, (bq, bq), 0)
               - lax.broadcasted_iota(jnp.int32, (bq, bq), 1))
        relf = rel.astype(F32)
        for g in range(NSA_GROUP):
            c = c_ref[hkv * NSA_GROUP + g]
            bias_ref[g, 0] = jnp.where(rel < 0, -c * (relf + WINDOW), NEG)
            bias_ref[g, 1] = jnp.where(rel >= 0, -c * relf, NEG)

    q = q_ref[...].reshape(NSA_GROUP * bq, HEAD_DIM)
    sp_all = lax.dot_general(q, kp_ref[...], NT_DIMS, preferred_element_type=F32)
    sc_all = lax.dot_general(q, kc_ref[...], NT_DIMS, preferred_element_type=F32)
    first = jnp.where(i == 0, NEG, 0.0)
    one = one_ref[...]
    vp = jnp.concatenate([vp_ref[...], one], axis=1)
    vc = jnp.concatenate([vc_ref[...], one], axis=1)
    gates = gate_ref[...]
    for g in range(NSA_GROUP):
        rows = slice(g * bq, (g + 1) * bq)
        cols = slice(g * HEAD_DIM, (g + 1) * HEAD_DIM)
        sp = sp_all[rows] + bias_ref[g, 0] + first
        sc = sc_all[rows] + bias_ref[g, 1]
        mx = jnp.maximum(jnp.max(sp, axis=1, keepdims=True), jnp.max(sc, axis=1, keepdims=True))
        pp = jnp.exp2(sp - mx).astype(BF16)
        pc = jnp.exp2(sc - mx).astype(BF16)
        pv = (jnp.dot(pp, vp, preferred_element_type=F32)
              + jnp.dot(pc, vc, preferred_element_type=F32))
        ow = pv[:, :HEAD_DIM] / pv[:, HEAD_DIM:]
        gc = gates[:, 3 * g + 0:3 * g + 1]
        gs = gates[:, 3 * g + 1:3 * g + 2]
        gw = gates[:, 3 * g + 2:3 * g + 3]
        o_ref[:, cols] = (gc * oc_ref[:, cols] + gs * os_ref[:, cols] + gw * ow).astype(o_ref.dtype)


def _win_combine(proj, o_cmp, o_sel, gates, cvals):
    s = proj.shape[1]
    bq = WINDOW
    gw = NSA_GROUP * HEAD_DIM
    d = HEAD_DIM
    prev = lambda i: jnp.maximum(i - 1, 0)
    grid_spec = pltpu.PrefetchScalarGridSpec(
        num_scalar_prefetch=1,
        grid=(NSA_KV_HEADS, s // bq),
        in_specs=[
            pl.BlockSpec((NSA_GROUP, bq, d), lambda h, i, c: (OFF_NQ // gw + h, i, 0)),
            pl.BlockSpec((None, bq, d), lambda h, i, c: (OFF_KW // d + h, prev(i), 0)),
            pl.BlockSpec((None, bq, d), lambda h, i, c: (OFF_KW // d + h, i, 0)),
            pl.BlockSpec((None, bq, d), lambda h, i, c: (OFF_VW // d + h, prev(i), 0)),
            pl.BlockSpec((None, bq, d), lambda h, i, c: (OFF_VW // d + h, i, 0)),
            pl.BlockSpec((bq, LANES), lambda h, i, c: (0, 0)),
            pl.BlockSpec((bq, gw), lambda h, i, c: (i, h)),
            pl.BlockSpec((bq, gw), lambda h, i, c: (i, h)),
            pl.BlockSpec((bq, LANES), lambda h, i, c: (i, h)),
        ],
        out_specs=pl.BlockSpec((bq, gw), lambda h, i, c: (i, h)),
        scratch_shapes=[pltpu.VMEM((NSA_GROUP, 2, bq, bq), F32)],
    )
    return pl.pallas_call(
        _win_kernel,
        grid_spec=grid_spec,
        out_shape=jax.ShapeDtypeStruct((s, NSA_WIDTH), BF16),
        compiler_params=_params(2),
        name="nsa_window_combine",
    )(cvals, proj, proj, proj, proj, proj, _ones_block(bq), o_cmp, o_sel, gates)


def _ffn(h, norm_g, w_gate, w_up, w_down):
    u = _rmsnorm(h, norm_g, BF16)
    act = _ffn_up(u, w_gate, w_up)
    return _mm_res(act, w_down.astype(BF16), h, 0.5)


def _overlap_matrix(s, rows):
    n_cmp = (s - CMP_BLOCK) // CMP_STRIDE + 1
    n_sel = s // SEL_BLOCK
    cs = np.arange(rows)[:, None] * CMP_STRIDE
    ss = np.arange(LANES)[None, :] * SEL_BLOCK
    ov = (cs <= ss + SEL_BLOCK - 1) & (cs + CMP_BLOCK - 1 >= ss)
    ov &= (np.arange(rows)[:, None] < n_cmp) & (np.arange(LANES)[None, :] < n_sel)
    return jnp.asarray(ov, BF16)


def _query_colscale():
    cs = np.ones((1, OFF_G), np.float32)
    cs[:, OFF_DQ:OFF_DK] = QSCALE
    cs[:, OFF_NQ:OFF_KC] = QSCALE
    return jnp.asarray(cs)


def _layer(h, layer, ffn1_norm, ffn1_w_gate, ffn1_w_up, ffn1_w_down, mix_norm, w_in, gate_bias,
           lambda_q1, lambda_k1, lambda_q2, lambda_k2, diff_norm,
           cmp_pos_k, cmp_w1_k, cmp_w2_k, cmp_pos_v, cmp_w1_v, cmp_w2_v,
           w_out, ffn2_norm, ffn2_w_gate, ffn2_w_up, ffn2_w_down):
    s = h.shape[0]
    assert s % WINDOW == 0 and s // SEL_BLOCK <= LANES and OFF_G % 512 == 0
    h = _ffn(h, ffn1_norm, ffn1_w_gate, ffn1_w_up, ffn1_w_down)

    u = _rmsnorm(h, mix_norm, BF16)
    w_in_t = w_in.T
    proj = _proj_slabs(u, w_in_t, _query_colscale())
    wg = w_in_t[OFF_G:].T.reshape(-1, NSA_KV_HEADS, GATES_PER_KV)
    wg = jnp.pad(wg, ((0, 0), (0, 0), (0, LANES - GATES_PER_KV))).reshape(-1, NSA_KV_HEADS * LANES)
    gb = jnp.pad(gate_bias.astype(F32).reshape(NSA_KV_HEADS, GATES_PER_KV),
                 ((0, 0), (0, LANES - GATES_PER_KV))).reshape(1, NSA_KV_HEADS * LANES)
    gates = _gate_proj(u, wg.astype(BF16), gb)

    lam_rows = jnp.pad(jnp.stack([lambda_q1, lambda_k1, lambda_q2, lambda_k2]).astype(F32),
                       ((0, 4), (0, 0)))
    o_diff = _diff_attention(proj, lam_rows, diff_norm.reshape(1, -1).astype(F32), layer,
                             _pick(s, 1024))

    nsa_c, nsa_qext = _alibi_consts(NSA_HEADS)
    rows = s // CMP_STRIDE
    kv_cmp = proj[OFF_KC // LANES:OFF_KS // LANES].reshape(
        2, NSA_KV_HEADS, rows, CMP_STRIDE * HEAD_DIM)
    pos = jnp.stack([cmp_pos_k, cmp_pos_v]).astype(F32).reshape(2, 1, CMP_BLOCK * HEAD_DIM)
    w1 = jnp.stack([cmp_w1_k, cmp_w1_v]).astype(BF16)
    w2 = jnp.stack([cmp_w2_k, cmp_w2_v]).astype(BF16)
    kvc = _compress(kv_cmp, pos, w1, w2)
    o_cmp, sel, any_rows = _cmp_select(proj, kvc, _overlap_matrix(s, rows), nsa_c, _pick(s, 256))
    o_sel = _sel_attention(proj, sel, any_rows, nsa_c, nsa_qext, _pick(s, 512))
    o_nsa = _win_combine(proj, o_cmp, o_sel, gates, nsa_c)

    wo = w_out.astype(BF16)
    h = _mm2_res(o_diff, o_nsa, wo[:DIFF_WIDTH], wo[DIFF_WIDTH:], h)
    return _ffn(h, ffn2_norm, ffn2_w_gate, ffn2_w_up, ffn2_w_down)


def kernel(x, ffn1_norm, ffn1_w_gate, ffn1_w_up, ffn1_w_down, mix_norm, w_in, gate_bias, lambda_q1, lambda_k1, lambda_q2, lambda_k2, diff_norm, cmp_pos_k, cmp_w1_k, cmp_w2_k, cmp_pos_v, cmp_w1_v, cmp_w2_v, w_out, ffn2_norm, ffn2_w_gate, ffn2_w_up, ffn2_w_down, final_norm):
    b, s, d = x.shape
    per_layer = (ffn1_norm, ffn1_w_gate, ffn1_w_up, ffn1_w_down, mix_norm, w_in, gate_bias,
                 lambda_q1, lambda_k1, lambda_q2, lambda_k2, diff_norm,
                 cmp_pos_k, cmp_w1_k, cmp_w2_k, cmp_pos_v, cmp_w1_v, cmp_w2_v,
                 w_out, ffn2_norm, ffn2_w_gate, ffn2_w_up, ffn2_w_down)
    outs = []
    for bi in range(b):
        h = x.reshape(s, d) if b == 1 else x[bi]
        for layer in range(DEPTH):
            h = _layer(h, layer, *[p[layer] for p in per_layer])
        outs.append(_rmsnorm(h, final_norm, x.dtype))
    return outs[0].reshape(1, s, d) if b == 1 else jnp.stack(outs)
```

```python
import functools
import math

import ml_dtypes
import numpy as np
import jax
import jax.numpy as jnp
from jax import lax
from jax.experimental import pallas as pl
from jax.experimental.pallas import tpu as pltpu

D_MODEL = 4096
DEPTH = 1
HEAD_DIM = 128
DIFF_V_DIM = 2 * HEAD_DIM
DIFF_HEADS = (D_MODEL // 2) // DIFF_V_DIM
DIFF_WIDTH = DIFF_HEADS * DIFF_V_DIM
NSA_HEADS = (D_MODEL - DIFF_WIDTH) // HEAD_DIM
NSA_KV_HEADS = 4
NSA_GROUP = NSA_HEADS // NSA_KV_HEADS
NSA_WIDTH = NSA_HEADS * HEAD_DIM
CMP_BLOCK = 32
CMP_STRIDE = 16
CMP_HIDDEN = 256
SEL_BLOCK = 64
SEL_TOPN = 16
WINDOW = 512
EPS = 1e-6
NEG = -1e30
FORCE_SCORE = 1e4

LANES = 128
VMEM_LIMIT = 56 * 1024 * 1024
LOG2E = 1.4426950408889634
QSCALE = HEAD_DIM ** -0.5 * LOG2E
MASK_BIG = 2.0 ** 100
KPOS_SPLIT = 32
OFF_DQ = 0
OFF_DK = OFF_DQ + DIFF_HEADS * 2 * HEAD_DIM
OFF_DV = OFF_DK + DIFF_HEADS * 2 * HEAD_DIM
OFF_NQ = OFF_DV + DIFF_HEADS * DIFF_V_DIM
OFF_KC = OFF_NQ + NSA_HEADS * HEAD_DIM
OFF_VC = OFF_KC + NSA_KV_HEADS * HEAD_DIM
OFF_KS = OFF_VC + NSA_KV_HEADS * HEAD_DIM
OFF_VS = OFF_KS + NSA_KV_HEADS * HEAD_DIM
OFF_KW = OFF_VS + NSA_KV_HEADS * HEAD_DIM
OFF_VW = OFF_KW + NSA_KV_HEADS * HEAD_DIM
OFF_G = OFF_VW + NSA_KV_HEADS * HEAD_DIM
GATES_PER_KV = 3 * NSA_GROUP

F32 = jnp.float32
BF16 = jnp.bfloat16
NT_DIMS = (((1,), (1,)), ((), ()))


def _params(n_axes):
    return pltpu.CompilerParams(dimension_semantics=("arbitrary",) * n_axes,
                                vmem_limit_bytes=VMEM_LIMIT)


def _pick(n, pref):
    b = min(pref, n)
    while n % b:
        b //= 2
    return b


def _rmsnorm_kernel(x_ref, g_ref, o_ref):
    x = x_ref[...]
    ms = jnp.mean(x * x, axis=-1, keepdims=True)
    o_ref[...] = (x * lax.rsqrt(ms + EPS) * g_ref[...]).astype(o_ref.dtype)


def _rmsnorm(x, g, out_dtype):
    s, d = x.shape
    bm = _pick(s, 256)
    return pl.pallas_call(
        _rmsnorm_kernel,
        grid=(s // bm,),
        in_specs=[pl.BlockSpec((bm, d), lambda i: (i, 0)),
                  pl.BlockSpec((1, d), lambda i: (0, 0))],
        out_specs=pl.BlockSpec((bm, d), lambda i: (i, 0)),
        out_shape=jax.ShapeDtypeStruct((s, d), out_dtype),
        compiler_params=_params(1),
        name="rmsnorm",
    )(x, g.reshape(1, d).astype(F32))


def _ffn_up_kernel(u_ref, wg_ref, wu_ref, o_ref):
    u = u_ref[...]
    g = jnp.dot(u, wg_ref[...].astype(BF16), preferred_element_type=F32)
    up = jnp.dot(u, wu_ref[...].astype(BF16), preferred_element_type=F32)
    o_ref[...] = (g * jax.nn.sigmoid(g) * up).astype(o_ref.dtype)


def _ffn_up(u, wg, wu):
    s, d = u.shape
    f = wg.shape[1]
    bm, bn = _pick(s, 2048), _pick(f, 256)
    return pl.pallas_call(
        _ffn_up_kernel,
        grid=(s // bm, f // bn),
        in_specs=[pl.BlockSpec((bm, d), lambda i, j: (i, 0), pipeline_mode=pl.Buffered(1)),
                  pl.BlockSpec((d, bn), lambda i, j: (0, j)),
                  pl.BlockSpec((d, bn), lambda i, j: (0, j))],
        out_specs=pl.BlockSpec((bm, bn), lambda i, j: (i, j)),
        out_shape=jax.ShapeDtypeStruct((s, f), BF16),
        compiler_params=_params(2),
        name="ffn_up",
    )(u, wg, wu)


def _mm_res_kernel(a_ref, b_ref, r_ref, o_ref, *, alpha):
    o_ref[...] = r_ref[...] + alpha * jnp.dot(a_ref[...], b_ref[...], preferred_element_type=F32)


def _mm_res(a, b, res, alpha):
    s, kdim = a.shape
    n = b.shape[1]
    bm, bn = _pick(s, 512), _pick(n, 512)
    return pl.pallas_call(
        functools.partial(_mm_res_kernel, alpha=alpha),
        grid=(s // bm, n // bn),
        in_specs=[pl.BlockSpec((bm, kdim), lambda i, j: (i, 0)),
                  pl.BlockSpec((kdim, bn), lambda i, j: (0, j)),
                  pl.BlockSpec((bm, bn), lambda i, j: (i, j))],
        out_specs=pl.BlockSpec((bm, bn), lambda i, j: (i, j)),
        out_shape=jax.ShapeDtypeStruct((s, n), F32),
        compiler_params=_params(2),
        name="mm_res",
    )(a, b, res)


def _proj_kernel(a_ref, bt_ref, cs_ref, o_ref):
    acc = lax.dot_general(a_ref[...], bt_ref[...].astype(BF16), NT_DIMS,
                          preferred_element_type=F32) * cs_ref[...]
    for t in range(o_ref.shape[0]):
        o_ref[t] = acc[:, t * LANES:(t + 1) * LANES].astype(o_ref.dtype)


def _proj_slabs(a, bt, colscale):
    s, kdim = a.shape
    n = colscale.shape[1]
    bm, bn = _pick(s, 2048), _pick(n, 512)
    return pl.pallas_call(
        _proj_kernel,
        grid=(s // bm, n // bn),
        in_specs=[pl.BlockSpec((bm, kdim), lambda i, j: (i, 0), pipeline_mode=pl.Buffered(1)),
                  pl.BlockSpec((bn, kdim), lambda i, j: (j, 0)),
                  pl.BlockSpec((1, bn), lambda i, j: (0, j))],
        out_specs=pl.BlockSpec((bn // LANES, bm, LANES), lambda i, j: (j, i, 0)),
        out_shape=jax.ShapeDtypeStruct((n // LANES, s, LANES), BF16),
        compiler_params=_params(2),
        name="proj_slabs",
    )(a, bt, colscale)


def _gate_kernel(a_ref, b_ref, bias_ref, o_ref):
    z = jnp.dot(a_ref[...], b_ref[...], preferred_element_type=F32) + bias_ref[...]
    o_ref[...] = jax.nn.sigmoid(z)


def _gate_proj(u, wg, bias):
    s, kdim = u.shape
    n = wg.shape[1]
    bm = _pick(s, 1024)
    return pl.pallas_call(
        _gate_kernel,
        grid=(s // bm,),
        in_specs=[pl.BlockSpec((bm, kdim), lambda i: (i, 0)),
                  pl.BlockSpec((kdim, n), lambda i: (0, 0)),
                  pl.BlockSpec((1, n), lambda i: (0, 0))],
        out_specs=pl.BlockSpec((bm, n), lambda i: (i, 0)),
        out_shape=jax.ShapeDtypeStruct((s, n), F32),
        compiler_params=_params(1),
        name="gate_proj",
    )(u, wg, bias)


def _mm2_res_kernel(a1_ref, a2_ref, b1_ref, b2_ref, r_ref, o_ref):
    acc = jnp.dot(a1_ref[...], b1_ref[...], preferred_element_type=F32)
    acc += jnp.dot(a2_ref[...], b2_ref[...], preferred_element_type=F32)
    o_ref[...] = r_ref[...] + acc


def _mm2_res(a1, a2, b1, b2, res):
    s, k1 = a1.shape
    k2 = a2.shape[1]
    n = b1.shape[1]
    bm, bn = _pick(s, 1024), _pick(n, 512)
    return pl.pallas_call(
        _mm2_res_kernel,
        grid=(s // bm, n // bn),
        in_specs=[pl.BlockSpec((bm, k1), lambda i, j: (i, 0)),
                  pl.BlockSpec((bm, k2), lambda i, j: (i, 0)),
                  pl.BlockSpec((k1, bn), lambda i, j: (0, j)),
                  pl.BlockSpec((k2, bn), lambda i, j: (0, j)),
                  pl.BlockSpec((bm, bn), lambda i, j: (i, j))],
        out_specs=pl.BlockSpec((bm, bn), lambda i, j: (i, j)),
        out_shape=jax.ShapeDtypeStruct((s, n), F32),
        compiler_params=_params(2),
        name="mm2_res",
    )(a1, a2, b1, b2, res)


def _tri_pairs(n):
    qi = np.repeat(np.arange(n), np.arange(1, n + 1))
    kj = np.concatenate([np.arange(i + 1) for i in range(n)])
    return jnp.asarray(qi, jnp.int32), jnp.asarray(kj, jnp.int32)


def _bf16_round(x):
    return np.asarray(x, np.float32).astype(ml_dtypes.bfloat16).astype(np.float64)


def _alibi_consts(n_heads):
    c = np.float32(2.0 ** (-8.0 * np.arange(1, n_heads + 1) / n_heads) * LOG2E).astype(np.float64)
    hi = _bf16_round(c)
    mid = _bf16_round(c - hi)
    lo = _bf16_round(c - hi - mid)
    rows = np.zeros((n_heads, 1, LANES), np.float32)
    for col, piece in enumerate((hi, hi, mid, mid, lo, lo)):
        rows[:, 0, col] = piece
    return jnp.asarray(c, F32), jnp.asarray(rows, BF16)


def _kpos_ext(bk):
    pos = np.arange(bk)
    a = (pos // KPOS_SPLIT) * KPOS_SPLIT
    b = pos % KPOS_SPLIT
    ext = np.zeros((bk, LANES), np.float32)
    for col in range(0, 6, 2):
        ext[:, col] = a
        ext[:, col + 1] = b
    return jnp.asarray(ext, BF16)


def _ones_block(rows):
    return jnp.ones((rows, LANES), BF16)


def _causal_tile(b):
    r = np.arange(b)
    return jnp.asarray(np.where(r[:, None] >= r[None, :], 0.0, NEG), F32)


def _lanes(x, width):
    if width == LANES:
        return x
    return jnp.tile(x, (1, width // LANES))


def _flash_update(s, c, m_ref, rows):
    m_prev = m_ref[rows, :]
    m_cur = jnp.max(s, axis=1, keepdims=True) + c
    m_next = jnp.maximum(m_prev, m_cur)
    m_ref[rows, :] = m_next
    p = jnp.exp2(s - _lanes(m_next - c, s.shape[1]))
    return p, jnp.exp2(m_prev - m_next)


def _diff_kernel(qi_ref, kj_ref, c_ref, q_ref, k_ref, v_ref, qext_ref, kext_ref, one_ref, causal_ref,
                 lam_ref, g_ref, o_ref, qa_ref, m_ref, acc_ref, *, bq, lam0):
    h = pl.program_id(0)
    p = pl.program_id(1)
    qi = qi_ref[p]
    kj = kj_ref[p]
    acc_w = DIFF_V_DIM + LANES

    @pl.when(kj == 0)
    def _():
        m_ref[...] = jnp.full_like(m_ref, NEG)
        acc_ref[...] = jnp.zeros_like(acc_ref)
        ext = jnp.broadcast_to(qext_ref[...], (bq, LANES))
        for mp in range(2):
            qa_ref[mp, :, :HEAD_DIM] = q_ref[mp]
            qa_ref[mp, :, HEAD_DIM:] = ext

    def step(diag):
        c = 0.0 if diag else -c_ref[h] * ((qi - kj) * bq).astype(F32)
        va = jnp.concatenate([v_ref[0], v_ref[1], one_ref[...]], axis=1)
        kext = kext_ref[...]
        blocks = ((0, bq // 2, bq // 2), (bq // 2, bq // 2, bq)) if diag else ((0, bq, bq),)
        for mp in range(2):
            ka = jnp.concatenate([k_ref[mp], kext], axis=1)
            for r0, nr, nk in blocks:
                s = lax.dot_general(qa_ref[mp, r0:r0 + nr, :], ka[:nk], NT_DIMS,
                                    preferred_element_type=F32)
                if diag:
                    s = s + causal_ref[r0:r0 + nr, :nk]
                rows = pl.ds(mp * bq + r0, nr)
                pr, alpha = _flash_update(s, c, m_ref, rows)
                pv = jnp.dot(pr.astype(BF16), va[:nk], preferred_element_type=F32)
                acc_ref[rows, :] = _lanes(alpha, acc_w) * acc_ref[rows, :] + pv

    @pl.when(kj < qi)
    def _():
        step(False)

    @pl.when(kj == qi)
    def _():
        step(True)
        lam_rows = lam_ref[...]
        d1 = jnp.sum(lam_rows[0:1] * lam_rows[1:2], axis=1, keepdims=True)
        d2 = jnp.sum(lam_rows[2:3] * lam_rows[3:4], axis=1, keepdims=True)
        lam = jnp.exp(d1) - jnp.exp(d2) + lam0
        o1 = acc_ref[0:bq, :DIFF_V_DIM] / _lanes(acc_ref[0:bq, DIFF_V_DIM:], DIFF_V_DIM)
        o2 = acc_ref[bq:2 * bq, :DIFF_V_DIM] / _lanes(acc_ref[bq:2 * bq, DIFF_V_DIM:], DIFF_V_DIM)
        o = o1 - lam * o2
        ms = jnp.mean(o * o, axis=-1, keepdims=True)
        y = o * lax.rsqrt(ms + EPS) * g_ref[...]
        o_ref[...] = (y * (1.0 - lam0)).astype(o_ref.dtype)


def _diff_attention(proj, lam_rows, gain, layer, bq):
    s = proj.shape[1]
    qi, kj = _tri_pairs(s // bq)
    cvals, qext = _alibi_consts(DIFF_HEADS)
    lam0 = 0.8 - 0.6 * math.exp(-0.3 * layer)
    w = DIFF_V_DIM
    const2 = lambda h, p, qi, kj, c: (0, 0)
    grid_spec = pltpu.PrefetchScalarGridSpec(
        num_scalar_prefetch=3,
        grid=(DIFF_HEADS, qi.shape[0]),
        in_specs=[
            pl.BlockSpec((2, bq, LANES), lambda h, p, qi, kj, c: (OFF_DQ // w + h, qi[p], 0)),
            pl.BlockSpec((2, bq, LANES), lambda h, p, qi, kj, c: (OFF_DK // w + h, kj[p], 0)),
            pl.BlockSpec((2, bq, LANES), lambda h, p, qi, kj, c: (OFF_DV // w + h, kj[p], 0)),
            pl.BlockSpec((None, 1, LANES), lambda h, p, qi, kj, c: (h, 0, 0)),
            pl.BlockSpec((bq, LANES), const2),
            pl.BlockSpec((bq, LANES), const2),
            pl.BlockSpec((bq, bq), const2),
            pl.BlockSpec((8, HEAD_DIM), const2),
            pl.BlockSpec((1, w), const2),
        ],
        out_specs=pl.BlockSpec((bq, w), lambda h, p, qi, kj, c: (qi[p], h)),
        scratch_shapes=[pltpu.VMEM((2, bq, 2 * HEAD_DIM), BF16),
                        pltpu.VMEM((2 * bq, LANES), F32),
                        pltpu.VMEM((2 * bq, w + LANES), F32)],
    )
    return pl.pallas_call(
        functools.partial(_diff_kernel, bq=bq, lam0=lam0),
        grid_spec=grid_spec,
        out_shape=jax.ShapeDtypeStruct((s, DIFF_WIDTH), BF16),
        compiler_params=_params(2),
        name="diff_attention",
    )(qi, kj, cvals, proj, proj, proj, qext, _kpos_ext(bq), _ones_block(bq), _causal_tile(bq),
      lam_rows, gain)


def _compress_kernel(a_ref, pos_ref, w1_ref, w2_ref, o_ref):
    half = CMP_STRIDE * HEAD_DIM
    a = a_ref[...].astype(F32)
    pos = pos_ref[...]
    top = (a + pos[:, :half]).astype(BF16)
    bot = (a + pos[:, half:]).astype(BF16)
    t = jnp.dot(top, w1_ref[:half, :], preferred_element_type=F32)
    b = jnp.dot(bot, w1_ref[half:, :], preferred_element_type=F32)
    rows = a.shape[0]
    hid = t + pltpu.roll(b, rows - 1, 0)
    act = jax.nn.gelu(hid)
    o_ref[...] = jnp.dot(act.astype(BF16), w2_ref[...], preferred_element_type=F32).astype(o_ref.dtype)


def _compress(a, pos, w1, w2):
    _, hkv, rows, width = a.shape
    return pl.pallas_call(
        _compress_kernel,
        grid=(2, hkv),
        in_specs=[pl.BlockSpec((None, None, rows, width), lambda t, h: (t, h, 0, 0)),
                  pl.BlockSpec((None, 1, 2 * width), lambda t, h: (t, 0, 0)),
                  pl.BlockSpec((None, 2 * width, CMP_HIDDEN), lambda t, h: (t, 0, 0)),
                  pl.BlockSpec((None, CMP_HIDDEN, HEAD_DIM), lambda t, h: (t, 0, 0))],
        out_specs=pl.BlockSpec((None, None, rows, HEAD_DIM), lambda t, h: (t, h, 0, 0)),
        out_shape=jax.ShapeDtypeStruct((2, hkv, rows, HEAD_DIM), BF16),
        compiler_params=_params(2),
        name="nsa_compress",
    )(a, pos, w1, w2)


def _split3(x):
    hi = x.astype(BF16)
    r1 = x - hi.astype(F32)
    mid = r1.astype(BF16)
    lo = (r1 - mid.astype(F32)).astype(BF16)
    return hi, mid, lo


def _cmp_select_kernel(c_ref, q_ref, kc_ref, vc_ref, ov_ref, oc_ref, sel_ref, any_ref,
                       *, bq, n_cmp, topn):
    hkv = pl.program_id(0)
    i = pl.program_id(1)
    ncp = kc_ref.shape[0]
    q = q_ref[...].reshape(NSA_GROUP * bq, HEAD_DIM)
    s = lax.dot_general(q, kc_ref[...], NT_DIMS, preferred_element_type=F32)
    t = i * bq + lax.broadcasted_iota(jnp.int32, (bq, ncp), 0)
    cidx = lax.broadcasted_iota(jnp.int32, (bq, ncp), 1)
    dist = t - (cidx * CMP_STRIDE + CMP_BLOCK - 1)
    ok = (dist >= 0) & (cidx < n_cmp)
    distf = dist.astype(F32)
    vc = vc_ref[...]
    psum = jnp.zeros((bq, ncp), F32)
    for g in range(NSA_GROUP):
        sg = jnp.where(ok, s[g * bq:(g + 1) * bq] - c_ref[hkv * NSA_GROUP + g] * distf, NEG)
        mx = jnp.max(sg, axis=1, keepdims=True)
        e = jnp.where(ok, jnp.exp2(sg - mx), 0.0)
        den = jnp.sum(e, axis=1, keepdims=True)
        pg = e / jnp.where(den > 0.0, den, 1.0)
        oc_ref[:, g * HEAD_DIM:(g + 1) * HEAD_DIM] = jnp.dot(
            pg.astype(BF16), vc, preferred_element_type=F32)
        psum = psum + pg
    ov = ov_ref[...]
    imp = jnp.zeros((bq, LANES), F32)
    for piece in _split3(psum):
        imp = imp + jnp.dot(piece, ov, preferred_element_type=F32)
    imp_t = imp.T
    tq = i * bq + lax.broadcasted_iota(jnp.int32, (LANES, bq), 1)
    blk = lax.broadcasted_iota(jnp.int32, (LANES, bq), 0)
    cur = lax.shift_right_arithmetic(tq, SEL_BLOCK.bit_length() - 1)
    forced = (blk == 0) | (blk == cur) | (blk == cur - 1)
    score = jnp.where(forced, FORCE_SCORE, jnp.where(blk <= cur, imp_t, -1.0))
    removed = -3.0e38
    blkf = blk.astype(F32)

    def pick(_, sc):
        mx = jnp.max(sc, axis=0, keepdims=True)
        first = jnp.min(jnp.where(sc == mx, blkf, float(LANES)), axis=0, keepdims=True)
        return jnp.where(blkf == first, removed, sc)

    picked = lax.fori_loop(0, topn, pick, score)
    sel = jnp.where(picked == removed, 1.0, 0.0).T
    sel_ref[...] = sel.astype(sel_ref.dtype)
    any_ref[...] = jnp.max(sel, axis=0, keepdims=True)


def _cmp_select(proj, kvc, overlap, cvals, bq):
    s = proj.shape[1]
    n_cmp = (s - CMP_BLOCK) // CMP_STRIDE + 1
    n_sel = s // SEL_BLOCK
    topn = min(SEL_TOPN, n_sel)
    ncp = kvc.shape[2]
    gw = NSA_GROUP * HEAD_DIM
    grid_spec = pltpu.PrefetchScalarGridSpec(
        num_scalar_prefetch=1,
        grid=(NSA_KV_HEADS, s // bq),
        in_specs=[
            pl.BlockSpec((NSA_GROUP, bq, LANES), lambda h, i, c: (OFF_NQ // gw + h, i, 0)),
            pl.BlockSpec((None, None, ncp, HEAD_DIM), lambda h, i, c: (0, h, 0, 0)),
            pl.BlockSpec((None, None, ncp, HEAD_DIM), lambda h, i, c: (1, h, 0, 0)),
            pl.BlockSpec((ncp, LANES), lambda h, i, c: (0, 0)),
        ],
        out_specs=[pl.BlockSpec((bq, gw), lambda h, i, c: (i, h)),
                   pl.BlockSpec((None, bq, LANES), lambda h, i, c: (h, i, 0)),
                   pl.BlockSpec((None, None, 1, LANES), lambda h, i, c: (h, i, 0, 0))],
    )
    return pl.pallas_call(
        functools.partial(_cmp_select_kernel, bq=bq, n_cmp=n_cmp, topn=topn),
        grid_spec=grid_spec,
        out_shape=[jax.ShapeDtypeStruct((s, NSA_WIDTH), F32),
                   jax.ShapeDtypeStruct((NSA_KV_HEADS, s, LANES), BF16),
                   jax.ShapeDtypeStruct((NSA_KV_HEADS, s // bq, 1, LANES), F32)],
        compiler_params=_params(2),
        name="nsa_cmp_select",
    )(cvals, proj, kvc, kvc, overlap)


def _sel_kernel(qi_ref, kj_ref, nact_ref, c_ref, q_ref, k_ref, v_ref, sel_ref, exp_ref, qext_ref,
                kext_ref, one_ref, causal_ref, o_ref, qa_ref, m_ref, acc_ref, *, bq, n_pairs):
    hkv = pl.program_id(0)
    p = pl.program_id(1)
    qi = qi_ref[hkv * n_pairs + p]
    kj = kj_ref[hkv * n_pairs + p]
    live = p < nact_ref[hkv]

    @pl.when(live & (kj == 0))
    def _():
        m_ref[...] = jnp.full_like(m_ref, NEG)
        acc_ref[...] = jnp.zeros_like(acc_ref)
        for g in range(NSA_GROUP):
            qa_ref[g * bq:(g + 1) * bq, :HEAD_DIM] = q_ref[g]
            qa_ref[g * bq:(g + 1) * bq, HEAD_DIM:] = jnp.broadcast_to(qext_ref[g], (bq, LANES))

    def step(diag):
        one = one_ref[...]
        maskb = jnp.dot(jnp.concatenate([sel_ref[...], one], axis=1), exp_ref[...],
                        preferred_element_type=F32)
        if diag:
            maskb = maskb + causal_ref[...]
        ka = jnp.concatenate([k_ref[...], kext_ref[...]], axis=1)
        va = jnp.concatenate([v_ref[...], one], axis=1)
        dtile = ((qi - kj) * bq).astype(F32)
        s_all = lax.dot_general(qa_ref[...], ka, NT_DIMS, preferred_element_type=F32)
        blocks = ((0, bq // 2, bq // 2), (bq // 2, bq // 2, bq)) if diag else ((0, bq, bq),)
        for g in range(NSA_GROUP):
            c = 0.0 if diag else -c_ref[hkv * NSA_GROUP + g] * dtile
            for r0, nr, nk in blocks:
                rows = pl.ds(g * bq + r0, nr)
                s = s_all[g * bq + r0:g * bq + r0 + nr, :nk] + maskb[r0:r0 + nr, :nk]
                pr, alpha = _flash_update(s, c, m_ref, rows)
                pv = jnp.dot(pr.astype(BF16), va[:nk], preferred_element_type=F32)
                acc_ref[rows, :] = _lanes(alpha, 2 * HEAD_DIM) * acc_ref[rows, :] + pv

    @pl.when(live & (kj < qi))
    def _():
        step(False)

    @pl.when(live & (kj == qi))
    def _():
        step(True)
        for g in range(NSA_GROUP):
            rows = pl.ds(g * bq, bq)
            o_ref[:, g * HEAD_DIM:(g + 1) * HEAD_DIM] = (
                acc_ref[rows, :HEAD_DIM] / acc_ref[rows, HEAD_DIM:])


def _expand_aug(s, bk):
    key_blk = np.arange(s) // SEL_BLOCK
    e = np.zeros((2 * LANES, s), np.float32)
    e[key_blk, np.arange(s)] = MASK_BIG
    e[LANES, :] = -MASK_BIG
    return jnp.asarray(e.reshape(2 * LANES, s // bk, bk).transpose(1, 0, 2), BF16)


def _sel_schedule(any_rows, s, bq):
    nq = s // bq
    per_tile = bq // SEL_BLOCK
    a = any_rows.reshape(NSA_KV_HEADS, nq, -1, LANES).max(axis=2)
    a = a[:, :, :nq * per_tile].reshape(NSA_KV_HEADS, nq, nq, per_tile).max(axis=-1) > 0.0
    tq, tk = _tri_pairs(nq)
    act = a[:, tq, tk] | (tk == 0)[None, :] | (tk == tq)[None, :]
    order = jnp.argsort(jnp.logical_not(act), axis=1, stable=True)
    nact = jnp.sum(act, axis=1).astype(jnp.int32)
    pos = jnp.minimum(jnp.arange(tq.shape[0], dtype=jnp.int32)[None, :], nact[:, None] - 1)
    idx = jnp.take_along_axis(order, pos, axis=1)
    return tq[idx].reshape(-1), tk[idx].reshape(-1), nact


def _sel_attention(proj, sel, any_rows, cvals, qext, bq):
    s = proj.shape[1]
    qi, kj, nact = _sel_schedule(any_rows, s, bq)
    n_pairs = qi.shape[0] // NSA_KV_HEADS
    gw = NSA_GROUP * HEAD_DIM
    d = HEAD_DIM
    const2 = lambda h, p, qi, kj, na, c: (0, 0)
    grid_spec = pltpu.PrefetchScalarGridSpec(
        num_scalar_prefetch=4,
        grid=(NSA_KV_HEADS, n_pairs),
        in_specs=[
            pl.BlockSpec((NSA_GROUP, bq, d),
                         lambda h, p, qi, kj, na, c: (OFF_NQ // gw + h, qi[h * n_pairs + p], 0)),
            pl.BlockSpec((None, bq, d),
                         lambda h, p, qi, kj, na, c: (OFF_KS // d + h, kj[h * n_pairs + p], 0)),
            pl.BlockSpec((None, bq, d),
                         lambda h, p, qi, kj, na, c: (OFF_VS // d + h, kj[h * n_pairs + p], 0)),
            pl.BlockSpec((None, bq, LANES), lambda h, p, qi, kj, na, c: (h, qi[h * n_pairs + p], 0)),
            pl.BlockSpec((None, 2 * LANES, bq), lambda h, p, qi, kj, na, c: (kj[h * n_pairs + p], 0, 0)),
            pl.BlockSpec((None, NSA_GROUP, 1, LANES), lambda h, p, qi, kj, na, c: (h, 0, 0, 0)),
            pl.BlockSpec((bq, LANES), const2),
            pl.BlockSpec((bq, LANES), const2),
            pl.BlockSpec((bq, bq), const2),
        ],
        out_specs=pl.BlockSpec((bq, gw), lambda h, p, qi, kj, na, c: (qi[h * n_pairs + p], h)),
        scratch_shapes=[pltpu.VMEM((NSA_GROUP * bq, 2 * d), BF16),
                        pltpu.VMEM((NSA_GROUP * bq, LANES), F32),
                        pltpu.VMEM((NSA_GROUP * bq, 2 * d), F32)],
    )
    return pl.pallas_call(
        functools.partial(_sel_kernel, bq=bq, n_pairs=n_pairs),
        grid_spec=grid_spec,
        out_shape=jax.ShapeDtypeStruct((s, NSA_WIDTH), F32),
        compiler_params=_params(2),
        name="nsa_selected",
    )(qi, kj, nact, cvals, proj, proj, proj, sel, _expand_aug(s, bq),
      qext.reshape(NSA_KV_HEADS, NSA_GROUP, 1, LANES), _kpos_ext(bq), _ones_block(bq), _causal_tile(bq))


def _win_kernel(c_ref, q_ref, kp_ref, kc_ref, vp_ref, vc_ref, one_ref, oc_ref, os_ref, gate_ref, o_ref,
                bias_ref):
    hkv = pl.program_id(0)
    i = pl.program_id(1)
    bq = WINDOW
    half = bq // 2
    band = bq + half

    @pl.when(i == 0)
    def _():
        r = lax.broadcasted_iota(jnp.int32, (half, band), 0)
        b = lax.broadcasted_iota(jnp.int32, (half, band), 1)
        d = WINDOW + r - b
        inside = (d >= 0) & (d < WINDOW)
        df = d.astype(F32)
        for g in range(NSA_GROUP):
            bias = jnp.where(inside, -c_ref[hkv * NSA_GROUP + g] * df, NEG)
            bias_ref[g, 0] = bias
            for hh in range(2):
                bias_ref[g, 1 + hh] = jnp.where(b < bq - hh * half, NEG, bias)

    kband = jnp.concatenate([kp_ref[...], kc_ref[...]], axis=0)
    vband = jnp.concatenate([vp_ref[...], vc_ref[...]], axis=0)
    one = one_ref[...]
    for hh in range(2):
        rows = slice(hh * half, (hh + 1) * half)
        keys = slice(hh * half, hh * half + band)
        which = jnp.where(i == 0, 1 + hh, 0)
        q = q_ref[:, rows, :].reshape(NSA_GROUP * half, HEAD_DIM)
        s_all = lax.dot_general(q, kband[keys], NT_DIMS, preferred_element_type=F32)
        va = jnp.concatenate([vband[keys], one], axis=1)
        ps = []
        for g in range(NSA_GROUP):
            s = s_all[g * half:(g + 1) * half] + bias_ref[g, which]
            ps.append(jnp.exp2(s - jnp.max(s, axis=1, keepdims=True)).astype(BF16))
        pv = jnp.dot(jnp.concatenate(ps, axis=0), va, preferred_element_type=F32)
        gates = gate_ref[rows, :]
        for g in range(NSA_GROUP):
            cols = slice(g * HEAD_DIM, (g + 1) * HEAD_DIM)
            pg = pv[g * half:(g + 1) * half]
            ow = pg[:, :HEAD_DIM] / pg[:, HEAD_DIM:]
            gc = gates[:, 3 * g + 0:3 * g + 1]
            gs = gates[:, 3 * g + 1:3 * g + 2]
            gw = gates[:, 3 * g + 2:3 * g + 3]
            o_ref[rows, cols] = (gc * oc_ref[rows, cols] + gs * os_ref[rows, cols]
                                 + gw * ow).astype(o_ref.dtype)


def _win_combine(proj, o_cmp, o_sel, gates, cvals):
    s = proj.shape[1]
    bq = WINDOW
    band = bq + bq // 2
    gw = NSA_GROUP * HEAD_DIM
    d = HEAD_DIM
    prev = lambda i: jnp.maximum(i - 1, 0)
    grid_spec = pltpu.PrefetchScalarGridSpec(
        num_scalar_prefetch=1,
        grid=(NSA_KV_HEADS, s // bq),
        in_specs=[
            pl.BlockSpec((NSA_GROUP, bq, d), lambda h, i, c: (OFF_NQ // gw + h, i, 0)),
            pl.BlockSpec((None, bq, d), lambda h, i, c: (OFF_KW // d + h, prev(i), 0)),
            pl.BlockSpec((None, bq, d), lambda h, i, c: (OFF_KW // d + h, i, 0)),
            pl.BlockSpec((None, bq, d), lambda h, i, c: (OFF_VW // d + h, prev(i), 0)),
            pl.BlockSpec((None, bq, d), lambda h, i, c: (OFF_VW // d + h, i, 0)),
            pl.BlockSpec((band, LANES), lambda h, i, c: (0, 0)),
            pl.BlockSpec((bq, gw), lambda h, i, c: (i, h)),
            pl.BlockSpec((bq, gw), lambda h, i, c: (i, h)),
            pl.BlockSpec((bq, LANES), lambda h, i, c: (i, h)),
        ],
        out_specs=pl.BlockSpec((bq, gw), lambda h, i, c: (i, h)),
        scratch_shapes=[pltpu.VMEM((NSA_GROUP, 3, bq // 2, band), F32)],
    )
    return pl.pallas_call(
        _win_kernel,
        grid_spec=grid_spec,
        out_shape=jax.ShapeDtypeStruct((s, NSA_WIDTH), BF16),
        compiler_params=_params(2),
        name="nsa_window_combine",
    )(cvals, proj, proj, proj, proj, proj, _ones_block(band), o_cmp, o_sel, gates)


def _ffn(h, norm_g, w_gate, w_up, w_down):
    u = _rmsnorm(h, norm_g, BF16)
    act = _ffn_up(u, w_gate, w_up)
    return _mm_res(act, w_down.astype(BF16), h, 0.5)


def _overlap_matrix(s, rows):
    n_cmp = (s - CMP_BLOCK) // CMP_STRIDE + 1
    n_sel = s // SEL_BLOCK
    cs = np.arange(rows)[:, None] * CMP_STRIDE
    ss = np.arange(LANES)[None, :] * SEL_BLOCK
    ov = (cs <= ss + SEL_BLOCK - 1) & (cs + CMP_BLOCK - 1 >= ss)
    ov &= (np.arange(rows)[:, None] < n_cmp) & (np.arange(LANES)[None, :] < n_sel)
    return jnp.asarray(ov, BF16)


def _query_colscale():
    cs = np.ones((1, OFF_G), np.float32)
    cs[:, OFF_DQ:OFF_DK] = QSCALE
    cs[:, OFF_NQ:OFF_KC] = QSCALE
    return jnp.asarray(cs)


def _layer(h, layer, ffn1_norm, ffn1_w_gate, ffn1_w_up, ffn1_w_down, mix_norm, w_in, gate_bias,
           lambda_q1, lambda_k1, lambda_q2, lambda_k2, diff_norm,
           cmp_pos_k, cmp_w1_k, cmp_w2_k, cmp_pos_v, cmp_w1_v, cmp_w2_v,
           w_out, ffn2_norm, ffn2_w_gate, ffn2_w_up, ffn2_w_down):
    s = h.shape[0]
    assert s % WINDOW == 0 and s // SEL_BLOCK <= LANES and OFF_G % 512 == 0
    h = _ffn(h, ffn1_norm, ffn1_w_gate, ffn1_w_up, ffn1_w_down)

    u = _rmsnorm(h, mix_norm, BF16)
    w_in_t = w_in.T
    proj = _proj_slabs(u, w_in_t, _query_colscale())
    wg = w_in_t[OFF_G:].T.reshape(-1, NSA_KV_HEADS, GATES_PER_KV)
    wg = jnp.pad(wg, ((0, 0), (0, 0), (0, LANES - GATES_PER_KV))).reshape(-1, NSA_KV_HEADS * LANES)
    gb = jnp.pad(gate_bias.astype(F32).reshape(NSA_KV_HEADS, GATES_PER_KV),
                 ((0, 0), (0, LANES - GATES_PER_KV))).reshape(1, NSA_KV_HEADS * LANES)
    gates = _gate_proj(u, wg.astype(BF16), gb)

    lam_rows = jnp.pad(jnp.stack([lambda_q1, lambda_k1, lambda_q2, lambda_k2]).astype(F32),
                       ((0, 4), (0, 0)))
    o_diff = _diff_attention(proj, lam_rows, diff_norm.reshape(1, -1).astype(F32), layer,
                             _pick(s, 1024))

    nsa_c, nsa_qext = _alibi_consts(NSA_HEADS)
    rows = s // CMP_STRIDE
    kv_cmp = proj[OFF_KC // LANES:OFF_KS // LANES].reshape(
        2, NSA_KV_HEADS, rows, CMP_STRIDE * HEAD_DIM)
    pos = jnp.stack([cmp_pos_k, cmp_pos_v]).astype(F32).reshape(2, 1, CMP_BLOCK * HEAD_DIM)
    w1 = jnp.stack([cmp_w1_k, cmp_w1_v]).astype(BF16)
    w2 = jnp.stack([cmp_w2_k, cmp_w2_v]).astype(BF16)
    kvc = _compress(kv_cmp, pos, w1, w2)
    o_cmp, sel, any_rows = _cmp_select(proj, kvc, _overlap_matrix(s, rows), nsa_c, _pick(s, 256))
    o_sel = _sel_attention(proj, sel, any_rows, nsa_c, nsa_qext, _pick(s, 512))
    o_nsa = _win_combine(proj, o_cmp, o_sel, gates, nsa_c)

    wo = w_out.astype(BF16)
    h = _mm2_res(o_diff, o_nsa, wo[:DIFF_WIDTH], wo[DIFF_WIDTH:], h)
    return _ffn(h, ffn2_norm, ffn2_w_gate, ffn2_w_up, ffn2_w_down)


def kernel(x, ffn1_norm, ffn1_w_gate, ffn1_w_up, ffn1_w_down, mix_norm, w_in, gate_bias, lambda_q1, lambda_k1, lambda_q2, lambda_k2, diff_norm, cmp_pos_k, cmp_w1_k, cmp_w2_k, cmp_pos_v, cmp_w1_v, cmp_w2_v, w_out, ffn2_norm, ffn2_w_gate, ffn2_w_up, ffn2_w_down, final_norm):
    b, s, d = x.shape
    per_layer = (ffn1_norm, ffn1_w_gate, ffn1_w_up, ffn1_w_down, mix_norm, w_in, gate_bias,
                 lambda_q1, lambda_k1, lambda_q2, lambda_k2, diff_norm,
                 cmp_pos_k, cmp_w1_k, cmp_w2_k, cmp_pos_v, cmp_w1_v, cmp_w2_v,
                 w_out, ffn2_norm, ffn2_w_gate, ffn2_w_up, ffn2_w_down)
    outs = []
    for bi in range(b):
        h = x.reshape(s, d) if b == 1 else x[bi]
        for layer in range(DEPTH):
            h = _layer(h, layer, *[p[layer] for p in per_layer])
        outs.append(_rmsnorm(h, final_norm, x.dtype))
    return outs[0].reshape(1, s, d) if b == 1 else jnp.stack(outs)
```

```python
import functools
import math

import ml_dtypes
import numpy as np
import jax
import jax.numpy as jnp
from jax import lax
from jax.experimental import pallas as pl
from jax.experimental.pallas import tpu as pltpu

D_MODEL = 4096
DEPTH = 1
HEAD_DIM = 128
DIFF_V_DIM = 2 * HEAD_DIM
DIFF_HEADS = (D_MODEL // 2) // DIFF_V_DIM
DIFF_WIDTH = DIFF_HEADS * DIFF_V_DIM
NSA_HEADS = (D_MODEL - DIFF_WIDTH) // HEAD_DIM
NSA_KV_HEADS = 4
NSA_GROUP = NSA_HEADS // NSA_KV_HEADS
NSA_WIDTH = NSA_HEADS * HEAD_DIM
CMP_BLOCK = 32
CMP_STRIDE = 16
CMP_HIDDEN = 256
SEL_BLOCK = 64
SEL_TOPN = 16
WINDOW = 512
EPS = 1e-6
NEG = -1e30
FORCE_SCORE = 1e4

LANES = 128
VMEM_LIMIT = 56 * 1024 * 1024
LOG2E = 1.4426950408889634
QSCALE = HEAD_DIM ** -0.5 * LOG2E
MASK_BIG = 2.0 ** 100
KPOS_SPLIT = 32
OFF_DQ = 0
OFF_DK = OFF_DQ + DIFF_HEADS * 2 * HEAD_DIM
OFF_DV = OFF_DK + DIFF_HEADS * 2 * HEAD_DIM
OFF_NQ = OFF_DV + DIFF_HEADS * DIFF_V_DIM
OFF_KC = OFF_NQ + NSA_HEADS * HEAD_DIM
OFF_VC = OFF_KC + NSA_KV_HEADS * HEAD_DIM
OFF_KS = OFF_VC + NSA_KV_HEADS * HEAD_DIM
OFF_VS = OFF_KS + NSA_KV_HEADS * HEAD_DIM
OFF_KW = OFF_VS + NSA_KV_HEADS * HEAD_DIM
OFF_VW = OFF_KW + NSA_KV_HEADS * HEAD_DIM
OFF_G = OFF_VW + NSA_KV_HEADS * HEAD_DIM
GATES_PER_KV = 3 * NSA_GROUP

F32 = jnp.float32
BF16 = jnp.bfloat16
NT_DIMS = (((1,), (1,)), ((), ()))


def _params(n_axes):
    return pltpu.CompilerParams(dimension_semantics=("arbitrary",) * n_axes,
                                vmem_limit_bytes=VMEM_LIMIT)


def _pick(n, pref):
    b = min(pref, n)
    while n % b:
        b //= 2
    return b


def _rmsnorm_kernel(x_ref, g_ref, o_ref):
    x = x_ref[...]
    ms = jnp.mean(x * x, axis=-1, keepdims=True)
    o_ref[...] = (x * lax.rsqrt(ms + EPS) * g_ref[...]).astype(o_ref.dtype)


def _rmsnorm(x, g, out_dtype):
    s, d = x.shape
    bm = _pick(s, 256)
    return pl.pallas_call(
        _rmsnorm_kernel,
        grid=(s // bm,),
        in_specs=[pl.BlockSpec((bm, d), lambda i: (i, 0)),
                  pl.BlockSpec((1, d), lambda i: (0, 0))],
        out_specs=pl.BlockSpec((bm, d), lambda i: (i, 0)),
        out_shape=jax.ShapeDtypeStruct((s, d), out_dtype),
        compiler_params=_params(1),
        name="rmsnorm",
    )(x, g.reshape(1, d).astype(F32))


def _ffn_up_kernel(u_ref, wg_ref, wu_ref, wd_ref, o_ref, wdb_ref):
    @pl.when(pl.program_id(0) == 0)
    def _():
        wdb_ref[...] = wd_ref[...].astype(BF16)

    u = u_ref[...]
    g = jnp.dot(u, wg_ref[...].astype(BF16), preferred_element_type=F32)
    up = jnp.dot(u, wu_ref[...].astype(BF16), preferred_element_type=F32)
    o_ref[...] = (g * jax.nn.sigmoid(g) * up).astype(o_ref.dtype)


def _ffn_up(u, wg, wu, wd):
    s, d = u.shape
    f = wg.shape[1]
    bm, bn = _pick(s, 2048), _pick(f, 256)
    nj = f // bn
    wd_block = lambda i, j: (jnp.where(i == 0, j, nj - 1), 0)
    return pl.pallas_call(
        _ffn_up_kernel,
        grid=(s // bm, nj),
        in_specs=[pl.BlockSpec((bm, d), lambda i, j: (i, 0), pipeline_mode=pl.Buffered(1)),
                  pl.BlockSpec((d, bn), lambda i, j: (0, j)),
                  pl.BlockSpec((d, bn), lambda i, j: (0, j)),
                  pl.BlockSpec((bn, wd.shape[1]), wd_block)],
        out_specs=[pl.BlockSpec((bm, bn), lambda i, j: (i, j)),
                   pl.BlockSpec((bn, wd.shape[1]), wd_block)],
        out_shape=[jax.ShapeDtypeStruct((s, f), BF16),
                   jax.ShapeDtypeStruct(wd.shape, BF16)],
        compiler_params=_params(2),
        name="ffn_up",
    )(u, wg, wu, wd)


def _mm_res_kernel(a_ref, b_ref, r_ref, o_ref, *, alpha):
    o_ref[...] = r_ref[...] + alpha * jnp.dot(a_ref[...], b_ref[...], preferred_element_type=F32)


def _mm_res(a, b, res, alpha):
    s, kdim = a.shape
    n = b.shape[1]
    bm, bn = _pick(s, 512), _pick(n, 512)
    return pl.pallas_call(
        functools.partial(_mm_res_kernel, alpha=alpha),
        grid=(s // bm, n // bn),
        in_specs=[pl.BlockSpec((bm, kdim), lambda i, j: (i, 0)),
                  pl.BlockSpec((kdim, bn), lambda i, j: (0, j)),
                  pl.BlockSpec((bm, bn), lambda i, j: (i, j))],
        out_specs=pl.BlockSpec((bm, bn), lambda i, j: (i, j)),
        out_shape=jax.ShapeDtypeStruct((s, n), F32),
        compiler_params=_params(2),
        name="mm_res",
    )(a, b, res)


def _proj_kernel(a_ref, bt_ref, cs_ref, o_ref):
    acc = lax.dot_general(a_ref[...], bt_ref[...].astype(BF16), NT_DIMS,
                          preferred_element_type=F32) * cs_ref[...]
    for t in range(o_ref.shape[0]):
        o_ref[t] = acc[:, t * LANES:(t + 1) * LANES].astype(o_ref.dtype)


def _proj_slabs(a, bt, colscale):
    s, kdim = a.shape
    n = colscale.shape[1]
    bm, bn = _pick(s, 2048), _pick(n, 512)
    return pl.pallas_call(
        _proj_kernel,
        grid=(s // bm, n // bn),
        in_specs=[pl.BlockSpec((bm, kdim), lambda i, j: (i, 0), pipeline_mode=pl.Buffered(1)),
                  pl.BlockSpec((bn, kdim), lambda i, j: (j, 0)),
                  pl.BlockSpec((1, bn), lambda i, j: (0, j))],
        out_specs=pl.BlockSpec((bn // LANES, bm, LANES), lambda i, j: (j, i, 0)),
        out_shape=jax.ShapeDtypeStruct((n // LANES, s, LANES), BF16),
        compiler_params=_params(2),
        name="proj_slabs",
    )(a, bt, colscale)


def _gate_kernel(a_ref, b_ref, bias_ref, o_ref):
    z = jnp.dot(a_ref[...], b_ref[...], preferred_element_type=F32) + bias_ref[...]
    o_ref[...] = jax.nn.sigmoid(z)


def _gate_proj(u, wg, bias):
    s, kdim = u.shape
    n = wg.shape[1]
    bm = _pick(s, 1024)
    return pl.pallas_call(
        _gate_kernel,
        grid=(s // bm,),
        in_specs=[pl.BlockSpec((bm, kdim), lambda i: (i, 0)),
                  pl.BlockSpec((kdim, n), lambda i: (0, 0)),
                  pl.BlockSpec((1, n), lambda i: (0, 0))],
        out_specs=pl.BlockSpec((bm, n), lambda i: (i, 0)),
        out_shape=jax.ShapeDtypeStruct((s, n), F32),
        compiler_params=_params(1),
        name="gate_proj",
    )(u, wg, bias)


def _mm2_res_kernel(a1_ref, a2_ref, b1_ref, b2_ref, r_ref, o_ref):
    acc = jnp.dot(a1_ref[...], b1_ref[...], preferred_element_type=F32)
    acc += jnp.dot(a2_ref[...], b2_ref[...], preferred_element_type=F32)
    o_ref[...] = r_ref[...] + acc


def _mm2_res(a1, a2, b1, b2, res):
    s, k1 = a1.shape
    k2 = a2.shape[1]
    n = b1.shape[1]
    bm, bn = _pick(s, 1024), _pick(n, 512)
    return pl.pallas_call(
        _mm2_res_kernel,
        grid=(s // bm, n // bn),
        in_specs=[pl.BlockSpec((bm, k1), lambda i, j: (i, 0)),
                  pl.BlockSpec((bm, k2), lambda i, j: (i, 0)),
                  pl.BlockSpec((k1, bn), lambda i, j: (0, j)),
                  pl.BlockSpec((k2, bn), lambda i, j: (0, j)),
                  pl.BlockSpec((bm, bn), lambda i, j: (i, j))],
        out_specs=pl.BlockSpec((bm, bn), lambda i, j: (i, j)),
        out_shape=jax.ShapeDtypeStruct((s, n), F32),
        compiler_params=_params(2),
        name="mm2_res",
    )(a1, a2, b1, b2, res)


def _tri_pairs(n):
    qi = np.repeat(np.arange(n), np.arange(1, n + 1))
    kj = np.concatenate([np.arange(i + 1) for i in range(n)])
    return jnp.asarray(qi, jnp.int32), jnp.asarray(kj, jnp.int32)


def _bf16_round(x):
    return np.asarray(x, np.float32).astype(ml_dtypes.bfloat16).astype(np.float64)


def _alibi_consts(n_heads):
    c = np.float32(2.0 ** (-8.0 * np.arange(1, n_heads + 1) / n_heads) * LOG2E).astype(np.float64)
    hi = _bf16_round(c)
    mid = _bf16_round(c - hi)
    lo = _bf16_round(c - hi - mid)
    rows = np.zeros((n_heads, 1, LANES), np.float32)
    for col, piece in enumerate((hi, hi, mid, mid, lo, lo)):
        rows[:, 0, col] = piece
    return jnp.asarray(c, F32), jnp.asarray(rows, BF16)


def _kpos_ext(bk):
    pos = np.arange(bk)
    a = (pos // KPOS_SPLIT) * KPOS_SPLIT
    b = pos % KPOS_SPLIT
    ext = np.zeros((bk, LANES), np.float32)
    for col in range(0, 6, 2):
        ext[:, col] = a
        ext[:, col + 1] = b
    return jnp.asarray(ext, BF16)


def _ones_block(rows):
    return jnp.ones((rows, LANES), BF16)


def _causal_tile(b):
    r = np.arange(b)
    return jnp.asarray(np.where(r[:, None] >= r[None, :], 0.0, NEG), F32)


def _lanes(x, width):
    if width == LANES:
        return x
    return jnp.tile(x, (1, width // LANES))


def _flash_update(s, c, m_ref, rows):
    m_prev = m_ref[rows, :]
    m_cur = jnp.max(s, axis=1, keepdims=True) + c
    m_next = jnp.maximum(m_prev, m_cur)
    m_ref[rows, :] = m_next
    p = jnp.exp2(s - _lanes(m_next - c, s.shape[1]))
    return p, jnp.exp2(m_prev - m_next)


def _diff_kernel(qi_ref, kj_ref, c_ref, q_ref, k_ref, v_ref, qext_ref, kext_ref, one_ref, causal_ref,
                 lam_ref, g_ref, o_ref, qa_ref, m_ref, acc_ref, *, bq, lam0):
    h = pl.program_id(0)
    p = pl.program_id(1)
    qi = qi_ref[p]
    kj = kj_ref[p]
    acc_w = DIFF_V_DIM + LANES

    @pl.when(kj == 0)
    def _():
        m_ref[...] = jnp.full_like(m_ref, NEG)
        acc_ref[...] = jnp.zeros_like(acc_ref)
        ext = jnp.broadcast_to(qext_ref[...], (bq, LANES))
        for mp in range(2):
            qa_ref[mp, :, :HEAD_DIM] = q_ref[mp]
            qa_ref[mp, :, HEAD_DIM:] = ext

    def step(diag):
        c = 0.0 if diag else -c_ref[h] * ((qi - kj) * bq).astype(F32)
        va = jnp.concatenate([v_ref[0], v_ref[1], one_ref[...]], axis=1)
        kext = kext_ref[...]
        blocks = ((0, bq // 2, bq // 2), (bq // 2, bq // 2, bq)) if diag else ((0, bq, bq),)
        for mp in range(2):
            ka = jnp.concatenate([k_ref[mp], kext], axis=1)
            for r0, nr, nk in blocks:
                s = lax.dot_general(qa_ref[mp, r0:r0 + nr, :], ka[:nk], NT_DIMS,
                                    preferred_element_type=F32)
                if diag:
                    s = s + causal_ref[r0:r0 + nr, :nk]
                rows = pl.ds(mp * bq + r0, nr)
                pr, alpha = _flash_update(s, c, m_ref, rows)
                pv = jnp.dot(pr.astype(BF16), va[:nk], preferred_element_type=F32)
                acc_ref[rows, :] = _lanes(alpha, acc_w) * acc_ref[rows, :] + pv

    @pl.when(kj < qi)
    def _():
        step(False)

    @pl.when(kj == qi)
    def _():
        step(True)
        lam_rows = lam_ref[...]
        d1 = jnp.sum(lam_rows[0:1] * lam_rows[1:2], axis=1, keepdims=True)
        d2 = jnp.sum(lam_rows[2:3] * lam_rows[3:4], axis=1, keepdims=True)
        lam = jnp.exp(d1) - jnp.exp(d2) + lam0
        o1 = acc_ref[0:bq, :DIFF_V_DIM] / _lanes(acc_ref[0:bq, DIFF_V_DIM:], DIFF_V_DIM)
        o2 = acc_ref[bq:2 * bq, :DIFF_V_DIM] / _lanes(acc_ref[bq:2 * bq, DIFF_V_DIM:], DIFF_V_DIM)
        o = o1 - lam * o2
        ms = jnp.mean(o * o, axis=-1, keepdims=True)
        y = o * lax.rsqrt(ms + EPS) * g_ref[...]
        o_ref[...] = (y * (1.0 - lam0)).astype(o_ref.dtype)


def _diff_attention(proj, lam_rows, gain, layer, bq):
    s = proj.shape[1]
    qi, kj = _tri_pairs(s // bq)
    cvals, qext = _alibi_consts(DIFF_HEADS)
    lam0 = 0.8 - 0.6 * math.exp(-0.3 * layer)
    w = DIFF_V_DIM
    const2 = lambda h, p, qi, kj, c: (0, 0)
    grid_spec = pltpu.PrefetchScalarGridSpec(
        num_scalar_prefetch=3,
        grid=(DIFF_HEADS, qi.shape[0]),
        in_specs=[
            pl.BlockSpec((2, bq, LANES), lambda h, p, qi, kj, c: (OFF_DQ // w + h, qi[p], 0)),
            pl.BlockSpec((2, bq, LANES), lambda h, p, qi, kj, c: (OFF_DK // w + h, kj[p], 0)),
            pl.BlockSpec((2, bq, LANES), lambda h, p, qi, kj, c: (OFF_DV // w + h, kj[p], 0)),
            pl.BlockSpec((None, 1, LANES), lambda h, p, qi, kj, c: (h, 0, 0)),
            pl.BlockSpec((bq, LANES), const2),
            pl.BlockSpec((bq, LANES), const2),
            pl.BlockSpec((bq, bq), const2),
            pl.BlockSpec((8, HEAD_DIM), const2),
            pl.BlockSpec((1, w), const2),
        ],
        out_specs=pl.BlockSpec((bq, w), lambda h, p, qi, kj, c: (qi[p], h)),
        scratch_shapes=[pltpu.VMEM((2, bq, 2 * HEAD_DIM), BF16),
                        pltpu.VMEM((2 * bq, LANES), F32),
                        pltpu.VMEM((2 * bq, w + LANES), F32)],
    )
    return pl.pallas_call(
        functools.partial(_diff_kernel, bq=bq, lam0=lam0),
        grid_spec=grid_spec,
        out_shape=jax.ShapeDtypeStruct((s, DIFF_WIDTH), BF16),
        compiler_params=_params(2),
        name="diff_attention",
    )(qi, kj, cvals, proj, proj, proj, qext, _kpos_ext(bq), _ones_block(bq), _causal_tile(bq),
      lam_rows, gain)


def _compress_kernel(a_ref, pos_ref, w1_ref, w2_ref, o_ref):
    half = CMP_STRIDE * HEAD_DIM
    a = a_ref[...].astype(F32)
    pos = pos_ref[...]
    top = (a + pos[:, :half]).astype(BF16)
    bot = (a + pos[:, half:]).astype(BF16)
    t = jnp.dot(top, w1_ref[:half, :], preferred_element_type=F32)
    b = jnp.dot(bot, w1_ref[half:, :], preferred_element_type=F32)
    rows = a.shape[0]
    hid = t + pltpu.roll(b, rows - 1, 0)
    act = jax.nn.gelu(hid)
    o_ref[...] = jnp.dot(act.astype(BF16), w2_ref[...], preferred_element_type=F32).astype(o_ref.dtype)


def _compress(a, pos, w1, w2):
    _, hkv, rows, width = a.shape
    return pl.pallas_call(
        _compress_kernel,
        grid=(2, hkv),
        in_specs=[pl.BlockSpec((None, None, rows, width), lambda t, h: (t, h, 0, 0)),
                  pl.BlockSpec((None, 1, 2 * width), lambda t, h: (t, 0, 0)),
                  pl.BlockSpec((None, 2 * width, CMP_HIDDEN), lambda t, h: (t, 0, 0)),
                  pl.BlockSpec((None, CMP_HIDDEN, HEAD_DIM), lambda t, h: (t, 0, 0))],
        out_specs=pl.BlockSpec((None, None, rows, HEAD_DIM), lambda t, h: (t, h, 0, 0)),
        out_shape=jax.ShapeDtypeStruct((2, hkv, rows, HEAD_DIM), BF16),
        compiler_params=_params(2),
        name="nsa_compress",
    )(a, pos, w1, w2)


def _split3(x):
    hi = x.astype(BF16)
    r1 = x - hi.astype(F32)
    mid = r1.astype(BF16)
    lo = (r1 - mid.astype(F32)).astype(BF16)
    return hi, mid, lo


def _cmp_select_kernel(c_ref, q_ref, kc_ref, vc_ref, ov_ref, oc_ref, sel_ref, any_ref,
                       *, bq, n_cmp, topn):
    hkv = pl.program_id(0)
    i = pl.program_id(1)
    ncp = kc_ref.shape[0]
    q = q_ref[...].reshape(NSA_GROUP * bq, HEAD_DIM)
    s = lax.dot_general(q, kc_ref[...], NT_DIMS, preferred_element_type=F32)
    t = i * bq + lax.broadcasted_iota(jnp.int32, (bq, ncp), 0)
    cidx = lax.broadcasted_iota(jnp.int32, (bq, ncp), 1)
    dist = t - (cidx * CMP_STRIDE + CMP_BLOCK - 1)
    ok = (dist >= 0) & (cidx < n_cmp)
    distf = dist.astype(F32)
    vc = vc_ref[...]
    psum = jnp.zeros((bq, ncp), F32)
    for g in range(NSA_GROUP):
        sg = jnp.where(ok, s[g * bq:(g + 1) * bq] - c_ref[hkv * NSA_GROUP + g] * distf, NEG)
        mx = jnp.max(sg, axis=1, keepdims=True)
        e = jnp.where(ok, jnp.exp2(sg - mx), 0.0)
        den = jnp.sum(e, axis=1, keepdims=True)
        pg = e / jnp.where(den > 0.0, den, 1.0)
        oc_ref[:, g * HEAD_DIM:(g + 1) * HEAD_DIM] = jnp.dot(
            pg.astype(BF16), vc, preferred_element_type=F32)
        psum = psum + pg
    ov = ov_ref[...]
    imp = jnp.zeros((bq, LANES), F32)
    for piece in _split3(psum):
        imp = imp + jnp.dot(piece, ov, preferred_element_type=F32)
    imp_t = imp.T
    tq = i * bq + lax.broadcasted_iota(jnp.int32, (LANES, bq), 1)
    blk = lax.broadcasted_iota(jnp.int32, (LANES, bq), 0)
    cur = lax.shift_right_arithmetic(tq, SEL_BLOCK.bit_length() - 1)
    forced = (blk == 0) | (blk == cur) | (blk == cur - 1)
    score = jnp.where(forced, FORCE_SCORE, jnp.where(blk <= cur, imp_t, -1.0))
    removed = -3.0e38
    blkf = blk.astype(F32)

    def pick(_, sc):
        mx = jnp.max(sc, axis=0, keepdims=True)
        first = jnp.min(jnp.where(sc == mx, blkf, float(LANES)), axis=0, keepdims=True)
        return jnp.where(blkf == first, removed, sc)

    picked = lax.fori_loop(0, topn, pick, score)
    sel = jnp.where(picked == removed, 1.0, 0.0).T
    sel_ref[...] = sel.astype(sel_ref.dtype)
    any_ref[...] = jnp.max(sel, axis=0, keepdims=True)


def _cmp_select(proj, kvc, overlap, cvals, bq):
    s = proj.shape[1]
    n_cmp = (s - CMP_BLOCK) // CMP_STRIDE + 1
    n_sel = s // SEL_BLOCK
    topn = min(SEL_TOPN, n_sel)
    ncp = kvc.shape[2]
    gw = NSA_GROUP * HEAD_DIM
    grid_spec = pltpu.PrefetchScalarGridSpec(
        num_scalar_prefetch=1,
        grid=(NSA_KV_HEADS, s // bq),
        in_specs=[
            pl.BlockSpec((NSA_GROUP, bq, LANES), lambda h, i, c: (OFF_NQ // gw + h, i, 0)),
            pl.BlockSpec((None, None, ncp, HEAD_DIM), lambda h, i, c: (0, h, 0, 0)),
            pl.BlockSpec((None, None, ncp, HEAD_DIM), lambda h, i, c: (1, h, 0, 0)),
            pl.BlockSpec((ncp, LANES), lambda h, i, c: (0, 0)),
        ],
        out_specs=[pl.BlockSpec((bq, gw), lambda h, i, c: (i, h)),
                   pl.BlockSpec((None, bq, LANES), lambda h, i, c: (h, i, 0)),
                   pl.BlockSpec((None, None, 1, LANES), lambda h, i, c: (h, i, 0, 0))],
    )
    return pl.pallas_call(
        functools.partial(_cmp_select_kernel, bq=bq, n_cmp=n_cmp, topn=topn),
        grid_spec=grid_spec,
        out_shape=[jax.ShapeDtypeStruct((s, NSA_WIDTH), F32),
                   jax.ShapeDtypeStruct((NSA_KV_HEADS, s, LANES), BF16),
                   jax.ShapeDtypeStruct((NSA_KV_HEADS, s // bq, 1, LANES), F32)],
        compiler_params=_params(2),
        name="nsa_cmp_select",
    )(cvals, proj, kvc, kvc, overlap)


def _sel_kernel(qi_ref, kj_ref, nact_ref, c_ref, q_ref, k_ref, v_ref, sel_ref, exp_ref, qext_ref,
                kext_ref, one_ref, causal_ref, o_ref, qa_ref, m_ref, acc_ref, *, bq, n_pairs):
    hkv = pl.program_id(0)
    p = pl.program_id(1)
    qi = qi_ref[hkv * n_pairs + p]
    kj = kj_ref[hkv * n_pairs + p]
    live = p < nact_ref[hkv]

    @pl.when(live & (kj == 0))
    def _():
        m_ref[...] = jnp.full_like(m_ref, NEG)
        acc_ref[...] = jnp.zeros_like(acc_ref)
        for g in range(NSA_GROUP):
            qa_ref[g * bq:(g + 1) * bq, :HEAD_DIM] = q_ref[g]
            qa_ref[g * bq:(g + 1) * bq, HEAD_DIM:] = jnp.broadcast_to(qext_ref[g], (bq, LANES))

    def step(diag):
        one = one_ref[...]
        maskb = jnp.dot(jnp.concatenate([sel_ref[...], one], axis=1), exp_ref[...],
                        preferred_element_type=F32)
        if diag:
            maskb = maskb + causal_ref[...]
        ka = jnp.concatenate([k_ref[...], kext_ref[...]], axis=1)
        va = jnp.concatenate([v_ref[...], one], axis=1)
        dtile = ((qi - kj) * bq).astype(F32)
        s_all = lax.dot_general(qa_ref[...], ka, NT_DIMS, preferred_element_type=F32)
        blocks = ((0, bq // 2, bq // 2), (bq // 2, bq // 2, bq)) if diag else ((0, bq, bq),)
        for g in range(NSA_GROUP):
            c = 0.0 if diag else -c_ref[hkv * NSA_GROUP + g] * dtile
            for r0, nr, nk in blocks:
                rows = pl.ds(g * bq + r0, nr)
                s = s_all[g * bq + r0:g * bq + r0 + nr, :nk] + maskb[r0:r0 + nr, :nk]
                pr, alpha = _flash_update(s, c, m_ref, rows)
                pv = jnp.dot(pr.astype(BF16), va[:nk], preferred_element_type=F32)
                acc_ref[rows, :] = _lanes(alpha, 2 * HEAD_DIM) * acc_ref[rows, :] + pv

    @pl.when(live & (kj < qi))
    def _():
        step(False)

    @pl.when(live & (kj == qi))
    def _():
        step(True)
        for g in range(NSA_GROUP):
            rows = pl.ds(g * bq, bq)
            o_ref[:, g * HEAD_DIM:(g + 1) * HEAD_DIM] = (
                acc_ref[rows, :HEAD_DIM] / acc_ref[rows, HEAD_DIM:])


def _expand_aug(s, bk):
    key_blk = np.arange(s) // SEL_BLOCK
    e = np.zeros((2 * LANES, s), np.float32)
    e[key_blk, np.arange(s)] = MASK_BIG
    e[LANES, :] = -MASK_BIG
    return jnp.asarray(e.reshape(2 * LANES, s // bk, bk).transpose(1, 0, 2), BF16)


def _sel_schedule(any_rows, s, bq):
    nq = s // bq
    per_tile = bq // SEL_BLOCK
    a = any_rows.reshape(NSA_KV_HEADS, nq, -1, LANES).max(axis=2)
    a = a[:, :, :nq * per_tile].reshape(NSA_KV_HEADS, nq, nq, per_tile).max(axis=-1) > 0.0
    tq, tk = _tri_pairs(nq)
    act = a[:, tq, tk] | (tk == 0)[None, :] | (tk == tq)[None, :]
    order = jnp.argsort(jnp.logical_not(act), axis=1, stable=True)
    nact = jnp.sum(act, axis=1).astype(jnp.int32)
    pos = jnp.minimum(jnp.arange(tq.shape[0], dtype=jnp.int32)[None, :], nact[:, None] - 1)
    idx = jnp.take_along_axis(order, pos, axis=1)
    return tq[idx].reshape(-1), tk[idx].reshape(-1), nact


def _sel_attention(proj, sel, any_rows, cvals, qext, bq):
    s = proj.shape[1]
    qi, kj, nact = _sel_schedule(any_rows, s, bq)
    n_pairs = qi.shape[0] // NSA_KV_HEADS
    gw = NSA_GROUP * HEAD_DIM
    d = HEAD_DIM
    const2 = lambda h, p, qi, kj, na, c: (0, 0)
    grid_spec = pltpu.PrefetchScalarGridSpec(
        num_scalar_prefetch=4,
        grid=(NSA_KV_HEADS, n_pairs),
        in_specs=[
            pl.BlockSpec((NSA_GROUP, bq, d),
                         lambda h, p, qi, kj, na, c: (OFF_NQ // gw + h, qi[h * n_pairs + p], 0)),
            pl.BlockSpec((None, bq, d),
                         lambda h, p, qi, kj, na, c: (OFF_KS // d + h, kj[h * n_pairs + p], 0)),
            pl.BlockSpec((None, bq, d),
                         lambda h, p, qi, kj, na, c: (OFF_VS // d + h, kj[h * n_pairs + p], 0)),
            pl.BlockSpec((None, bq, LANES), lambda h, p, qi, kj, na, c: (h, qi[h * n_pairs + p], 0)),
            pl.BlockSpec((None, 2 * LANES, bq), lambda h, p, qi, kj, na, c: (kj[h * n_pairs + p], 0, 0)),
            pl.BlockSpec((None, NSA_GROUP, 1, LANES), lambda h, p, qi, kj, na, c: (h, 0, 0, 0)),
            pl.BlockSpec((bq, LANES), const2),
            pl.BlockSpec((bq, LANES), const2),
            pl.BlockSpec((bq, bq), const2),
        ],
        out_specs=pl.BlockSpec((bq, gw), lambda h, p, qi, kj, na, c: (qi[h * n_pairs + p], h)),
        scratch_shapes=[pltpu.VMEM((NSA_GROUP * bq, 2 * d), BF16),
                        pltpu.VMEM((NSA_GROUP * bq, LANES), F32),
                        pltpu.VMEM((NSA_GROUP * bq, 2 * d), F32)],
    )
    return pl.pallas_call(
        functools.partial(_sel_kernel, bq=bq, n_pairs=n_pairs),
        grid_spec=grid_spec,
        out_shape=jax.ShapeDtypeStruct((s, NSA_WIDTH), F32),
        compiler_params=_params(2),
        name="nsa_selected",
    )(qi, kj, nact, cvals, proj, proj, proj, sel, _expand_aug(s, bq),
      qext.reshape(NSA_KV_HEADS, NSA_GROUP, 1, LANES), _kpos_ext(bq), _ones_block(bq), _causal_tile(bq))


def _win_kernel(c_ref, q_ref, kp_ref, kc_ref, vp_ref, vc_ref, one_ref, oc_ref, os_ref, gate_ref, o_ref,
                bias_ref):
    hkv = pl.program_id(0)
    i = pl.program_id(1)
    bq = WINDOW
    half = bq // 2
    band = bq + half

    @pl.when(i == 0)
    def _():
        r = lax.broadcasted_iota(jnp.int32, (half, band), 0)
        b = lax.broadcasted_iota(jnp.int32, (half, band), 1)
        d = WINDOW + r - b
        inside = (d >= 0) & (d < WINDOW)
        df = d.astype(F32)
        for g in range(NSA_GROUP):
            bias = jnp.where(inside, -c_ref[hkv * NSA_GROUP + g] * df, NEG)
            bias_ref[g, 0] = bias
            for hh in range(2):
                bias_ref[g, 1 + hh] = jnp.where(b < bq - hh * half, NEG, bias)

    kband = jnp.concatenate([kp_ref[...], kc_ref[...]], axis=0)
    vband = jnp.concatenate([vp_ref[...], vc_ref[...]], axis=0)
    one = one_ref[...]
    for hh in range(2):
        rows = slice(hh * half, (hh + 1) * half)
        keys = slice(hh * half, hh * half + band)
        which = jnp.where(i == 0, 1 + hh, 0)
        q = q_ref[:, rows, :].reshape(NSA_GROUP * half, HEAD_DIM)
        s_all = lax.dot_general(q, kband[keys], NT_DIMS, preferred_element_type=F32)
        va = jnp.concatenate([vband[keys], one], axis=1)
        ps = []
        for g in range(NSA_GROUP):
            s = s_all[g * half:(g + 1) * half] + bias_ref[g, which]
            ps.append(jnp.exp2(s - jnp.max(s, axis=1, keepdims=True)).astype(BF16))
        pv = jnp.dot(jnp.concatenate(ps, axis=0), va, preferred_element_type=F32)
        gates = gate_ref[rows, :]
        for g in range(NSA_GROUP):
            cols = slice(g * HEAD_DIM, (g + 1) * HEAD_DIM)
            pg = pv[g * half:(g + 1) * half]
            ow = pg[:, :HEAD_DIM] / pg[:, HEAD_DIM:]
            gc = gates[:, 3 * g + 0:3 * g + 1]
            gs = gates[:, 3 * g + 1:3 * g + 2]
            gw = gates[:, 3 * g + 2:3 * g + 3]
            o_ref[rows, cols] = (gc * oc_ref[rows, cols] + gs * os_ref[rows, cols]
                                 + gw * ow).astype(o_ref.dtype)


def _win_combine(proj, o_cmp, o_sel, gates, cvals):
    s = proj.shape[1]
    bq = WINDOW
    band = bq + bq // 2
    gw = NSA_GROUP * HEAD_DIM
    d = HEAD_DIM
    prev = lambda i: jnp.maximum(i - 1, 0)
    grid_spec = pltpu.PrefetchScalarGridSpec(
        num_scalar_prefetch=1,
        grid=(NSA_KV_HEADS, s // bq),
        in_specs=[
            pl.BlockSpec((NSA_GROUP, bq, d), lambda h, i, c: (OFF_NQ // gw + h, i, 0)),
            pl.BlockSpec((None, bq, d), lambda h, i, c: (OFF_KW // d + h, prev(i), 0)),
            pl.BlockSpec((None, bq, d), lambda h, i, c: (OFF_KW // d + h, i, 0)),
            pl.BlockSpec((None, bq, d), lambda h, i, c: (OFF_VW // d + h, prev(i), 0)),
            pl.BlockSpec((None, bq, d), lambda h, i, c: (OFF_VW // d + h, i, 0)),
            pl.BlockSpec((band, LANES), lambda h, i, c: (0, 0)),
            pl.BlockSpec((bq, gw), lambda h, i, c: (i, h)),
            pl.BlockSpec((bq, gw), lambda h, i, c: (i, h)),
            pl.BlockSpec((bq, LANES), lambda h, i, c: (i, h)),
        ],
        out_specs=pl.BlockSpec((bq, gw), lambda h, i, c: (i, h)),
        scratch_shapes=[pltpu.VMEM((NSA_GROUP, 3, bq // 2, band), F32)],
    )
    return pl.pallas_call(
        _win_kernel,
        grid_spec=grid_spec,
        out_shape=jax.ShapeDtypeStruct((s, NSA_WIDTH), BF16),
        compiler_params=_params(2),
        name="nsa_window_combine",
    )(cvals, proj, proj, proj, proj, proj, _ones_block(band), o_cmp, o_sel, gates)


def _ffn(h, norm_g, w_gate, w_up, w_down):
    u = _rmsnorm(h, norm_g, BF16)
    act, w_down_bf16 = _ffn_up(u, w_gate, w_up, w_down)
    return _mm_res(act, w_down_bf16, h, 0.5)


def _overlap_matrix(s, rows):
    n_cmp = (s - CMP_BLOCK) // CMP_STRIDE + 1
    n_sel = s // SEL_BLOCK
    cs = np.arange(rows)[:, None] * CMP_STRIDE
    ss = np.arange(LANES)[None, :] * SEL_BLOCK
    ov = (cs <= ss + SEL_BLOCK - 1) & (cs + CMP_BLOCK - 1 >= ss)
    ov &= (np.arange(rows)[:, None] < n_cmp) & (np.arange(LANES)[None, :] < n_sel)
    return jnp.asarray(ov, BF16)


def _query_colscale():
    cs = np.ones((1, OFF_G), np.float32)
    cs[:, OFF_DQ:OFF_DK] = QSCALE
    cs[:, OFF_NQ:OFF_KC] = QSCALE
    return jnp.asarray(cs)


def _layer(h, layer, ffn1_norm, ffn1_w_gate, ffn1_w_up, ffn1_w_down, mix_norm, w_in, gate_bias,
           lambda_q1, lambda_k1, lambda_q2, lambda_k2, diff_norm,
           cmp_pos_k, cmp_w1_k, cmp_w2_k, cmp_pos_v, cmp_w1_v, cmp_w2_v,
           w_out, ffn2_norm, ffn2_w_gate, ffn2_w_up, ffn2_w_down):
    s = h.shape[0]
    assert s % WINDOW == 0 and s // SEL_BLOCK <= LANES and OFF_G % 512 == 0
    h = _ffn(h, ffn1_norm, ffn1_w_gate, ffn1_w_up, ffn1_w_down)

    u = _rmsnorm(h, mix_norm, BF16)
    w_in_t = w_in.T
    proj = _proj_slabs(u, w_in_t, _query_colscale())
    wg = w_in_t[OFF_G:].T.reshape(-1, NSA_KV_HEADS, GATES_PER_KV)
    wg = jnp.pad(wg, ((0, 0), (0, 0), (0, LANES - GATES_PER_KV))).reshape(-1, NSA_KV_HEADS * LANES)
    gb = jnp.pad(gate_bias.astype(F32).reshape(NSA_KV_HEADS, GATES_PER_KV),
                 ((0, 0), (0, LANES - GATES_PER_KV))).reshape(1, NSA_KV_HEADS * LANES)
    gates = _gate_proj(u, wg.astype(BF16), gb)

    lam_rows = jnp.pad(jnp.stack([lambda_q1, lambda_k1, lambda_q2, lambda_k2]).astype(F32),
                       ((0, 4), (0, 0)))
    o_diff = _diff_attention(proj, lam_rows, diff_norm.reshape(1, -1).astype(F32), layer,
                             _pick(s, 1024))

    nsa_c, nsa_qext = _alibi_consts(NSA_HEADS)
    rows = s // CMP_STRIDE
    kv_cmp = proj[OFF_KC // LANES:OFF_KS // LANES].reshape(
        2, NSA_KV_HEADS, rows, CMP_STRIDE * HEAD_DIM)
    pos = jnp.stack([cmp_pos_k, cmp_pos_v]).astype(F32).reshape(2, 1, CMP_BLOCK * HEAD_DIM)
    w1 = jnp.stack([cmp_w1_k, cmp_w1_v]).astype(BF16)
    w2 = jnp.stack([cmp_w2_k, cmp_w2_v]).astype(BF16)
    kvc = _compress(kv_cmp, pos, w1, w2)
    o_cmp, sel, any_rows = _cmp_select(proj, kvc, _overlap_matrix(s, rows), nsa_c, _pick(s, 256))
    o_sel = _sel_attention(proj, sel, any_rows, nsa_c, nsa_qext, _pick(s, 512))
    o_nsa = _win_combine(proj, o_cmp, o_sel, gates, nsa_c)

    wo = w_out.astype(BF16)
    h = _mm2_res(o_diff, o_nsa, wo[:DIFF_WIDTH], wo[DIFF_WIDTH:], h)
    return _ffn(h, ffn2_norm, ffn2_w_gate, ffn2_w_up, ffn2_w_down)


def kernel(x, ffn1_norm, ffn1_w_gate, ffn1_w_up, ffn1_w_down, mix_norm, w_in, gate_bias, lambda_q1, lambda_k1, lambda_q2, lambda_k2, diff_norm, cmp_pos_k, cmp_w1_k, cmp_w2_k, cmp_pos_v, cmp_w1_v, cmp_w2_v, w_out, ffn2_norm, ffn2_w_gate, ffn2_w_up, ffn2_w_down, final_norm):
    b, s, d = x.shape
    per_layer = (ffn1_norm, ffn1_w_gate, ffn1_w_up, ffn1_w_down, mix_norm, w_in, gate_bias,
                 lambda_q1, lambda_k1, lambda_q2, lambda_k2, diff_norm,
                 cmp_pos_k, cmp_w1_k, cmp_w2_k, cmp_pos_v, cmp_w1_v, cmp_w2_v,
                 w_out, ffn2_norm, ffn2_w_gate, ffn2_w_up, ffn2_w_down)
    outs = []
    for bi in range(b):
        h = x.reshape(s, d) if b == 1 else x[bi]
        for layer in range(DEPTH):
            h = _layer(h, layer, *[p[layer] for p in per_layer])
        outs.append(_rmsnorm(h, final_norm, x.dtype))
    return outs[0].reshape(1, s, d) if b == 1 else jnp.stack(outs)
```

```python
import functools
import math

import ml_dtypes
import numpy as np
import jax
import jax.numpy as jnp
from jax import lax
from jax.experimental import pallas as pl
from jax.experimental.pallas import tpu as pltpu

D_MODEL = 4096
DEPTH = 1
HEAD_DIM = 128
DIFF_V_DIM = 2 * HEAD_DIM
DIFF_HEADS = (D_MODEL // 2) // DIFF_V_DIM
DIFF_WIDTH = DIFF_HEADS * DIFF_V_DIM
NSA_HEADS = (D_MODEL - DIFF_WIDTH) // HEAD_DIM
NSA_KV_HEADS = 4
NSA_GROUP = NSA_HEADS // NSA_KV_HEADS
NSA_WIDTH = NSA_HEADS * HEAD_DIM
CMP_BLOCK = 32
CMP_STRIDE = 16
CMP_HIDDEN = 256
SEL_BLOCK = 64
SEL_TOPN = 16
WINDOW = 512
EPS = 1e-6
NEG = -1e30
FORCE_SCORE = 1e4

LANES = 128
VMEM_LIMIT = 56 * 1024 * 1024
LOG2E = 1.4426950408889634
QSCALE = HEAD_DIM ** -0.5 * LOG2E
MASK_BIG = 2.0 ** 100
KPOS_SPLIT = 32
OFF_DQ = 0
OFF_DK = OFF_DQ + DIFF_HEADS * 2 * HEAD_DIM
OFF_DV = OFF_DK + DIFF_HEADS * 2 * HEAD_DIM
OFF_NQ = OFF_DV + DIFF_HEADS * DIFF_V_DIM
OFF_KC = OFF_NQ + NSA_HEADS * HEAD_DIM
OFF_VC = OFF_KC + NSA_KV_HEADS * HEAD_DIM
OFF_KS = OFF_VC + NSA_KV_HEADS * HEAD_DIM
OFF_VS = OFF_KS + NSA_KV_HEADS * HEAD_DIM
OFF_KW = OFF_VS + NSA_KV_HEADS * HEAD_DIM
OFF_VW = OFF_KW + NSA_KV_HEADS * HEAD_DIM
OFF_G = OFF_VW + NSA_KV_HEADS * HEAD_DIM
GATES_PER_KV = 3 * NSA_GROUP

F32 = jnp.float32
BF16 = jnp.bfloat16
NT_DIMS = (((1,), (1,)), ((), ()))


def _params(n_axes):
    return pltpu.CompilerParams(dimension_semantics=("arbitrary",) * n_axes,
                                vmem_limit_bytes=VMEM_LIMIT)


def _pick(n, pref):
    b = min(pref, n)
    while n % b:
        b //= 2
    return b


def _rmsnorm_kernel(x_ref, g_ref, o_ref):
    x = x_ref[...]
    ms = jnp.mean(x * x, axis=-1, keepdims=True)
    o_ref[...] = (x * lax.rsqrt(ms + EPS) * g_ref[...]).astype(o_ref.dtype)


def _rmsnorm(x, g, out_dtype):
    s, d = x.shape
    bm = _pick(s, 256)
    return pl.pallas_call(
        _rmsnorm_kernel,
        grid=(s // bm,),
        in_specs=[pl.BlockSpec((bm, d), lambda i: (i, 0)),
                  pl.BlockSpec((1, d), lambda i: (0, 0))],
        out_specs=pl.BlockSpec((bm, d), lambda i: (i, 0)),
        out_shape=jax.ShapeDtypeStruct((s, d), out_dtype),
        compiler_params=_params(1),
        name="rmsnorm",
    )(x, g.reshape(1, d).astype(F32))


def _ffn_up_kernel(u_ref, wg_ref, wu_ref, wd_ref, o_ref, wdb_ref):
    @pl.when(pl.program_id(0) == 0)
    def _():
        wdb_ref[...] = wd_ref[...].astype(BF16)

    u = u_ref[...]
    g = jnp.dot(u, wg_ref[...].astype(BF16), preferred_element_type=F32)
    up = jnp.dot(u, wu_ref[...].astype(BF16), preferred_element_type=F32)
    o_ref[...] = (g * jax.nn.sigmoid(g) * up).astype(o_ref.dtype)


def _ffn_up(u, wg, wu, wd):
    s, d = u.shape
    f = wg.shape[1]
    bm, bn = _pick(s, 2048), _pick(f, 256)
    nj = f // bn
    wd_block = lambda i, j: (jnp.where(i == 0, j, nj - 1), 0)
    return pl.pallas_call(
        _ffn_up_kernel,
        grid=(s // bm, nj),
        in_specs=[pl.BlockSpec((bm, d), lambda i, j: (i, 0), pipeline_mode=pl.Buffered(1)),
                  pl.BlockSpec((d, bn), lambda i, j: (0, j)),
                  pl.BlockSpec((d, bn), lambda i, j: (0, j)),
                  pl.BlockSpec((bn, wd.shape[1]), wd_block)],
        out_specs=[pl.BlockSpec((bm, bn), lambda i, j: (i, j)),
                   pl.BlockSpec((bn, wd.shape[1]), wd_block)],
        out_shape=[jax.ShapeDtypeStruct((s, f), BF16),
                   jax.ShapeDtypeStruct(wd.shape, BF16)],
        compiler_params=_params(2),
        name="ffn_up",
    )(u, wg, wu, wd)


WEIGHT_RING = 3


def _mm_res_kernel(a_ref, b_hbm, r_ref, o_ref, buf_ref, sem_ref, *, alpha, nj, total, bn):
    t = pl.program_id(0) * nj + pl.program_id(1)

    def copy(step, slot):
        col = pl.multiple_of(lax.rem(step, nj) * bn, bn)
        return pltpu.make_async_copy(b_hbm.at[:, pl.ds(col, bn)], buf_ref.at[slot], sem_ref.at[slot])

    @pl.when(t == 0)
    def _():
        for k in range(WEIGHT_RING - 1):
            copy(k, k).start()

    ahead = t + WEIGHT_RING - 1

    @pl.when(ahead < total)
    def _():
        copy(ahead, lax.rem(ahead, WEIGHT_RING)).start()

    slot = lax.rem(t, WEIGHT_RING)
    copy(t, slot).wait()
    o_ref[...] = r_ref[...] + alpha * jnp.dot(a_ref[...], buf_ref[slot], preferred_element_type=F32)


def _mm_res(a, b, res, alpha):
    s, kdim = a.shape
    n = b.shape[1]
    bm, bn = _pick(s, 512), _pick(n, 256)
    nj = n // bn
    total = (s // bm) * nj
    assert total >= WEIGHT_RING - 1
    return pl.pallas_call(
        functools.partial(_mm_res_kernel, alpha=alpha, nj=nj, total=total, bn=bn),
        grid=(s // bm, nj),
        in_specs=[pl.BlockSpec((bm, kdim), lambda i, j: (i, 0)),
                  pl.BlockSpec(memory_space=pl.ANY),
                  pl.BlockSpec((bm, bn), lambda i, j: (i, j))],
        out_specs=pl.BlockSpec((bm, bn), lambda i, j: (i, j)),
        out_shape=jax.ShapeDtypeStruct((s, n), F32),
        scratch_shapes=[pltpu.VMEM((WEIGHT_RING, kdim, bn), BF16),
                        pltpu.SemaphoreType.DMA((WEIGHT_RING,))],
        compiler_params=_params(2),
        name="mm_res",
    )(a, b, res)


def _proj_kernel(a_ref, bt_ref, cs_ref, o_ref):
    acc = lax.dot_general(a_ref[...], bt_ref[...].astype(BF16), NT_DIMS,
                          preferred_element_type=F32) * cs_ref[...]
    for t in range(o_ref.shape[0]):
        o_ref[t] = acc[:, t * LANES:(t + 1) * LANES].astype(o_ref.dtype)


def _proj_slabs(a, bt, colscale):
    s, kdim = a.shape
    n = colscale.shape[1]
    bm, bn = _pick(s, 2048), _pick(n, 512)
    return pl.pallas_call(
        _proj_kernel,
        grid=(s // bm, n // bn),
        in_specs=[pl.BlockSpec((bm, kdim), lambda i, j: (i, 0), pipeline_mode=pl.Buffered(1)),
                  pl.BlockSpec((bn, kdim), lambda i, j: (j, 0)),
                  pl.BlockSpec((1, bn), lambda i, j: (0, j))],
        out_specs=pl.BlockSpec((bn // LANES, bm, LANES), lambda i, j: (j, i, 0)),
        out_shape=jax.ShapeDtypeStruct((n // LANES, s, LANES), BF16),
        compiler_params=_params(2),
        name="proj_slabs",
    )(a, bt, colscale)


def _gate_kernel(a_ref, b_ref, bias_ref, o_ref):
    z = jnp.dot(a_ref[...], b_ref[...], preferred_element_type=F32) + bias_ref[...]
    o_ref[...] = jax.nn.sigmoid(z)


def _gate_proj(u, wg, bias):
    s, kdim = u.shape
    n = wg.shape[1]
    bm = _pick(s, 1024)
    return pl.pallas_call(
        _gate_kernel,
        grid=(s // bm,),
        in_specs=[pl.BlockSpec((bm, kdim), lambda i: (i, 0)),
                  pl.BlockSpec((kdim, n), lambda i: (0, 0)),
                  pl.BlockSpec((1, n), lambda i: (0, 0))],
        out_specs=pl.BlockSpec((bm, n), lambda i: (i, 0)),
        out_shape=jax.ShapeDtypeStruct((s, n), F32),
        compiler_params=_params(1),
        name="gate_proj",
    )(u, wg, bias)


def _mm2_res_kernel(a1_ref, a2_ref, b1_ref, b2_ref, r_ref, o_ref):
    acc = jnp.dot(a1_ref[...], b1_ref[...], preferred_element_type=F32)
    acc += jnp.dot(a2_ref[...], b2_ref[...], preferred_element_type=F32)
    o_ref[...] = r_ref[...] + acc


def _mm2_res(a1, a2, b1, b2, res):
    s, k1 = a1.shape
    k2 = a2.shape[1]
    n = b1.shape[1]
    bm, bn = _pick(s, 1024), _pick(n, 512)
    return pl.pallas_call(
        _mm2_res_kernel,
        grid=(s // bm, n // bn),
        in_specs=[pl.BlockSpec((bm, k1), lambda i, j: (i, 0)),
                  pl.BlockSpec((bm, k2), lambda i, j: (i, 0)),
                  pl.BlockSpec((k1, bn), lambda i, j: (0, j)),
                  pl.BlockSpec((k2, bn), lambda i, j: (0, j)),
                  pl.BlockSpec((bm, bn), lambda i, j: (i, j))],
        out_specs=pl.BlockSpec((bm, bn), lambda i, j: (i, j)),
        out_shape=jax.ShapeDtypeStruct((s, n), F32),
        compiler_params=_params(2),
        name="mm2_res",
    )(a1, a2, b1, b2, res)


def _tri_pairs(n):
    qi = np.repeat(np.arange(n), np.arange(1, n + 1))
    kj = np.concatenate([np.arange(i + 1) for i in range(n)])
    return jnp.asarray(qi, jnp.int32), jnp.asarray(kj, jnp.int32)


def _bf16_round(x):
    return np.asarray(x, np.float32).astype(ml_dtypes.bfloat16).astype(np.float64)


def _alibi_consts(n_heads):
    c = np.float32(2.0 ** (-8.0 * np.arange(1, n_heads + 1) / n_heads) * LOG2E).astype(np.float64)
    hi = _bf16_round(c)
    mid = _bf16_round(c - hi)
    lo = _bf16_round(c - hi - mid)
    rows = np.zeros((n_heads, 1, LANES), np.float32)
    for col, piece in enumerate((hi, hi, mid, mid, lo, lo)):
        rows[:, 0, col] = piece
    return jnp.asarray(c, F32), jnp.asarray(rows, BF16)


def _kpos_ext(bk):
    pos = np.arange(bk)
    a = (pos // KPOS_SPLIT) * KPOS_SPLIT
    b = pos % KPOS_SPLIT
    ext = np.zeros((bk, LANES), np.float32)
    for col in range(0, 6, 2):
        ext[:, col] = a
        ext[:, col + 1] = b
    return jnp.asarray(ext, BF16)


def _ones_block(rows):
    return jnp.ones((rows, LANES), BF16)


def _causal_tile(b):
    r = np.arange(b)
    return jnp.asarray(np.where(r[:, None] >= r[None, :], 0.0, NEG), F32)


def _lanes(x, width):
    if width == LANES:
        return x
    return jnp.tile(x, (1, width // LANES))


def _flash_update(s, c, m_ref, rows):
    m_prev = m_ref[rows, :]
    m_cur = jnp.max(s, axis=1, keepdims=True) + c
    m_next = jnp.maximum(m_prev, m_cur)
    m_ref[rows, :] = m_next
    p = jnp.exp2(s - _lanes(m_next - c, s.shape[1]))
    return p, jnp.exp2(m_prev - m_next)


def _diff_kernel(qi_ref, kj_ref, c_ref, q_ref, k_ref, v_ref, qext_ref, kext_ref, one_ref, causal_ref,
                 lam_ref, g_ref, o_ref, qa_ref, m_ref, acc_ref, *, bq, lam0):
    h = pl.program_id(0)
    p = pl.program_id(1)
    qi = qi_ref[p]
    kj = kj_ref[p]
    acc_w = DIFF_V_DIM + LANES

    @pl.when(kj == 0)
    def _():
        m_ref[...] = jnp.full_like(m_ref, NEG)
        acc_ref[...] = jnp.zeros_like(acc_ref)
        ext = jnp.broadcast_to(qext_ref[...], (bq, LANES))
        for mp in range(2):
            qa_ref[mp, :, :HEAD_DIM] = q_ref[mp]
            qa_ref[mp, :, HEAD_DIM:] = ext

    def step(diag):
        c = 0.0 if diag else -c_ref[h] * ((qi - kj) * bq).astype(F32)
        va = jnp.concatenate([v_ref[0], v_ref[1], one_ref[...]], axis=1)
        kext = kext_ref[...]
        blocks = ((0, bq // 2, bq // 2), (bq // 2, bq // 2, bq)) if diag else ((0, bq, bq),)
        for mp in range(2):
            ka = jnp.concatenate([k_ref[mp], kext], axis=1)
            for r0, nr, nk in blocks:
                s = lax.dot_general(qa_ref[mp, r0:r0 + nr, :], ka[:nk], NT_DIMS,
                                    preferred_element_type=F32)
                if diag:
                    s = s + causal_ref[r0:r0 + nr, :nk]
                rows = pl.ds(mp * bq + r0, nr)
                pr, alpha = _flash_update(s, c, m_ref, rows)
                pv = jnp.dot(pr.astype(BF16), va[:nk], preferred_element_type=F32)
                acc_ref[rows, :] = _lanes(alpha, acc_w) * acc_ref[rows, :] + pv

    @pl.when(kj < qi)
    def _():
        step(False)

    @pl.when(kj == qi)
    def _():
        step(True)
        lam_rows = lam_ref[...]
        d1 = jnp.sum(lam_rows[0:1] * lam_rows[1:2], axis=1, keepdims=True)
        d2 = jnp.sum(lam_rows[2:3] * lam_rows[3:4], axis=1, keepdims=True)
        lam = jnp.exp(d1) - jnp.exp(d2) + lam0
        o1 = acc_ref[0:bq, :DIFF_V_DIM] / _lanes(acc_ref[0:bq, DIFF_V_DIM:], DIFF_V_DIM)
        o2 = acc_ref[bq:2 * bq, :DIFF_V_DIM] / _lanes(acc_ref[bq:2 * bq, DIFF_V_DIM:], DIFF_V_DIM)
        o = o1 - lam * o2
        ms = jnp.mean(o * o, axis=-1, keepdims=True)
        y = o * lax.rsqrt(ms + EPS) * g_ref[...]
        o_ref[...] = (y * (1.0 - lam0)).astype(o_ref.dtype)


def _diff_attention(proj, lam_rows, gain, layer, bq):
    s = proj.shape[1]
    qi, kj = _tri_pairs(s // bq)
    cvals, qext = _alibi_consts(DIFF_HEADS)
    lam0 = 0.8 - 0.6 * math.exp(-0.3 * layer)
    w = DIFF_V_DIM
    const2 = lambda h, p, qi, kj, c: (0, 0)
    grid_spec = pltpu.PrefetchScalarGridSpec(
        num_scalar_prefetch=3,
        grid=(DIFF_HEADS, qi.shape[0]),
        in_specs=[
            pl.BlockSpec((2, bq, LANES), lambda h, p, qi, kj, c: (OFF_DQ // w + h, qi[p], 0)),
            pl.BlockSpec((2, bq, LANES), lambda h, p, qi, kj, c: (OFF_DK // w + h, kj[p], 0)),
            pl.BlockSpec((2, bq, LANES), lambda h, p, qi, kj, c: (OFF_DV // w + h, kj[p], 0)),
            pl.BlockSpec((None, 1, LANES), lambda h, p, qi, kj, c: (h, 0, 0)),
            pl.BlockSpec((bq, LANES), const2),
            pl.BlockSpec((bq, LANES), const2),
            pl.BlockSpec((bq, bq), const2),
            pl.BlockSpec((8, HEAD_DIM), const2),
            pl.BlockSpec((1, w), const2),
        ],
        out_specs=pl.BlockSpec((bq, w), lambda h, p, qi, kj, c: (qi[p], h)),
        scratch_shapes=[pltpu.VMEM((2, bq, 2 * HEAD_DIM), BF16),
                        pltpu.VMEM((2 * bq, LANES), F32),
                        pltpu.VMEM((2 * bq, w + LANES), F32)],
    )
    return pl.pallas_call(
        functools.partial(_diff_kernel, bq=bq, lam0=lam0),
        grid_spec=grid_spec,
        out_shape=jax.ShapeDtypeStruct((s, DIFF_WIDTH), BF16),
        compiler_params=_params(2),
        name="diff_attention",
    )(qi, kj, cvals, proj, proj, proj, qext, _kpos_ext(bq), _ones_block(bq), _causal_tile(bq),
      lam_rows, gain)


def _compress_kernel(a_ref, pos_ref, w1_ref, w2_ref, o_ref):
    half = CMP_STRIDE * HEAD_DIM
    a = a_ref[...].astype(F32)
    pos = pos_ref[...]
    top = (a + pos[:, :half]).astype(BF16)
    bot = (a + pos[:, half:]).astype(BF16)
    t = jnp.dot(top, w1_ref[:half, :], preferred_element_type=F32)
    b = jnp.dot(bot, w1_ref[half:, :], preferred_element_type=F32)
    rows = a.shape[0]
    hid = t + pltpu.roll(b, rows - 1, 0)
    act = jax.nn.gelu(hid)
    o_ref[...] = jnp.dot(act.astype(BF16), w2_ref[...], preferred_element_type=F32).astype(o_ref.dtype)


def _compress(a, pos, w1, w2):
    _, hkv, rows, width = a.shape
    return pl.pallas_call(
        _compress_kernel,
        grid=(2, hkv),
        in_specs=[pl.BlockSpec((None, None, rows, width), lambda t, h: (t, h, 0, 0)),
                  pl.BlockSpec((None, 1, 2 * width), lambda t, h: (t, 0, 0)),
                  pl.BlockSpec((None, 2 * width, CMP_HIDDEN), lambda t, h: (t, 0, 0)),
                  pl.BlockSpec((None, CMP_HIDDEN, HEAD_DIM), lambda t, h: (t, 0, 0))],
        out_specs=pl.BlockSpec((None, None, rows, HEAD_DIM), lambda t, h: (t, h, 0, 0)),
        out_shape=jax.ShapeDtypeStruct((2, hkv, rows, HEAD_DIM), BF16),
        compiler_params=_params(2),
        name="nsa_compress",
    )(a, pos, w1, w2)


def _split3(x):
    hi = x.astype(BF16)
    r1 = x - hi.astype(F32)
    mid = r1.astype(BF16)
    lo = (r1 - mid.astype(F32)).astype(BF16)
    return hi, mid, lo


def _cmp_select_kernel(c_ref, q_ref, kc_ref, vc_ref, ov_ref, oc_ref, sel_ref, any_ref,
                       *, bq, n_cmp, topn):
    hkv = pl.program_id(0)
    i = pl.program_id(1)
    ncp = kc_ref.shape[0]
    q = q_ref[...].reshape(NSA_GROUP * bq, HEAD_DIM)
    s = lax.dot_general(q, kc_ref[...], NT_DIMS, preferred_element_type=F32)
    t = i * bq + lax.broadcasted_iota(jnp.int32, (bq, ncp), 0)
    cidx = lax.broadcasted_iota(jnp.int32, (bq, ncp), 1)
    dist = t - (cidx * CMP_STRIDE + CMP_BLOCK - 1)
    ok = (dist >= 0) & (cidx < n_cmp)
    distf = dist.astype(F32)
    vc = vc_ref[...]
    psum = jnp.zeros((bq, ncp), F32)
    for g in range(NSA_GROUP):
        sg = jnp.where(ok, s[g * bq:(g + 1) * bq] - c_ref[hkv * NSA_GROUP + g] * distf, NEG)
        mx = jnp.max(sg, axis=1, keepdims=True)
        e = jnp.where(ok, jnp.exp2(sg - mx), 0.0)
        den = jnp.sum(e, axis=1, keepdims=True)
        pg = e / jnp.where(den > 0.0, den, 1.0)
        oc_ref[:, g * HEAD_DIM:(g + 1) * HEAD_DIM] = jnp.dot(
            pg.astype(BF16), vc, preferred_element_type=F32)
        psum = psum + pg
    ov = ov_ref[...]
    imp = jnp.zeros((bq, LANES), F32)
    for piece in _split3(psum):
        imp = imp + jnp.dot(piece, ov, preferred_element_type=F32)
    imp_t = imp.T
    tq = i * bq + lax.broadcasted_iota(jnp.int32, (LANES, bq), 1)
    blk = lax.broadcasted_iota(jnp.int32, (LANES, bq), 0)
    cur = lax.shift_right_arithmetic(tq, SEL_BLOCK.bit_length() - 1)
    forced = (blk == 0) | (blk == cur) | (blk == cur - 1)
    score = jnp.where(forced, FORCE_SCORE, jnp.where(blk <= cur, imp_t, -1.0))
    removed = -3.0e38
    blkf = blk.astype(F32)

    def pick(_, sc):
        mx = jnp.max(sc, axis=0, keepdims=True)
        first = jnp.min(jnp.where(sc == mx, blkf, float(LANES)), axis=0, keepdims=True)
        return jnp.where(blkf == first, removed, sc)

    picked = lax.fori_loop(0, topn, pick, score)
    sel = jnp.where(picked == removed, 1.0, 0.0).T
    sel_ref[...] = sel.astype(sel_ref.dtype)
    any_ref[...] = jnp.max(sel, axis=0, keepdims=True)


def _cmp_select(proj, kvc, overlap, cvals, bq):
    s = proj.shape[1]
    n_cmp = (s - CMP_BLOCK) // CMP_STRIDE + 1
    n_sel = s // SEL_BLOCK
    topn = min(SEL_TOPN, n_sel)
    ncp = kvc.shape[2]
    gw = NSA_GROUP * HEAD_DIM
    grid_spec = pltpu.PrefetchScalarGridSpec(
        num_scalar_prefetch=1,
        grid=(NSA_KV_HEADS, s // bq),
        in_specs=[
            pl.BlockSpec((NSA_GROUP, bq, LANES), lambda h, i, c: (OFF_NQ // gw + h, i, 0)),
            pl.BlockSpec((None, None, ncp, HEAD_DIM), lambda h, i, c: (0, h, 0, 0)),
            pl.BlockSpec((None, None, ncp, HEAD_DIM), lambda h, i, c: (1, h, 0, 0)),
            pl.BlockSpec((ncp, LANES), lambda h, i, c: (0, 0)),
        ],
        out_specs=[pl.BlockSpec((bq, gw), lambda h, i, c: (i, h)),
                   pl.BlockSpec((None, bq, LANES), lambda h, i, c: (h, i, 0)),
                   pl.BlockSpec((None, None, 1, LANES), lambda h, i, c: (h, i, 0, 0))],
    )
    return pl.pallas_call(
        functools.partial(_cmp_select_kernel, bq=bq, n_cmp=n_cmp, topn=topn),
        grid_spec=grid_spec,
        out_shape=[jax.ShapeDtypeStruct((s, NSA_WIDTH), F32),
                   jax.ShapeDtypeStruct((NSA_KV_HEADS, s, LANES), BF16),
                   jax.ShapeDtypeStruct((NSA_KV_HEADS, s // bq, 1, LANES), F32)],
        compiler_params=_params(2),
        name="nsa_cmp_select",
    )(cvals, proj, kvc, kvc, overlap)


def _sel_kernel(qi_ref, kj_ref, nact_ref, c_ref, q_ref, k_ref, v_ref, sel_ref, exp_ref, qext_ref,
                kext_ref, one_ref, causal_ref, o_ref, qa_ref, m_ref, acc_ref, *, bq, n_pairs):
    hkv = pl.program_id(0)
    p = pl.program_id(1)
    qi = qi_ref[hkv * n_pairs + p]
    kj = kj_ref[hkv * n_pairs + p]
    live = p < nact_ref[hkv]

    @pl.when(live & (kj == 0))
    def _():
        m_ref[...] = jnp.full_like(m_ref, NEG)
        acc_ref[...] = jnp.zeros_like(acc_ref)
        for g in range(NSA_GROUP):
            qa_ref[g * bq:(g + 1) * bq, :HEAD_DIM] = q_ref[g]
            qa_ref[g * bq:(g + 1) * bq, HEAD_DIM:] = jnp.broadcast_to(qext_ref[g], (bq, LANES))

    def step(diag):
        one = one_ref[...]
        maskb = jnp.dot(jnp.concatenate([sel_ref[...], one], axis=1), exp_ref[...],
                        preferred_element_type=F32)
        if diag:
            maskb = maskb + causal_ref[...]
        ka = jnp.concatenate([k_ref[...], kext_ref[...]], axis=1)
        va = jnp.concatenate([v_ref[...], one], axis=1)
        dtile = ((qi - kj) * bq).astype(F32)
        s_all = lax.dot_general(qa_ref[...], ka, NT_DIMS, preferred_element_type=F32)
        blocks = ((0, bq // 2, bq // 2), (bq // 2, bq // 2, bq)) if diag else ((0, bq, bq),)
        for g in range(NSA_GROUP):
            c = 0.0 if diag else -c_ref[hkv * NSA_GROUP + g] * dtile
            for r0, nr, nk in blocks:
                rows = pl.ds(g * bq + r0, nr)
                s = s_all[g * bq + r0:g * bq + r0 + nr, :nk] + maskb[r0:r0 + nr, :nk]
                pr, alpha = _flash_update(s, c, m_ref, rows)
                pv = jnp.dot(pr.astype(BF16), va[:nk], preferred_element_type=F32)
                acc_ref[rows, :] = _lanes(alpha, 2 * HEAD_DIM) * acc_ref[rows, :] + pv

    @pl.when(live & (kj < qi))
    def _():
        step(False)

    @pl.when(live & (kj == qi))
    def _():
        step(True)
        for g in range(NSA_GROUP):
            rows = pl.ds(g * bq, bq)
            o_ref[:, g * HEAD_DIM:(g + 1) * HEAD_DIM] = (
                acc_ref[rows, :HEAD_DIM] / acc_ref[rows, HEAD_DIM:])


def _expand_aug(s, bk):
    key_blk = np.arange(s) // SEL_BLOCK
    e = np.zeros((2 * LANES, s), np.float32)
    e[key_blk, np.arange(s)] = MASK_BIG
    e[LANES, :] = -MASK_BIG
    return jnp.asarray(e.reshape(2 * LANES, s // bk, bk).transpose(1, 0, 2), BF16)


def _sel_schedule(any_rows, s, bq):
    nq = s // bq
    per_tile = bq // SEL_BLOCK
    a = any_rows.reshape(NSA_KV_HEADS, nq, -1, LANES).max(axis=2)
    a = a[:, :, :nq * per_tile].reshape(NSA_KV_HEADS, nq, nq, per_tile).max(axis=-1) > 0.0
    tq, tk = _tri_pairs(nq)
    act = a[:, tq, tk] | (tk == 0)[None, :] | (tk == tq)[None, :]
    order = jnp.argsort(jnp.logical_not(act), axis=1, stable=True)
    nact = jnp.sum(act, axis=1).astype(jnp.int32)
    pos = jnp.minimum(jnp.arange(tq.shape[0], dtype=jnp.int32)[None, :], nact[:, None] - 1)
    idx = jnp.take_along_axis(order, pos, axis=1)
    return tq[idx].reshape(-1), tk[idx].reshape(-1), nact


def _sel_attention(proj, sel, any_rows, cvals, qext, bq):
    s = proj.shape[1]
    qi, kj, nact = _sel_schedule(any_rows, s, bq)
    n_pairs = qi.shape[0] // NSA_KV_HEADS
    gw = NSA_GROUP * HEAD_DIM
    d = HEAD_DIM
    const2 = lambda h, p, qi, kj, na, c: (0, 0)
    grid_spec = pltpu.PrefetchScalarGridSpec(
        num_scalar_prefetch=4,
        grid=(NSA_KV_HEADS, n_pairs),
        in_specs=[
            pl.BlockSpec((NSA_GROUP, bq, d),
                         lambda h, p, qi, kj, na, c: (OFF_NQ // gw + h, qi[h * n_pairs + p], 0)),
            pl.BlockSpec((None, bq, d),
                         lambda h, p, qi, kj, na, c: (OFF_KS // d + h, kj[h * n_pairs + p], 0)),
            pl.BlockSpec((None, bq, d),
                         lambda h, p, qi, kj, na, c: (OFF_VS // d + h, kj[h * n_pairs + p], 0)),
            pl.BlockSpec((None, bq, LANES), lambda h, p, qi, kj, na, c: (h, qi[h * n_pairs + p], 0)),
            pl.BlockSpec((None, 2 * LANES, bq), lambda h, p, qi, kj, na, c: (kj[h * n_pairs + p], 0, 0)),
            pl.BlockSpec((None, NSA_GROUP, 1, LANES), lambda h, p, qi, kj, na, c: (h, 0, 0, 0)),
            pl.BlockSpec((bq, LANES), const2),
            pl.BlockSpec((bq, LANES), const2),
            pl.BlockSpec((bq, bq), const2),
        ],
        out_specs=pl.BlockSpec((bq, gw), lambda h, p, qi, kj, na, c: (qi[h * n_pairs + p], h)),
        scratch_shapes=[pltpu.VMEM((NSA_GROUP * bq, 2 * d), BF16),
                        pltpu.VMEM((NSA_GROUP * bq, LANES), F32),
                        pltpu.VMEM((NSA_GROUP * bq, 2 * d), F32)],
    )
    return pl.pallas_call(
        functools.partial(_sel_kernel, bq=bq, n_pairs=n_pairs),
        grid_spec=grid_spec,
        out_shape=jax.ShapeDtypeStruct((s, NSA_WIDTH), F32),
        compiler_params=_params(2),
        name="nsa_selected",
    )(qi, kj, nact, cvals, proj, proj, proj, sel, _expand_aug(s, bq),
      qext.reshape(NSA_KV_HEADS, NSA_GROUP, 1, LANES), _kpos_ext(bq), _ones_block(bq), _causal_tile(bq))


def _win_kernel(c_ref, q_ref, kp_ref, kc_ref, vp_ref, vc_ref, one_ref, oc_ref, os_ref, gate_ref, o_ref,
                bias_ref):
    hkv = pl.program_id(0)
    i = pl.program_id(1)
    bq = WINDOW
    half = bq // 2
    band = bq + half

    @pl.when(i == 0)
    def _():
        r = lax.broadcasted_iota(jnp.int32, (half, band), 0)
        b = lax.broadcasted_iota(jnp.int32, (half, band), 1)
        d = WINDOW + r - b
        inside = (d >= 0) & (d < WINDOW)
        df = d.astype(F32)
        for g in range(NSA_GROUP):
            bias = jnp.where(inside, -c_ref[hkv * NSA_GROUP + g] * df, NEG)
            bias_ref[g, 0] = bias
            for hh in range(2):
                bias_ref[g, 1 + hh] = jnp.where(b < bq - hh * half, NEG, bias)

    kband = jnp.concatenate([kp_ref[...], kc_ref[...]], axis=0)
    vband = jnp.concatenate([vp_ref[...], vc_ref[...]], axis=0)
    one = one_ref[...]
    for hh in range(2):
        rows = slice(hh * half, (hh + 1) * half)
        keys = slice(hh * half, hh * half + band)
        which = jnp.where(i == 0, 1 + hh, 0)
        q = q_ref[:, rows, :].reshape(NSA_GROUP * half, HEAD_DIM)
        s_all = lax.dot_general(q, kband[keys], NT_DIMS, preferred_element_type=F32)
        va = jnp.concatenate([vband[keys], one], axis=1)
        ps = []
        for g in range(NSA_GROUP):
            s = s_all[g * half:(g + 1) * half] + bias_ref[g, which]
            ps.append(jnp.exp2(s - jnp.max(s, axis=1, keepdims=True)).astype(BF16))
        pv = jnp.dot(jnp.concatenate(ps, axis=0), va, preferred_element_type=F32)
        gates = gate_ref[rows, :]
        for g in range(NSA_GROUP):
            cols = slice(g * HEAD_DIM, (g + 1) * HEAD_DIM)
            pg = pv[g * half:(g + 1) * half]
            ow = pg[:, :HEAD_DIM] / pg[:, HEAD_DIM:]
            gc = gates[:, 3 * g + 0:3 * g + 1]
            gs = gates[:, 3 * g + 1:3 * g + 2]
            gw = gates[:, 3 * g + 2:3 * g + 3]
            o_ref[rows, cols] = (gc * oc_ref[rows, cols] + gs * os_ref[rows, cols]
                                 + gw * ow).astype(o_ref.dtype)


def _win_combine(proj, o_cmp, o_sel, gates, cvals):
    s = proj.shape[1]
    bq = WINDOW
    band = bq + bq // 2
    gw = NSA_GROUP * HEAD_DIM
    d = HEAD_DIM
    prev = lambda i: jnp.maximum(i - 1, 0)
    grid_spec = pltpu.PrefetchScalarGridSpec(
        num_scalar_prefetch=1,
        grid=(NSA_KV_HEADS, s // bq),
        in_specs=[
            pl.BlockSpec((NSA_GROUP, bq, d), lambda h, i, c: (OFF_NQ // gw + h, i, 0)),
            pl.BlockSpec((None, bq, d), lambda h, i, c: (OFF_KW // d + h, prev(i), 0)),
            pl.BlockSpec((None, bq, d), lambda h, i, c: (OFF_KW // d + h, i, 0)),
            pl.BlockSpec((None, bq, d), lambda h, i, c: (OFF_VW // d + h, prev(i), 0)),
            pl.BlockSpec((None, bq, d), lambda h, i, c: (OFF_VW // d + h, i, 0)),
            pl.BlockSpec((band, LANES), lambda h, i, c: (0, 0)),
            pl.BlockSpec((bq, gw), lambda h, i, c: (i, h)),
            pl.BlockSpec((bq, gw), lambda h, i, c: (i, h)),
            pl.BlockSpec((bq, LANES), lambda h, i, c: (i, h)),
        ],
        out_specs=pl.BlockSpec((bq, gw), lambda h, i, c: (i, h)),
        scratch_shapes=[pltpu.VMEM((NSA_GROUP, 3, bq // 2, band), F32)],
    )
    return pl.pallas_call(
        _win_kernel,
        grid_spec=grid_spec,
        out_shape=jax.ShapeDtypeStruct((s, NSA_WIDTH), BF16),
        compiler_params=_params(2),
        name="nsa_window_combine",
    )(cvals, proj, proj, proj, proj, proj, _ones_block(band), o_cmp, o_sel, gates)


def _ffn(h, norm_g, w_gate, w_up, w_down):
    u = _rmsnorm(h, norm_g, BF16)
    act, w_down_bf16 = _ffn_up(u, w_gate, w_up, w_down)
    return _mm_res(act, w_down_bf16, h, 0.5)


def _overlap_matrix(s, rows):
    n_cmp = (s - CMP_BLOCK) // CMP_STRIDE + 1
    n_sel = s // SEL_BLOCK
    cs = np.arange(rows)[:, None] * CMP_STRIDE
    ss = np.arange(LANES)[None, :] * SEL_BLOCK
    ov = (cs <= ss + SEL_BLOCK - 1) & (cs + CMP_BLOCK - 1 >= ss)
    ov &= (np.arange(rows)[:, None] < n_cmp) & (np.arange(LANES)[None, :] < n_sel)
    return jnp.asarray(ov, BF16)


def _query_colscale():
    cs = np.ones((1, OFF_G), np.float32)
    cs[:, OFF_DQ:OFF_DK] = QSCALE
    cs[:, OFF_NQ:OFF_KC] = QSCALE
    return jnp.asarray(cs)


def _layer(h, layer, ffn1_norm, ffn1_w_gate, ffn1_w_up, ffn1_w_down, mix_norm, w_in, gate_bias,
           lambda_q1, lambda_k1, lambda_q2, lambda_k2, diff_norm,
           cmp_pos_k, cmp_w1_k, cmp_w2_k, cmp_pos_v, cmp_w1_v, cmp_w2_v,
           w_out, ffn2_norm, ffn2_w_gate, ffn2_w_up, ffn2_w_down):
    s = h.shape[0]
    assert s % WINDOW == 0 and s // SEL_BLOCK <= LANES and OFF_G % 512 == 0
    h = _ffn(h, ffn1_norm, ffn1_w_gate, ffn1_w_up, ffn1_w_down)

    u = _rmsnorm(h, mix_norm, BF16)
    w_in_t = w_in.T
    proj = _proj_slabs(u, w_in_t, _query_colscale())
    wg = w_in_t[OFF_G:].T.reshape(-1, NSA_KV_HEADS, GATES_PER_KV)
    wg = jnp.pad(wg, ((0, 0), (0, 0), (0, LANES - GATES_PER_KV))).reshape(-1, NSA_KV_HEADS * LANES)
    gb = jnp.pad(gate_bias.astype(F32).reshape(NSA_KV_HEADS, GATES_PER_KV),
                 ((0, 0), (0, LANES - GATES_PER_KV))).reshape(1, NSA_KV_HEADS * LANES)
    gates = _gate_proj(u, wg.astype(BF16), gb)

    lam_rows = jnp.pad(jnp.stack([lambda_q1, lambda_k1, lambda_q2, lambda_k2]).astype(F32),
                       ((0, 4), (0, 0)))
    o_diff = _diff_attention(proj, lam_rows, diff_norm.reshape(1, -1).astype(F32), layer,
                             _pick(s, 1024))

    nsa_c, nsa_qext = _alibi_consts(NSA_HEADS)
    rows = s // CMP_STRIDE
    kv_cmp = proj[OFF_KC // LANES:OFF_KS // LANES].reshape(
        2, NSA_KV_HEADS, rows, CMP_STRIDE * HEAD_DIM)
    pos = jnp.stack([cmp_pos_k, cmp_pos_v]).astype(F32).reshape(2, 1, CMP_BLOCK * HEAD_DIM)
    w1 = jnp.stack([cmp_w1_k, cmp_w1_v]).astype(BF16)
    w2 = jnp.stack([cmp_w2_k, cmp_w2_v]).astype(BF16)
    kvc = _compress(kv_cmp, pos, w1, w2)
    o_cmp, sel, any_rows = _cmp_select(proj, kvc, _overlap_matrix(s, rows), nsa_c, _pick(s, 256))
    o_sel = _sel_attention(proj, sel, any_rows, nsa_c, nsa_qext, _pick(s, 512))
    o_nsa = _win_combine(proj, o_cmp, o_sel, gates, nsa_c)

    wo = w_out.astype(BF16)
    h = _mm2_res(o_diff, o_nsa, wo[:DIFF_WIDTH], wo[DIFF_WIDTH:], h)
    return _ffn(h, ffn2_norm, ffn2_w_gate, ffn2_w_up, ffn2_w_down)


def kernel(x, ffn1_norm, ffn1_w_gate, ffn1_w_up, ffn1_w_down, mix_norm, w_in, gate_bias, lambda_q1, lambda_k1, lambda_q2, lambda_k2, diff_norm, cmp_pos_k, cmp_w1_k, cmp_w2_k, cmp_pos_v, cmp_w1_v, cmp_w2_v, w_out, ffn2_norm, ffn2_w_gate, ffn2_w_up, ffn2_w_down, final_norm):
    b, s, d = x.shape
    per_layer = (ffn1_norm, ffn1_w_gate, ffn1_w_up, ffn1_w_down, mix_norm, w_in, gate_bias,
                 lambda_q1, lambda_k1, lambda_q2, lambda_k2, diff_norm,
                 cmp_pos_k, cmp_w1_k, cmp_w2_k, cmp_pos_v, cmp_w1_v, cmp_w2_v,
                 w_out, ffn2_norm, ffn2_w_gate, ffn2_w_up, ffn2_w_down)
    outs = []
    for bi in range(b):
        h = x.reshape(s, d) if b == 1 else x[bi]
        for layer in range(DEPTH):
            h = _layer(h, layer, *[p[layer] for p in per_layer])
        outs.append(_rmsnorm(h, final_norm, x.dtype))
    return outs[0].reshape(1, s, d) if b == 1 else jnp.stack(outs)
```
